```python
import math
import jax, jax.numpy as jnp
from jax import lax
import numpy as np

D_MODEL = 2048
BATCH = 8
SEQ = 2048
DEPTH = 2

CHUNK = 64
N_META = 16
LEAD = 128
Q_BLOCK = 128
EPS = 1e-6

CONV_DIM = 512
CONV_W = 3
ML_HEADS = 4
ML_DK = 256
ML_DV = 256
ML_DIM = ML_HEADS * ML_DV
DA_HEADS = 4
DA_HD = 64
DA_VD = 2 * DA_HD
DA_DIM = DA_HEADS * DA_VD
D_FF = 5632
FFN_CONV_W = 3
N_BRANCH = 3

D_IN = 3 * CONV_DIM + 2 * ML_HEADS * ML_DK + 2 * ML_DIM + 2 * ML_HEADS + 4 * DA_HEADS * DA_HD + DA_DIM + N_BRANCH * D_MODEL

kernel_name = "hybrid_gated_conv_mlstm_diffattn_encoder"


def _in_split_sizes():
    return (CONV_DIM, CONV_DIM, CONV_DIM,
            ML_HEADS * ML_DK, ML_HEADS * ML_DK, ML_DIM, ML_DIM, ML_HEADS, ML_HEADS,
            DA_HEADS * 2 * DA_HD, DA_HEADS * 2 * DA_HD, DA_DIM,
            D_MODEL, D_MODEL, D_MODEL)


def rmsnorm(x, g):
    xf = x.astype(jnp.float32)
    y = xf * lax.rsqrt(jnp.mean(xf * xf, axis=-1, keepdims=True) + EPS)
    return (y * g.astype(jnp.float32)).astype(x.dtype)


def causal_dwconv(x, w):
    W = w.shape[0]
    L = x.shape[1]
    xp = jnp.pad(x, ((0, 0), (W - 1, 0), (0, 0)))
    y = xp[:, 0:L] * w[0]
    for j in range(1, W):
        y = y + xp[:, j:j + L] * w[j]
    return y


def _pad_lead(t, n):
    return jnp.pad(t, ((0, 0), (n, 0)) + ((0, 0),) * (t.ndim - 2))


def _chunk_ids(n):
    pos = jnp.arange(n)
    return jnp.where(pos < LEAD, 0, 1 + (pos - LEAD) // CHUNK)


def mlstm(q, k, v, i_pre, f_pre):
    padn = LEAD - N_META
    q, k, v, i_pre, f_pre = [_pad_lead(t, padn) for t in (q, k, v, i_pre, f_pre)]
    Bsz, Lp, H, dk = q.shape
    dv = v.shape[-1]
    nc = Lp // CHUNK
    valid = (jnp.arange(Lp) >= padn)[None, :, None]
    log_i = jnp.where(valid, i_pre, -jnp.inf)
    log_f = jnp.where(valid, jax.nn.log_sigmoid(f_pre), 0.0)
    k = k * (dk ** -0.5)

    def to_chunks(t):
        return t.reshape(Bsz, nc, CHUNK, H, t.shape[-1]).transpose(1, 0, 3, 2, 4)

    def gate_chunks(t):
        return t.reshape(Bsz, nc, CHUNK, H).transpose(1, 0, 3, 2)

    tri = jnp.tril(jnp.ones((CHUNK, CHUNK), dtype=bool))

    def step(carry, inp):
        C, n, m = carry
        qc, kc, vc, lic, lfc = inp
        b = jnp.cumsum(lfc, axis=-1)
        D = b[..., :, None] - b[..., None, :] + lic[..., None, :]
        D = jnp.where(tri, D, -jnp.inf)
        inter = b + m[..., None]
        m_t = jnp.maximum(jnp.max(D, axis=-1), inter)
        w = jnp.exp(D - m_t[..., None])
        g = jnp.exp(inter - m_t)
        s = jnp.einsum('bhtd,bhsd->bhts', qc, kc) * w
        num = g[..., None] * jnp.einsum('bhed,bhtd->bhte', C, qc) + jnp.einsum('bhts,bhse->bhte', s, vc)
        den = g * jnp.einsum('bhd,bhtd->bht', n, qc) + jnp.sum(s, axis=-1)
        h = num / jnp.maximum(jnp.abs(den), jnp.exp(-m_t))[..., None]
        bL = b[..., -1]
        ds = bL[..., None] - b + lic
        m_new = jnp.maximum(bL + m, jnp.max(ds, axis=-1))
        wk = jnp.exp(ds - m_new[..., None])
        gs = jnp.exp(bL + m - m_new)
        C = gs[..., None, None] * C + jnp.einsum('bhse,bhsd->bhed', vc * wk[..., None], kc)
        n = gs[..., None] * n + jnp.einsum('bhs,bhsd->bhd', wk, kc)
        return (C, n, m_new), h

    init = (jnp.zeros((Bsz, H, dv, dk), jnp.float32),
            jnp.zeros((Bsz, H, dk), jnp.float32),
            jnp.zeros((Bsz, H), jnp.float32))
    _, hs = lax.scan(step, init, (to_chunks(q), to_chunks(k), to_chunks(v),
                                  gate_chunks(log_i), gate_chunks(log_f)))
    hs = hs.transpose(1, 0, 3, 2, 4).reshape(Bsz, Lp, H, dv)
    return hs[:, padn:]


def diff_attention(q, k, v, lam):
    padn = LEAD - N_META
    q, k, v = [_pad_lead(t, padn) for t in (q, k, v)]
    Lp = q.shape[1]
    cid = _chunk_ids(Lp)
    valid = jnp.arange(Lp) >= padn
    scale = DA_HD ** -0.5
    outs = []
    for j in range(Lp // Q_BLOCK):
        q0 = j * Q_BLOCK
        kend = q0 + Q_BLOCK
        qb = q[:, q0:kend]
        kb = k[:, :kend]
        vb = v[:, :kend]
        s = jnp.einsum('bqhmd,bkhmd->bhmqk', qb, kb).astype(jnp.float32) * scale
        mask = (cid[None, :kend] <= cid[q0:kend, None]) & valid[None, :kend]
        p = jax.nn.softmax(jnp.where(mask, s, -jnp.inf), axis=-1)
        a = p[:, :, 0] - lam * p[:, :, 1]
        outs.append(jnp.einsum('bhqk,bkhe->bqhe', a.astype(v.dtype), vb))
    return jnp.concatenate(outs, axis=1)[:, padn:]


def mixer_block(u, layer, w_in, conv_a, b_if, ml_norm, da_lambda, da_norm, w_br_a, w_br_m, w_br_d, w_out):
    Bsz, L, _ = u.shape
    f32 = jnp.float32
    z = u @ w_in
    idx = np.cumsum(_in_split_sizes())[:-1].tolist()
    (a_x, a_b, a_c, m_q, m_k, m_v, m_o, m_i, m_f,
     d_q, d_k, d_v, g_a, g_m, g_d) = jnp.split(z, idx, axis=-1)
    y_a = (a_b * causal_dwconv(a_c * a_x, conv_a)) @ w_br_a
    hm = mlstm(m_q.reshape(Bsz, L, ML_HEADS, ML_DK).astype(f32),
               m_k.reshape(Bsz, L, ML_HEADS, ML_DK).astype(f32),
               m_v.reshape(Bsz, L, ML_HEADS, ML_DV).astype(f32),
               (m_i + b_if[0]).astype(f32), (m_f + b_if[1]).astype(f32))
    hm = rmsnorm(hm, ml_norm.reshape(ML_HEADS, ML_DV)) * jax.nn.sigmoid(m_o.reshape(Bsz, L, ML_HEADS, ML_DV).astype(f32))
    y_m = hm.reshape(Bsz, L, ML_DIM).astype(u.dtype) @ w_br_m
    lam_init = 0.8 - 0.6 * math.exp(-0.3 * layer)
    lf = da_lambda.astype(f32)
    lam = jnp.exp(jnp.sum(lf[0] * lf[1])) - jnp.exp(jnp.sum(lf[2] * lf[3])) + lam_init
    hd = diff_attention(d_q.reshape(Bsz, L, DA_HEADS, 2, DA_HD),
                        d_k.reshape(Bsz, L, DA_HEADS, 2, DA_HD),
                        d_v.reshape(Bsz, L, DA_HEADS, DA_VD), lam)
    hd = rmsnorm(hd, da_norm) * (1.0 - lam_init)
    y_d = hd.reshape(Bsz, L, DA_DIM) @ w_br_d
    merged = jax.nn.sigmoid(g_a) * y_a + jax.nn.sigmoid(g_m) * y_m + jax.nn.sigmoid(g_d) * y_d
    return merged @ w_out


def channel_mixer(u, w_up, conv_w, conv_b, w_down):
    a, b = jnp.split(u @ w_up, 2, axis=-1)
    a = causal_dwconv(a, conv_w) + conv_b
    return (jax.nn.gelu(a, approximate=False) * b) @ w_down


def setup_inputs(seed: int = 0) -> dict:
    key = jax.random.key(seed)
    ks = jax.random.split(key, 24)
    nrm = lambda k, shape, s: jax.random.normal(k, shape, jnp.float32) * s
    b_if = jnp.stack([nrm(ks[5], (DEPTH, ML_HEADS), 0.1),
                      jnp.linspace(3.0, 6.0, ML_HEADS)[None, :] + nrm(ks[6], (DEPTH, ML_HEADS), 0.1)], axis=1)
    return {
        "x": nrm(ks[0], (BATCH, SEQ, D_MODEL), 1.0),
        "meta": nrm(ks[1], (N_META, D_MODEL), 1.0),
        "norm_mix": 1.0 + nrm(ks[2], (DEPTH, D_MODEL), 0.02),
        "w_in": nrm(ks[3], (DEPTH, D_MODEL, D_IN), D_MODEL ** -0.5),
        "conv_a": nrm(ks[4], (DEPTH, CONV_W, CONV_DIM), CONV_W ** -0.5),
        "b_if": b_if,
        "ml_norm": 1.0 + nrm(ks[7], (DEPTH, ML_DIM), 0.02),
        "da_lambda": nrm(ks[8], (DEPTH, 4, DA_HD), 0.1),
        "da_norm": 1.0 + nrm(ks[9], (DEPTH, DA_VD), 0.02),
        "w_br_a": nrm(ks[10], (DEPTH, CONV_DIM, D_MODEL), CONV_DIM ** -0.5),
        "w_br_m": nrm(ks[11], (DEPTH, ML_DIM, D_MODEL), ML_DIM ** -0.5),
        "w_br_d": nrm(ks[12], (DEPTH, DA_DIM, D_MODEL), DA_DIM ** -0.5),
        "w_out": nrm(ks[13], (DEPTH, D_MODEL, D_MODEL), D_MODEL ** -0.5),
        "norm_ffn": 1.0 + nrm(ks[14], (DEPTH, D_MODEL), 0.02),
        "w_up": nrm(ks[15], (DEPTH, D_MODEL, 2 * D_FF), D_MODEL ** -0.5),
        "conv_ffn": nrm(ks[16], (DEPTH, FFN_CONV_W, D_FF), FFN_CONV_W ** -0.5),
        "conv_ffn_b": nrm(ks[17], (DEPTH, D_FF), 0.02),
        "w_down": nrm(ks[18], (DEPTH, D_FF, D_MODEL), D_FF ** -0.5),
        "norm_f": 1.0 + nrm(ks[19], (D_MODEL,), 0.02),
    }


def reference(x, meta, norm_mix, w_in, conv_a, b_if, ml_norm, da_lambda, da_norm, w_br_a, w_br_m, w_br_d, w_out, norm_ffn, w_up, conv_ffn, conv_ffn_b, w_down, norm_f):
    Bsz = x.shape[0]
    h = jnp.concatenate([jnp.broadcast_to(meta[None].astype(x.dtype), (Bsz, N_META, D_MODEL)), x], axis=1)
    for i in range(DEPTH):
        h = h + mixer_block(rmsnorm(h, norm_mix[i]), i, w_in[i], conv_a[i], b_if[i], ml_norm[i],
                            da_lambda[i], da_norm[i], w_br_a[i], w_br_m[i], w_br_d[i], w_out[i])
        h = h + channel_mixer(rmsnorm(h, norm_ffn[i]), w_up[i], conv_ffn[i], conv_ffn_b[i], w_down[i])
    return rmsnorm(h, norm_f)[:, N_META:]
```

```python
import functools
import math

import jax
import jax.numpy as jnp
from jax import lax
from jax.experimental import pallas as pl
from jax.experimental.pallas import tpu as pltpu

F32 = jnp.float32
BF16 = jnp.bfloat16

D_MODEL = 2048
SEQ = 2048
N_META = 16
LEAD = 128
PADN = LEAD - N_META
LP = LEAD + SEQ
CHUNK = 64
EPS = 1e-6

CONV_DIM = 512
ML_HEADS = 4
ML_DK = 256
ML_DV = 256
ML_DIM = ML_HEADS * ML_DV
DA_HEADS = 4
DA_HD = 64
DA_VD = 2 * DA_HD
DA_DIM = DA_HEADS * DA_VD
D_FF = 5632

Z_GATE = 0
Z_MQ = 3 * D_MODEL
Z_AX = Z_MQ + 4 * ML_DIM
Z_DQ = Z_AX + 3 * CONV_DIM
Z_N = Z_DQ + 3 * DA_DIM
IF_W = 128

ML_T = 128
ML_EXT = ML_DV + 128
HALO = 16
VMEM_LIMIT = 56 * 1024 * 1024


def _dot(a, b):
    return jnp.dot(a, b, preferred_element_type=F32)


def _dot_nt(a, b):
    return lax.dot_general(a, b, (((1,), (1,)), ((), ())), preferred_element_type=F32)


def _dot_tn(a, b):
    return lax.dot_general(a, b, (((0,), (0,)), ((), ())), preferred_element_type=F32)


def _rms(x, g):
    return x * lax.rsqrt(jnp.mean(x * x, axis=-1, keepdims=True) + EPS) * g


def _sigmoid(x):
    return 1.0 / (1.0 + jnp.exp(-x))


def _row_valid(r0, tm):
    pos = r0 % LP + lax.broadcasted_iota(jnp.int32, (tm, 1), 0)
    pos = jnp.where(pos >= LP, pos - LP, pos)
    return pos >= PADN


def _params(sem):
    return pltpu.CompilerParams(dimension_semantics=sem, vmem_limit_bytes=VMEM_LIMIT)


def _resident(shape, index_map):
    return pl.BlockSpec(shape, index_map, pipeline_mode=pl.Buffered(1))


def _embed_kernel(x_ref, meta_ref, g_ref, h_ref, u_ref):
    j = pl.program_id(1)

    @pl.when(j == 0)
    def _():
        rows = jnp.concatenate([jnp.zeros((PADN, D_MODEL), F32), meta_ref[...]], axis=0)
        h_ref[...] = rows
        u_ref[...] = _rms(rows, g_ref[...]).astype(BF16)

    @pl.when(j > 0)
    def _():
        rows = x_ref[...]
        h_ref[...] = rows
        u_ref[...] = _rms(rows, g_ref[...]).astype(BF16)


def _embed(x2d, meta, g, bsz):
    nb = LP // LEAD
    xb = SEQ // LEAD
    m = bsz * LP
    return pl.pallas_call(
        _embed_kernel,
        grid=(bsz, nb),
        in_specs=[
            pl.BlockSpec((LEAD, D_MODEL), lambda b, j: (b * xb + jnp.maximum(j - 1, 0), 0)),
            pl.BlockSpec((N_META, D_MODEL), lambda b, j: (0, 0)),
            pl.BlockSpec((1, D_MODEL), lambda b, j: (0, 0)),
        ],
        out_specs=[
            pl.BlockSpec((LEAD, D_MODEL), lambda b, j: (b * nb + j, 0)),
            pl.BlockSpec((LEAD, D_MODEL), lambda b, j: (b * nb + j, 0)),
        ],
        out_shape=[jax.ShapeDtypeStruct((m, D_MODEL), F32),
                   jax.ShapeDtypeStruct((m, D_MODEL), BF16)],
        compiler_params=_params(("parallel", "arbitrary")),
        name="embed_norm",
    )(x2d, meta, g)


def _inproj_kernel(u_ref, w_ref, wif_ref, z_ref, zif_ref):
    z_ref[...] = _dot(u_ref[...], w_ref[...]).astype(BF16)

    @pl.when(pl.program_id(1) == 0)
    def _():
        zif_ref[...] = _dot(u_ref[...], wif_ref[...])


def _inproj(u, w, wif, tm=1024, tn=1024):
    m = u.shape[0]
    return pl.pallas_call(
        _inproj_kernel,
        grid=(m // tm, Z_N // tn),
        in_specs=[
            pl.BlockSpec((tm, D_MODEL), lambda i, j: (i, 0)),
            pl.BlockSpec((D_MODEL, tn), lambda i, j: (0, j)),
            pl.BlockSpec((D_MODEL, IF_W), lambda i, j: (0, 0)),
        ],
        out_specs=[
            pl.BlockSpec((tm, tn), lambda i, j: (i, j)),
            pl.BlockSpec((tm, IF_W), lambda i, j: (i, 0)),
        ],
        out_shape=[jax.ShapeDtypeStruct((m, Z_N), BF16),
                   jax.ShapeDtypeStruct((m, IF_W), F32)],
        compiler_params=_params(("parallel", "arbitrary")),
        name="in_proj",
    )(u, w, wif)


def _mlstm_kernel(q_ref, k_ref, v_ref, o_ref, if_ref, bias_ref, g_ref, out_ref, s_ref, m_ref):
    c = pl.program_id(1)
    t = ML_T
    ninf = -jnp.inf

    @pl.when(c == 0)
    def _():
        s_ref[...] = jnp.zeros_like(s_ref)
        m_ref[...] = jnp.zeros_like(m_ref)

    gt = (if_ref[...] + bias_ref[...]).T[0:8, :]
    lane = lax.broadcasted_iota(jnp.int32, (8, t), 1)
    row8 = lax.broadcasted_iota(jnp.int32, (8, t), 0)
    valid = (c * t + lane) >= PADN
    log_sig = jnp.minimum(gt, 0.0) - jnp.log1p(jnp.exp(-jnp.abs(gt)))
    is_i = row8 < ML_HEADS
    lg = jnp.where(is_i, jnp.where(valid, gt, ninf), jnp.where(valid, log_sig, 0.0))
    cs = jnp.where(is_i, 0.0, lg)
    sh = 1
    while sh < t:
        cs = cs + jnp.where(lane >= sh, pltpu.roll(cs, sh, 1), 0.0)
        sh *= 2
    rowpack = jnp.where(is_i, lg, cs)
    colpack = jnp.concatenate([rowpack, jnp.zeros((t - 8, t), F32)], axis=0).T

    tri = lax.broadcasted_iota(jnp.int32, (t, t), 0) >= lax.broadcasted_iota(jnp.int32, (t, t), 1)
    ones_blk = jnp.ones((t, ML_EXT - ML_DV), BF16)

    for hd in range(ML_HEADS):
        sl = slice(hd * ML_DK, (hd + 1) * ML_DK)
        q = q_ref[:, sl]
        k = k_ref[:, sl] * jnp.asarray(ML_DK ** -0.5, BF16)
        vext = jnp.concatenate([v_ref[:, sl], ones_blk], axis=1)
        li_r = rowpack[hd:hd + 1, :]
        b_r = rowpack[ML_HEADS + hd:ML_HEADS + hd + 1, :]
        li_c = colpack[:, hd:hd + 1]
        b_c = colpack[:, ML_HEADS + hd:ML_HEADS + hd + 1]
        m_prev = m_ref[hd][:, 0:1]

        dmat = jnp.where(tri, b_c - b_r + li_r, ninf)
        inter = b_c + m_prev
        m_t = jnp.maximum(jnp.max(dmat, axis=1, keepdims=True), inter)
        w = jnp.exp(dmat - m_t)
        g = jnp.exp(inter - m_t)
        s = _dot_nt(q, k) * w
        tot = g * _dot(q, s_ref[hd].astype(BF16)) + _dot(s.astype(BF16), vext)
        num = tot[:, :ML_DV]
        den = tot[:, ML_DV:ML_DV + 1]
        hh = num / jnp.maximum(jnp.abs(den), jnp.exp(-m_t))
        hn = _rms(hh, g_ref[:, sl])
        out_ref[:, sl] = (hn * _sigmoid(o_ref[:, sl].astype(F32))).astype(BF16)

        b_last = b_r[:, t - 1:t]
        ds_r = b_last - b_r + li_r
        ds_c = b_last - b_c + li_c
        m_new = jnp.maximum(b_last + m_prev, jnp.max(ds_r, axis=1, keepdims=True))
        wk = jnp.exp(ds_c - m_new)
        gs = jnp.exp(b_last + m_prev - m_new)
        vw = (vext.astype(F32) * wk).astype(BF16)
        s_ref[hd] = gs * s_ref[hd] + _dot_tn(k, vw)
        m_ref[hd] = jnp.broadcast_to(m_new, (1, 128))


def _mlstm(z, zif, bias, g, bsz):
    nch = LP // ML_T
    m = bsz * LP
    qb = Z_MQ // ML_DIM

    def zspec(off):
        return pl.BlockSpec((ML_T, ML_DIM), lambda b, c: (b * nch + c, qb + off))

    return pl.pallas_call(
        _mlstm_kernel,
        grid=(bsz, nch),
        in_specs=[
            zspec(0), zspec(1), zspec(2), zspec(3),
            pl.BlockSpec((ML_T, IF_W), lambda b, c: (b * nch + c, 0)),
            pl.BlockSpec((1, IF_W), lambda b, c: (0, 0)),
            pl.BlockSpec((1, ML_DIM), lambda b, c: (0, 0)),
        ],
        out_specs=pl.BlockSpec((ML_T, ML_DIM), lambda b, c: (b * nch + c, 0)),
        out_shape=jax.ShapeDtypeStruct((m, ML_DIM), BF16),
        scratch_shapes=[pltpu.VMEM((ML_HEADS, ML_DK, ML_EXT), F32),
                        pltpu.VMEM((ML_HEADS, 1, 128), F32)],
        compiler_params=_params(("parallel", "arbitrary")),
        name="mlstm",
    )(z, z, z, z, zif, bias, g)


ATT_QB = 256


def _attn_allowed(r0, rows, c0, cols):
    qpos = r0 + lax.broadcasted_iota(jnp.int32, (rows, 1), 0)
    kpos = c0 + lax.broadcasted_iota(jnp.int32, (1, cols), 1)
    kend = jnp.maximum(LEAD, (qpos // CHUNK + 1) * CHUNK)
    return (kpos >= PADN) & (kpos < kend)


def _attn_mask(s, r0, rows, kj):
    pieces = []
    c = 0
    while c < kj:
        if c == 0 or c >= r0:
            width = LEAD
            pieces.append(jnp.where(_attn_allowed(r0, rows, c, width), s[:, c:c + width], -jnp.inf))
        else:
            width = r0 - c
            pieces.append(s[:, c:c + width])
        c += width
    return pieces[0] if len(pieces) == 1 else jnp.concatenate(pieces, axis=1)


def _attn_kernel(lam_init, q_ref, k_ref, v_ref, lam_ref, g_ref, out_ref):
    lf = lam_ref[...]
    lam = (jnp.exp(jnp.sum(lf[0:1] * lf[1:2], axis=1, keepdims=True))
           - jnp.exp(jnp.sum(lf[2:3] * lf[3:4], axis=1, keepdims=True)) + lam_init)
    scale = jnp.asarray(DA_HD ** -0.5, BF16)
    for r0 in range(0, LP, ATT_QB):
        rows = min(ATT_QB, LP - r0)
        kj = r0 + rows
        q = q_ref[r0:r0 + rows, :] * scale
        first_map = lax.broadcasted_iota(jnp.int32, (rows, DA_VD), 1) < DA_HD
        k = k_ref[0:kj, :]
        probs = []
        for mp in range(2):
            qm = jnp.where(first_map if mp == 0 else jnp.logical_not(first_map), q, jnp.zeros_like(q))
            s = _attn_mask(_dot_nt(qm, k), r0, rows, kj)
            p = jnp.exp(s - jnp.max(s, axis=1, keepdims=True))
            probs.append((p, jnp.sum(p, axis=1, keepdims=True)))
        a = probs[0][0] * (1.0 / probs[0][1]) - probs[1][0] * (lam / probs[1][1])
        o = _dot(a.astype(BF16), v_ref[0:kj, :])
        out_ref[r0:r0 + rows, :] = (_rms(o, g_ref[...]) * (1.0 - lam_init)).astype(BF16)


def _attn(z, lam_p, g, layer, bsz):
    lam_init = 0.8 - 0.6 * math.exp(-0.3 * layer)
    m = bsz * LP
    qb = Z_DQ // DA_VD

    def zspec(off):
        return pl.BlockSpec((LP, DA_VD), lambda b, h: (b, qb + off * DA_HEADS + h))

    return pl.pallas_call(
        functools.partial(_attn_kernel, lam_init),
        grid=(bsz, DA_HEADS),
        in_specs=[
            zspec(0), zspec(1), zspec(2),
            pl.BlockSpec((4, DA_HD), lambda b, h: (0, 0)),
            pl.BlockSpec((1, DA_VD), lambda b, h: (0, 0)),
        ],
        out_specs=pl.BlockSpec((LP, DA_VD), lambda b, h: (b, h)),
        out_shape=jax.ShapeDtypeStruct((m, DA_DIM), BF16),
        compiler_params=_params(("parallel", "parallel")),
        name="diff_attn",
    )(z, z, z, lam_p, g)


def _causal_conv3(prev, cur, w):
    cc = jnp.concatenate([prev, cur], axis=0)
    return (w[0:1] * pltpu.roll(cc, 2, 0)[HALO:] + w[1:2] * pltpu.roll(cc, 1, 0)[HALO:]
            + w[2:3] * cur)


def _merge_kernel(tm, ax_ref, ab_ref, ac_ref, axh_ref, ach_ref, hm_ref, hd_ref,
                  ga_ref, gm_ref, gd_ref, h_ref, cw_ref, wa_ref, wm_ref, wd_ref, wo_ref, g_ref,
                  hn_ref, u_ref):
    i = pl.program_id(0)
    cur = ac_ref[...].astype(F32) * ax_ref[...].astype(F32)
    prev = ach_ref[...].astype(F32) * axh_ref[...].astype(F32)
    a_act = (ab_ref[...].astype(F32) * _causal_conv3(prev, cur, cw_ref[...])).astype(BF16)
    merged = (_sigmoid(ga_ref[...].astype(F32)) * _dot(a_act, wa_ref[...])
              + _sigmoid(gm_ref[...].astype(F32)) * _dot(hm_ref[...], wm_ref[...])
              + _sigmoid(gd_ref[...].astype(F32)) * _dot(hd_ref[...], wd_ref[...]))
    hn = h_ref[...] + _dot(merged.astype(BF16), wo_ref[...])
    hn_ref[...] = hn
    u_ref[...] = jnp.where(_row_valid(i * tm, tm), _rms(hn, g_ref[...]), 0.0).astype(BF16)


def _merge(z, hm, hd, h, conv_a, w_br_a, w_br_m, w_br_d, w_out, g, tm=256):
    m = h.shape[0]
    axb = Z_AX // CONV_DIM
    hb = tm // HALO

    def halo(col):
        return pl.BlockSpec((HALO, CONV_DIM), lambda i: (jnp.maximum(i * hb - 1, 0), col))

    return pl.pallas_call(
        functools.partial(_merge_kernel, tm),
        grid=(m // tm,),
        in_specs=[
            pl.BlockSpec((tm, CONV_DIM), lambda i: (i, axb)),
            pl.BlockSpec((tm, CONV_DIM), lambda i: (i, axb + 1)),
            pl.BlockSpec((tm, CONV_DIM), lambda i: (i, axb + 2)),
            halo(axb), halo(axb + 2),
            pl.BlockSpec((tm, ML_DIM), lambda i: (i, 0)),
            pl.BlockSpec((tm, DA_DIM), lambda i: (i, 0)),
            pl.BlockSpec((tm, D_MODEL), lambda i: (i, 0)),
            pl.BlockSpec((tm, D_MODEL), lambda i: (i, 1)),
            pl.BlockSpec((tm, D_MODEL), lambda i: (i, 2)),
            pl.BlockSpec((tm, D_MODEL), lambda i: (i, 0)),
            _resident((3, CONV_DIM), lambda i: (0, 0)),
            _resident((CONV_DIM, D_MODEL), lambda i: (0, 0)),
            _resident((ML_DIM, D_MODEL), lambda i: (0, 0)),
            _resident((DA_DIM, D_MODEL), lambda i: (0, 0)),
            _resident((D_MODEL, D_MODEL), lambda i: (0, 0)),
            _resident((1, D_MODEL), lambda i: (0, 0)),
        ],
        out_specs=[
            pl.BlockSpec((tm, D_MODEL), lambda i: (i, 0)),
            pl.BlockSpec((tm, D_MODEL), lambda i: (i, 0)),
        ],
        out_shape=[jax.ShapeDtypeStruct((m, D_MODEL), F32),
                   jax.ShapeDtypeStruct((m, D_MODEL), BF16)],
        compiler_params=_params(("parallel",)),
        name="merge_out_proj",
    )(z, z, z, z, z, hm, hd, z, z, z, h, conv_a, w_br_a, w_br_m, w_br_d, w_out, g)


def _ffn_up_kernel(u_ref, uh_ref, wa_ref, wb_ref, cw_ref, cb_ref, out_ref):
    u = u_ref[...]
    a = _dot(u, wa_ref[...])
    a_prev = _dot(uh_ref[...], wa_ref[...])
    conv = _causal_conv3(a_prev, a, cw_ref[...]) + cb_ref[...]
    gelu = 0.5 * conv * (1.0 + lax.erf(conv * (2.0 ** -0.5)))
    out_ref[...] = (gelu * _dot(u, wb_ref[...])).astype(BF16)


def _ffn_up(u, w_up, conv_w, conv_b, tm=1088, tn=512):
    m = u.shape[0]
    nb = D_FF // tn
    hb = tm // HALO
    return pl.pallas_call(
        _ffn_up_kernel,
        grid=(m // tm, nb),
        in_specs=[
            pl.BlockSpec((tm, D_MODEL), lambda i, j: (i, 0)),
            pl.BlockSpec((HALO, D_MODEL), lambda i, j: (jnp.maximum(i * hb - 1, 0), 0)),
            pl.BlockSpec((D_MODEL, tn), lambda i, j: (0, j)),
            pl.BlockSpec((D_MODEL, tn), lambda i, j: (0, nb + j)),
            pl.BlockSpec((3, tn), lambda i, j: (0, j)),
            pl.BlockSpec((1, tn), lambda i, j: (0, j)),
        ],
        out_specs=pl.BlockSpec((tm, tn), lambda i, j: (i, j)),
        out_shape=jax.ShapeDtypeStruct((m, D_FF), BF16),
        compiler_params=_params(("parallel", "arbitrary")),
        name="ffn_up",
    )(u, u, w_up, w_up, conv_w, conv_b)


def _ffn_down_kernel(tm, final, act_ref, w_ref, h_ref, g_ref, *out_refs):
    i = pl.program_id(0)
    hn = h_ref[...] + _dot(act_ref[...], w_ref[...])
    y = _rms(hn, g_ref[...])
    if final:
        out_refs[0][...] = y
    else:
        out_refs[0][...] = hn
        out_refs[1][...] = jnp.where(_row_valid(i * tm, tm), y, 0.0).astype(BF16)


def _ffn_down(act, w_down, h, g, final, tm=256):
    m = h.shape[0]
    row = pl.BlockSpec((tm, D_MODEL), lambda i: (i, 0))
    if final:
        out_specs = [row]
        out_shape = [jax.ShapeDtypeStruct((m, D_MODEL), F32)]
    else:
        out_specs = [row, row]
        out_shape = [jax.ShapeDtypeStruct((m, D_MODEL), F32),
                     jax.ShapeDtypeStruct((m, D_MODEL), BF16)]
    return pl.pallas_call(
        functools.partial(_ffn_down_kernel, tm, final),
        grid=(m // tm,),
        in_specs=[
            pl.BlockSpec((tm, D_FF), lambda i: (i, 0)),
            _resident((D_FF, D_MODEL), lambda i: (0, 0)),
            row,
            _resident((1, D_MODEL), lambda i: (0, 0)),
        ],
        out_specs=out_specs,
        out_shape=out_shape,
        compiler_params=_params(("parallel",)),
        name="ffn_down_final" if final else "ffn_down",
    )(act, w_down, h, g)


def _permute_w_in(w):
    a_end = 3 * CONV_DIM
    m_end = a_end + 4 * ML_DIM
    if_end = m_end + 2 * ML_HEADS
    d_end = if_end + 3 * DA_DIM
    wz = jnp.concatenate([w[:, d_end:], w[:, a_end:m_end], w[:, :a_end], w[:, if_end:d_end]], axis=1)
    wif = jnp.pad(w[:, m_end:if_end], ((0, 0), (0, IF_W - 2 * ML_HEADS)))
    return wz.astype(BF16), wif.astype(BF16)


def kernel(x, meta, norm_mix, w_in, conv_a, b_if, ml_norm, da_lambda, da_norm, w_br_a, w_br_m, w_br_d,
           w_out, norm_ffn, w_up, conv_ffn, conv_ffn_b, w_down, norm_f):
    bsz, seq, d = x.shape
    assert (seq, d) == (SEQ, D_MODEL)
    depth = w_in.shape[0]
    h, u = _embed(x.reshape(bsz * SEQ, D_MODEL), meta, norm_mix[0][None], bsz)
    for i in range(depth):
        wz, wif = _permute_w_in(w_in[i])
        z, zif = _inproj(u, wz, wif)
        bias = jnp.pad(b_if[i].reshape(1, 2 * ML_HEADS), ((0, 0), (0, IF_W - 2 * ML_HEADS)))
        hm = _mlstm(z, zif, bias, ml_norm[i][None], bsz)
        hd = _attn(z, da_lambda[i], da_norm[i][None], i, bsz)
        h, u = _merge(z, hm, hd, h, conv_a[i], w_br_a[i].astype(BF16), w_br_m[i].astype(BF16),
                      w_br_d[i].astype(BF16), w_out[i].astype(BF16), norm_ffn[i][None])
        act = _ffn_up(u, w_up[i].astype(BF16), conv_ffn[i], conv_ffn_b[i][None])
        final = i == depth - 1
        g_next = norm_f if final else norm_mix[i + 1]
        outs = _ffn_down(act, w_down[i].astype(BF16), h, g_next[None], final)
        if final:
            y = outs[0]
        else:
            h, u = outs
    return y.reshape(bsz, LP, D_MODEL)[:, LEAD:]
```

```python
import functools
import math

import jax
import jax.numpy as jnp
from jax import lax
from jax.experimental import pallas as pl
from jax.experimental.pallas import tpu as pltpu

F32 = jnp.float32
BF16 = jnp.bfloat16

D_MODEL = 2048
SEQ = 2048
N_META = 16
LEAD = 128
PADN = LEAD - N_META
LP = LEAD + SEQ
CHUNK = 64
EPS = 1e-6

CONV_DIM = 512
ML_HEADS = 4
ML_DK = 256
ML_DV = 256
ML_DIM = ML_HEADS * ML_DV
DA_HEADS = 4
DA_HD = 64
DA_VD = 2 * DA_HD
DA_DIM = DA_HEADS * DA_VD
D_FF = 5632

Z_GATE = 0
Z_MQ = 3 * D_MODEL
Z_AX = Z_MQ + 4 * ML_DIM
Z_DQ = Z_AX + 3 * CONV_DIM
Z_N = Z_DQ + 3 * DA_DIM
IF_W = 128

ML_T = 128
ML_EXT = ML_DV + 128
HALO = 16
VMEM_LIMIT = 56 * 1024 * 1024


def _dot(a, b):
    return jnp.dot(a, b, preferred_element_type=F32)


def _dot_nt(a, b):
    return lax.dot_general(a, b, (((1,), (1,)), ((), ())), preferred_element_type=F32)


def _dot_tn(a, b):
    return lax.dot_general(a, b, (((0,), (0,)), ((), ())), preferred_element_type=F32)


def _rms(x, g):
    return x * lax.rsqrt(jnp.mean(x * x, axis=-1, keepdims=True) + EPS) * g


def _sigmoid(x):
    return 1.0 / (1.0 + jnp.exp(-x))


def _row_valid(r0, tm):
    pos = r0 % LP + lax.broadcasted_iota(jnp.int32, (tm, 1), 0)
    pos = jnp.where(pos >= LP, pos - LP, pos)
    return pos >= PADN


def _params(sem):
    return pltpu.CompilerParams(dimension_semantics=sem, vmem_limit_bytes=VMEM_LIMIT)


def _resident(shape, index_map):
    return pl.BlockSpec(shape, index_map, pipeline_mode=pl.Buffered(1))


EMB_T = LP // 4


def _embed_kernel(x_ref, meta_ref, g_ref, h_ref, u_ref):
    j = pl.program_id(1)

    @pl.when(j == 0)
    def _():
        rows = jnp.concatenate([jnp.zeros((PADN, D_MODEL), F32), meta_ref[...],
                                x_ref[0:EMB_T - LEAD, :]], axis=0)
        h_ref[...] = rows
        u_ref[...] = _rms(rows, g_ref[...]).astype(BF16)

    @pl.when(j > 0)
    def _():
        rows = x_ref[...]
        h_ref[...] = rows
        u_ref[...] = _rms(rows, g_ref[...]).astype(BF16)


def _embed(x2d, meta, g, bsz):
    nb = LP // EMB_T
    m = bsz * LP
    return pl.pallas_call(
        _embed_kernel,
        grid=(bsz, nb),
        in_specs=[
            pl.BlockSpec((pl.Element(EMB_T), pl.Element(D_MODEL)),
                         lambda b, j: (pl.multiple_of(b * SEQ + jnp.maximum(j * EMB_T - LEAD, 0), 32), 0)),
            pl.BlockSpec((N_META, D_MODEL), lambda b, j: (0, 0)),
            pl.BlockSpec((1, D_MODEL), lambda b, j: (0, 0)),
        ],
        out_specs=[
            pl.BlockSpec((EMB_T, D_MODEL), lambda b, j: (b * nb + j, 0)),
            pl.BlockSpec((EMB_T, D_MODEL), lambda b, j: (b * nb + j, 0)),
        ],
        out_shape=[jax.ShapeDtypeStruct((m, D_MODEL), F32),
                   jax.ShapeDtypeStruct((m, D_MODEL), BF16)],
        compiler_params=_params(("parallel", "arbitrary")),
        name="embed_norm",
    )(x2d, meta, g)


def _inproj_kernel(u_ref, w_ref, wif_ref, z_ref, zif_ref):
    z_ref[...] = _dot(u_ref[...], w_ref[...]).astype(BF16)

    @pl.when(pl.program_id(1) == 0)
    def _():
        zif_ref[...] = _dot(u_ref[...], wif_ref[...])


def _inproj(u, w, wif, tm=1024, tn=1024):
    m = u.shape[0]
    return pl.pallas_call(
        _inproj_kernel,
        grid=(m // tm, Z_N // tn),
        in_specs=[
            pl.BlockSpec((tm, D_MODEL), lambda i, j: (i, 0)),
            pl.BlockSpec((D_MODEL, tn), lambda i, j: (0, j)),
            pl.BlockSpec((D_MODEL, IF_W), lambda i, j: (0, 0)),
        ],
        out_specs=[
            pl.BlockSpec((tm, tn), lambda i, j: (i, j)),
            pl.BlockSpec((tm, IF_W), lambda i, j: (i, 0)),
        ],
        out_shape=[jax.ShapeDtypeStruct((m, Z_N), BF16),
                   jax.ShapeDtypeStruct((m, IF_W), F32)],
        compiler_params=_params(("parallel", "arbitrary")),
        name="in_proj",
    )(u, w, wif)


def _mlstm_kernel(q_ref, k_ref, v_ref, o_ref, if_ref, bias_ref, g_ref, out_ref, s_ref, m_ref):
    c = pl.program_id(1)
    t = ML_T
    ninf = -jnp.inf

    @pl.when(c == 0)
    def _():
        s_ref[...] = jnp.zeros_like(s_ref)
        m_ref[...] = jnp.zeros_like(m_ref)

    gt = (if_ref[...] + bias_ref[...]).T[0:8, :]
    lane = lax.broadcasted_iota(jnp.int32, (8, t), 1)
    row8 = lax.broadcasted_iota(jnp.int32, (8, t), 0)
    valid = (c * t + lane) >= PADN
    log_sig = jnp.minimum(gt, 0.0) - jnp.log1p(jnp.exp(-jnp.abs(gt)))
    is_i = row8 < ML_HEADS
    lg = jnp.where(is_i, jnp.where(valid, gt, ninf), jnp.where(valid, log_sig, 0.0))
    cs = jnp.where(is_i, 0.0, lg)
    sh = 1
    while sh < t:
        cs = cs + jnp.where(lane >= sh, pltpu.roll(cs, sh, 1), 0.0)
        sh *= 2
    rowpack = jnp.where(is_i, lg, cs)
    colpack = jnp.concatenate([rowpack, jnp.zeros((t - 8, t), F32)], axis=0).T

    tri = lax.broadcasted_iota(jnp.int32, (t, t), 0) >= lax.broadcasted_iota(jnp.int32, (t, t), 1)
    ones_blk = jnp.ones((t, ML_EXT - ML_DV), BF16)

    for hd in range(ML_HEADS):
        sl = slice(hd * ML_DK, (hd + 1) * ML_DK)
        q = q_ref[:, sl]
        k = k_ref[:, sl] * jnp.asarray(ML_DK ** -0.5, BF16)
        vext = jnp.concatenate([v_ref[:, sl], ones_blk], axis=1)
        li_r = rowpack[hd:hd + 1, :]
        b_r = rowpack[ML_HEADS + hd:ML_HEADS + hd + 1, :]
        li_c = colpack[:, hd:hd + 1]
        b_c = colpack[:, ML_HEADS + hd:ML_HEADS + hd + 1]
        m_prev = m_ref[hd][:, 0:1]

        dmat = jnp.where(tri, b_c - b_r + li_r, ninf)
        inter = b_c + m_prev
        m_t = jnp.maximum(jnp.max(dmat, axis=1, keepdims=True), inter)
        w = jnp.exp(dmat - m_t)
        g = jnp.exp(inter - m_t)
        s = _dot_nt(q, k) * w
        tot = g * _dot(q, s_ref[hd].astype(BF16)) + _dot(s.astype(BF16), vext)
        num = tot[:, :ML_DV]
        den = tot[:, ML_DV:ML_DV + 1]
        hh = num / jnp.maximum(jnp.abs(den), jnp.exp(-m_t))
        hn = _rms(hh, g_ref[:, sl])
        out_ref[:, sl] = (hn * _sigmoid(o_ref[:, sl].astype(F32))).astype(BF16)

        b_last = b_r[:, t - 1:t]
        ds_r = b_last - b_r + li_r
        ds_c = b_last - b_c + li_c
        m_new = jnp.maximum(b_last + m_prev, jnp.max(ds_r, axis=1, keepdims=True))
        wk = jnp.exp(ds_c - m_new)
        gs = jnp.exp(b_last + m_prev - m_new)
        vw = (vext.astype(F32) * wk).astype(BF16)
        s_ref[hd] = gs * s_ref[hd] + _dot_tn(k, vw)
        m_ref[hd] = jnp.broadcast_to(m_new, (1, 128))


def _mlstm(z, zif, bias, g, bsz):
    nch = LP // ML_T
    m = bsz * LP
    qb = Z_MQ // ML_DIM

    def zspec(off):
        return pl.BlockSpec((ML_T, ML_DIM), lambda b, c: (b * nch + c, qb + off))

    return pl.pallas_call(
        _mlstm_kernel,
        grid=(bsz, nch),
        in_specs=[
            zspec(0), zspec(1), zspec(2), zspec(3),
            pl.BlockSpec((ML_T, IF_W), lambda b, c: (b * nch + c, 0)),
            pl.BlockSpec((1, IF_W), lambda b, c: (0, 0)),
            pl.BlockSpec((1, ML_DIM), lambda b, c: (0, 0)),
        ],
        out_specs=pl.BlockSpec((ML_T, ML_DIM), lambda b, c: (b * nch + c, 0)),
        out_shape=jax.ShapeDtypeStruct((m, ML_DIM), BF16),
        scratch_shapes=[pltpu.VMEM((ML_HEADS, ML_DK, ML_EXT), F32),
                        pltpu.VMEM((ML_HEADS, 1, 128), F32)],
        compiler_params=_params(("parallel", "arbitrary")),
        name="mlstm",
    )(z, z, z, z, zif, bias, g)


ATT_QB = 256


def _attn_allowed(r0, rows, c0, cols):
    qpos = r0 + lax.broadcasted_iota(jnp.int32, (rows, 1), 0)
    kpos = c0 + lax.broadcasted_iota(jnp.int32, (1, cols), 1)
    kend = jnp.maximum(LEAD, (qpos // CHUNK + 1) * CHUNK)
    return (kpos >= PADN) & (kpos < kend)


def _attn_mask(s, r0, rows, kj):
    pieces = []
    c = 0
    while c < kj:
        if c == 0 or c >= r0:
            width = LEAD
            pieces.append(jnp.where(_attn_allowed(r0, rows, c, width), s[:, c:c + width], -jnp.inf))
        else:
            width = r0 - c
            pieces.append(s[:, c:c + width])
        c += width
    return pieces[0] if len(pieces) == 1 else jnp.concatenate(pieces, axis=1)


def _attn_kernel(lam_init, q_ref, k_ref, v_ref, lam_ref, g_ref, out_ref):
    lf = lam_ref[...]
    lam = (jnp.exp(jnp.sum(lf[0:1] * lf[1:2], axis=1, keepdims=True))
           - jnp.exp(jnp.sum(lf[2:3] * lf[3:4], axis=1, keepdims=True)) + lam_init)
    scale = jnp.asarray(DA_HD ** -0.5, BF16)
    for r0 in range(0, LP, ATT_QB):
        rows = min(ATT_QB, LP - r0)
        kj = r0 + rows
        q = q_ref[r0:r0 + rows, :] * scale
        first_map = lax.broadcasted_iota(jnp.int32, (rows, DA_VD), 1) < DA_HD
        k = k_ref[0:kj, :]
        probs = []
        for mp in range(2):
            qm = jnp.where(first_map if mp == 0 else jnp.logical_not(first_map), q, jnp.zeros_like(q))
            s = _attn_mask(_dot_nt(qm, k), r0, rows, kj)
            p = jnp.exp(s - jnp.max(s, axis=1, keepdims=True))
            probs.append((p, jnp.sum(p, axis=1, keepdims=True)))
        a = probs[0][0] * (1.0 / probs[0][1]) - probs[1][0] * (lam / probs[1][1])
        o = _dot(a.astype(BF16), v_ref[0:kj, :])
        out_ref[r0:r0 + rows, :] = (_rms(o, g_ref[...]) * (1.0 - lam_init)).astype(BF16)


def _attn(z, lam_p, g, layer, bsz):
    lam_init = 0.8 - 0.6 * math.exp(-0.3 * layer)
    m = bsz * LP
    qb = Z_DQ // DA_VD

    def zspec(off):
        return pl.BlockSpec((LP, DA_VD), lambda b, h: (b, qb + off * DA_HEADS + h))

    return pl.pallas_call(
        functools.partial(_attn_kernel, lam_init),
        grid=(bsz, DA_HEADS),
        in_specs=[
            zspec(0), zspec(1), zspec(2),
            pl.BlockSpec((4, DA_HD), lambda b, h: (0, 0)),
            pl.BlockSpec((1, DA_VD), lambda b, h: (0, 0)),
        ],
        out_specs=pl.BlockSpec((LP, DA_VD), lambda b, h: (b, h)),
        out_shape=jax.ShapeDtypeStruct((m, DA_DIM), BF16),
        compiler_params=_params(("parallel", "parallel")),
        name="diff_attn",
    )(z, z, z, lam_p, g)


def _causal_conv3(prev, cur, w):
    cc = jnp.concatenate([prev, cur], axis=0)
    return (w[0:1] * pltpu.roll(cc, 2, 0)[HALO:] + w[1:2] * pltpu.roll(cc, 1, 0)[HALO:]
            + w[2:3] * cur)


def _merge_kernel(tm, ax_ref, ab_ref, ac_ref, axh_ref, ach_ref, hm_ref, hd_ref,
                  ga_ref, gm_ref, gd_ref, h_ref, cw_ref, wa_ref, wm_ref, wd_ref, wo_ref, g_ref,
                  hn_ref, u_ref):
    i = pl.program_id(0)
    cur = ac_ref[...].astype(F32) * ax_ref[...].astype(F32)
    prev = ach_ref[...].astype(F32) * axh_ref[...].astype(F32)
    a_act = (ab_ref[...].astype(F32) * _causal_conv3(prev, cur, cw_ref[...])).astype(BF16)
    merged = (_sigmoid(ga_ref[...].astype(F32)) * _dot(a_act, wa_ref[...])
              + _sigmoid(gm_ref[...].astype(F32)) * _dot(hm_ref[...], wm_ref[...])
              + _sigmoid(gd_ref[...].astype(F32)) * _dot(hd_ref[...], wd_ref[...]))
    hn = h_ref[...] + _dot(merged.astype(BF16), wo_ref[...])
    hn_ref[...] = hn
    u_ref[...] = jnp.where(_row_valid(i * tm, tm), _rms(hn, g_ref[...]), 0.0).astype(BF16)


def _merge(z, hm, hd, h, conv_a, w_br_a, w_br_m, w_br_d, w_out, g, tm=256):
    m = h.shape[0]
    axb = Z_AX // CONV_DIM
    hb = tm // HALO

    def halo(col):
        return pl.BlockSpec((HALO, CONV_DIM), lambda i: (jnp.maximum(i * hb - 1, 0), col))

    return pl.pallas_call(
        functools.partial(_merge_kernel, tm),
        grid=(m // tm,),
        in_specs=[
            pl.BlockSpec((tm, CONV_DIM), lambda i: (i, axb)),
            pl.BlockSpec((tm, CONV_DIM), lambda i: (i, axb + 1)),
            pl.BlockSpec((tm, CONV_DIM), lambda i: (i, axb + 2)),
            halo(axb), halo(axb + 2),
            pl.BlockSpec((tm, ML_DIM), lambda i: (i, 0)),
            pl.BlockSpec((tm, DA_DIM), lambda i: (i, 0)),
            pl.BlockSpec((tm, D_MODEL), lambda i: (i, 0)),
            pl.BlockSpec((tm, D_MODEL), lambda i: (i, 1)),
            pl.BlockSpec((tm, D_MODEL), lambda i: (i, 2)),
            pl.BlockSpec((tm, D_MODEL), lambda i: (i, 0)),
            _resident((3, CONV_DIM), lambda i: (0, 0)),
            _resident((CONV_DIM, D_MODEL), lambda i: (0, 0)),
            _resident((ML_DIM, D_MODEL), lambda i: (0, 0)),
            _resident((DA_DIM, D_MODEL), lambda i: (0, 0)),
            _resident((D_MODEL, D_MODEL), lambda i: (0, 0)),
            _resident((1, D_MODEL), lambda i: (0, 0)),
        ],
        out_specs=[
            pl.BlockSpec((tm, D_MODEL), lambda i: (i, 0)),
            pl.BlockSpec((tm, D_MODEL), lambda i: (i, 0)),
        ],
        out_shape=[jax.ShapeDtypeStruct((m, D_MODEL), F32),
                   jax.ShapeDtypeStruct((m, D_MODEL), BF16)],
        compiler_params=_params(("parallel",)),
        name="merge_out_proj",
    )(z, z, z, z, z, hm, hd, z, z, z, h, conv_a, w_br_a, w_br_m, w_br_d, w_out, g)


def _ffn_up_kernel(u_ref, uh_ref, wa_ref, wb_ref, cw_ref, cb_ref, out_ref, w_scr):
    @pl.when(pl.program_id(1) == 0)
    def _():
        w_scr[0] = wa_ref[...].astype(BF16)
        w_scr[1] = wb_ref[...].astype(BF16)

    u = u_ref[...]
    a = _dot(u, w_scr[0])
    a_prev = _dot(uh_ref[...], w_scr[0])
    conv = _causal_conv3(a_prev, a, cw_ref[...]) + cb_ref[...]
    gelu = 0.5 * conv * (1.0 + lax.erf(conv * (2.0 ** -0.5)))
    out_ref[...] = (gelu * _dot(u, w_scr[1])).astype(BF16)


def _ffn_up(u, w_up, layer, conv_w, conv_b, tm=1088, tn=512):
    m = u.shape[0]
    nb = D_FF // tn
    hb = tm // HALO
    return pl.pallas_call(
        _ffn_up_kernel,
        grid=(nb, m // tm),
        in_specs=[
            pl.BlockSpec((tm, D_MODEL), lambda j, i: (i, 0)),
            pl.BlockSpec((HALO, D_MODEL), lambda j, i: (jnp.maximum(i * hb - 1, 0), 0)),
            pl.BlockSpec((None, D_MODEL, tn), lambda j, i: (layer, 0, j)),
            pl.BlockSpec((None, D_MODEL, tn), lambda j, i: (layer, 0, nb + j)),
            pl.BlockSpec((3, tn), lambda j, i: (0, j)),
            pl.BlockSpec((1, tn), lambda j, i: (0, j)),
        ],
        out_specs=pl.BlockSpec((tm, tn), lambda j, i: (i, j)),
        out_shape=jax.ShapeDtypeStruct((m, D_FF), BF16),
        scratch_shapes=[pltpu.VMEM((2, D_MODEL, tn), BF16)],
        compiler_params=_params(("parallel", "arbitrary")),
        name="ffn_up",
    )(u, u, w_up, w_up, conv_w, conv_b)


def _ffn_down_kernel(tm, final, act_ref, w_ref, h_ref, g_ref, *out_refs):
    i = pl.program_id(0)
    hn = h_ref[...] + _dot(act_ref[...], w_ref[...])
    y = _rms(hn, g_ref[...])
    if final:
        out_refs[0][...] = y
    else:
        out_refs[0][...] = hn
        out_refs[1][...] = jnp.where(_row_valid(i * tm, tm), y, 0.0).astype(BF16)


def _ffn_down(act, w_down, h, g, final, tm=256):
    m = h.shape[0]
    row = pl.BlockSpec((tm, D_MODEL), lambda i: (i, 0))
    if final:
        m = m // LP * SEQ
        per_batch = SEQ // tm

        def stream_row(i):
            return pl.multiple_of(i * tm + LEAD * (i // per_batch + 1), LEAD)

        act_spec = pl.BlockSpec((pl.Element(tm), pl.Element(D_FF)), lambda i: (stream_row(i), 0))
        h_spec = pl.BlockSpec((pl.Element(tm), pl.Element(D_MODEL)), lambda i: (stream_row(i), 0))
        out_specs = [row]
        out_shape = [jax.ShapeDtypeStruct((m, D_MODEL), F32)]
    else:
        act_spec = pl.BlockSpec((tm, D_FF), lambda i: (i, 0))
        h_spec = row
        out_specs = [row, row]
        out_shape = [jax.ShapeDtypeStruct((m, D_MODEL), F32),
                     jax.ShapeDtypeStruct((m, D_MODEL), BF16)]
    return pl.pallas_call(
        functools.partial(_ffn_down_kernel, tm, final),
        grid=(m // tm,),
        in_specs=[
            act_spec,
            _resident((D_FF, D_MODEL), lambda i: (0, 0)),
            h_spec,
            _resident((1, D_MODEL), lambda i: (0, 0)),
        ],
        out_specs=out_specs,
        out_shape=out_shape,
        compiler_params=_params(("parallel",)),
        name="ffn_down_final" if final else "ffn_down",
    )(act, w_down, h, g)


def _permute_w_in(w):
    a_end = 3 * CONV_DIM
    m_end = a_end + 4 * ML_DIM
    if_end = m_end + 2 * ML_HEADS
    d_end = if_end + 3 * DA_DIM
    wz = jnp.concatenate([w[:, d_end:], w[:, a_end:m_end], w[:, :a_end], w[:, if_end:d_end]], axis=1)
    wif = jnp.pad(w[:, m_end:if_end], ((0, 0), (0, IF_W - 2 * ML_HEADS)))
    return wz.astype(BF16), wif.astype(BF16)


def kernel(x, meta, norm_mix, w_in, conv_a, b_if, ml_norm, da_lambda, da_norm, w_br_a, w_br_m, w_br_d,
           w_out, norm_ffn, w_up, conv_ffn, conv_ffn_b, w_down, norm_f):
    bsz, seq, d = x.shape
    assert (seq, d) == (SEQ, D_MODEL)
    depth = w_in.shape[0]
    h, u = _embed(x.reshape(bsz * SEQ, D_MODEL), meta, norm_mix[0][None], bsz)
    for i in range(depth):
        wz, wif = _permute_w_in(w_in[i])
        z, zif = _inproj(u, wz, wif)
        bias = jnp.pad(b_if[i].reshape(1, 2 * ML_HEADS), ((0, 0), (0, IF_W - 2 * ML_HEADS)))
        hm = _mlstm(z, zif, bias, ml_norm[i][None], bsz)
        hd = _attn(z, da_lambda[i], da_norm[i][None], i, bsz)
        h, u = _merge(z, hm, hd, h, conv_a[i], w_br_a[i].astype(BF16), w_br_m[i].astype(BF16),
                      w_br_d[i].astype(BF16), w_out[i].astype(BF16), norm_ffn[i][None])
        act = _ffn_up(u, w_up, i, conv_ffn[i], conv_ffn_b[i][None])
        final = i == depth - 1
        g_next = norm_f if final else norm_mix[i + 1]
        outs = _ffn_down(act, w_down[i].astype(BF16), h, g_next[None], final)
        if final:
            y = outs[0]
        else:
            h, u = outs
    return y.reshape(bsz, SEQ, D_MODEL)
```

```python
import functools
import math

import jax
import jax.numpy as jnp
from jax import lax
from jax.experimental import pallas as pl
from jax.experimental.pallas import tpu as pltpu

F32 = jnp.float32
BF16 = jnp.bfloat16

D_MODEL = 2048
SEQ = 2048
N_META = 16
L = N_META + SEQ
CHUNK = 64
EPS = 1e-6

CONV_DIM = 512
ML_HEADS = 4
ML_DK = 256
ML_DV = 256
ML_DIM = ML_HEADS * ML_DV
DA_HEADS = 4
DA_HD = 64
DA_VD = 2 * DA_HD
DA_DIM = DA_HEADS * DA_VD
D_FF = 5632

Z_GATE = 0
Z_MQ = 3 * D_MODEL
Z_AX = Z_MQ + 4 * ML_DIM
Z_DQ = Z_AX + 3 * CONV_DIM
Z_N = Z_DQ + 3 * DA_DIM
IF_W = 128

ML_T = 128
ML_EXT = ML_DV + 128
HALO = 16
ATT_QB = 256
EMB_T = L // 3
VMEM_LIMIT = 56 * 1024 * 1024


def _dot(a, b):
    return jnp.dot(a, b, preferred_element_type=F32)


def _dot_nt(a, b):
    return lax.dot_general(a, b, (((1,), (1,)), ((), ())), preferred_element_type=F32)


def _dot_tn(a, b):
    return lax.dot_general(a, b, (((0,), (0,)), ((), ())), preferred_element_type=F32)


def _rms(x, g):
    return x * lax.rsqrt(jnp.mean(x * x, axis=-1, keepdims=True) + EPS) * g


def _sigmoid(x):
    return 1.0 / (1.0 + jnp.exp(-x))


def _pos_in_batch(r0, tm):
    pos = r0 % L + lax.broadcasted_iota(jnp.int32, (tm, 1), 0)
    return jnp.where(pos >= L, pos - L, pos)


def _causal_conv3(prev, cur, w, pos):
    cc = jnp.concatenate([prev, cur], axis=0)
    x1 = jnp.where(pos >= 1, pltpu.roll(cc, 1, 0)[HALO:], 0.0)
    x2 = jnp.where(pos >= 2, pltpu.roll(cc, 2, 0)[HALO:], 0.0)
    return w[0:1] * x2 + w[1:2] * x1 + w[2:3] * cur


def _params(sem):
    return pltpu.CompilerParams(dimension_semantics=sem, vmem_limit_bytes=VMEM_LIMIT)


def _resident(shape, index_map):
    return pl.BlockSpec(shape, index_map, pipeline_mode=pl.Buffered(1))


def _embed_kernel(x_ref, meta_ref, g_ref, h_ref, u_ref):
    j = pl.program_id(1)

    @pl.when(j == 0)
    def _():
        rows = jnp.concatenate([meta_ref[...], x_ref[0:EMB_T - N_META, :]], axis=0)
        h_ref[...] = rows
        u_ref[...] = _rms(rows, g_ref[...]).astype(BF16)

    @pl.when(j > 0)
    def _():
        rows = x_ref[...]
        h_ref[...] = rows
        u_ref[...] = _rms(rows, g_ref[...]).astype(BF16)


def _embed(x2d, meta, g, bsz):
    nb = L // EMB_T
    m = bsz * L
    return pl.pallas_call(
        _embed_kernel,
        grid=(bsz, nb),
        in_specs=[
            pl.BlockSpec((pl.Element(EMB_T), pl.Element(D_MODEL)),
                         lambda b, j: (pl.multiple_of(b * SEQ + jnp.maximum(j * EMB_T - N_META, 0), 16), 0)),
            pl.BlockSpec((N_META, D_MODEL), lambda b, j: (0, 0)),
            pl.BlockSpec((1, D_MODEL), lambda b, j: (0, 0)),
        ],
        out_specs=[
            pl.BlockSpec((EMB_T, D_MODEL), lambda b, j: (b * nb + j, 0)),
            pl.BlockSpec((EMB_T, D_MODEL), lambda b, j: (b * nb + j, 0)),
        ],
        out_shape=[jax.ShapeDtypeStruct((m, D_MODEL), F32),
                   jax.ShapeDtypeStruct((m, D_MODEL), BF16)],
        compiler_params=_params(("parallel", "arbitrary")),
        name="embed_norm",
    )(x2d, meta, g)


def _inproj_kernel(u_ref, w_ref, wif_ref, z_ref, zif_ref):
    z_ref[...] = _dot(u_ref[...], w_ref[...]).astype(BF16)

    @pl.when(pl.program_id(1) == 0)
    def _():
        zif_ref[...] = _dot(u_ref[...], wif_ref[...])


def _inproj(u, w, wif, layer, tm=L, tn=1024):
    m = u.shape[0]
    return pl.pallas_call(
        _inproj_kernel,
        grid=(m // tm, Z_N // tn),
        in_specs=[
            pl.BlockSpec((tm, D_MODEL), lambda i, j: (i, 0)),
            pl.BlockSpec((None, D_MODEL, tn), lambda i, j: (layer, 0, j)),
            pl.BlockSpec((None, D_MODEL, IF_W), lambda i, j: (layer, 0, 0)),
        ],
        out_specs=[
            pl.BlockSpec((tm, tn), lambda i, j: (i, j)),
            pl.BlockSpec((tm, IF_W), lambda i, j: (i, 0)),
        ],
        out_shape=[jax.ShapeDtypeStruct((m, Z_N), BF16),
                   jax.ShapeDtypeStruct((m, IF_W), F32)],
        compiler_params=_params(("parallel", "arbitrary")),
        name="in_proj",
    )(u, w, wif)


def _mlstm_chunk(r0, first_valid, q_ref, k_ref, v_ref, o_ref, if_ref, bias_ref, g_ref, out_ref, s_ref, m_ref):
    t = ML_T
    ninf = -jnp.inf
    rows = pl.ds(r0, t)

    gt = (if_ref[rows, :] + bias_ref[...]).T[0:8, :]
    lane = lax.broadcasted_iota(jnp.int32, (8, t), 1)
    row8 = lax.broadcasted_iota(jnp.int32, (8, t), 0)
    log_sig = jnp.minimum(gt, 0.0) - jnp.log1p(jnp.exp(-jnp.abs(gt)))
    is_i = row8 < ML_HEADS
    lg = jnp.where(is_i, gt, log_sig)
    if first_valid:
        lg = jnp.where(lane >= first_valid, lg, jnp.where(is_i, ninf, 0.0))
    cs = jnp.where(is_i, 0.0, lg)
    sh = 1
    while sh < t:
        cs = cs + jnp.where(lane >= sh, pltpu.roll(cs, sh, 1), 0.0)
        sh *= 2
    rowpack = jnp.where(is_i, lg, cs)
    colpack = jnp.concatenate([rowpack, jnp.zeros((t - 8, t), F32)], axis=0).T

    tri = lax.broadcasted_iota(jnp.int32, (t, t), 0) >= lax.broadcasted_iota(jnp.int32, (t, t), 1)
    ones_blk = jnp.ones((t, ML_EXT - ML_DV), BF16)

    for hd in range(ML_HEADS):
        sl = slice(hd * ML_DK, (hd + 1) * ML_DK)
        q = q_ref[rows, sl]
        k = k_ref[rows, sl] * jnp.asarray(ML_DK ** -0.5, BF16)
        vext = jnp.concatenate([v_ref[rows, sl], ones_blk], axis=1)
        li_r = rowpack[hd:hd + 1, :]
        b_r = rowpack[ML_HEADS + hd:ML_HEADS + hd + 1, :]
        li_c = colpack[:, hd:hd + 1]
        b_c = colpack[:, ML_HEADS + hd:ML_HEADS + hd + 1]
        m_prev = m_ref[hd][:, 0:1]

        dmat = jnp.where(tri, b_c - b_r + li_r, ninf)
        inter = b_c + m_prev
        m_t = jnp.maximum(jnp.max(dmat, axis=1, keepdims=True), inter)
        w = jnp.exp(dmat - m_t)
        g = jnp.exp(inter - m_t)
        s = _dot_nt(q, k) * w
        tot = g * _dot(q, s_ref[hd].astype(BF16)) + _dot(s.astype(BF16), vext)
        num = tot[:, :ML_DV]
        den = tot[:, ML_DV:ML_DV + 1]
        hh = num / jnp.maximum(jnp.abs(den), jnp.exp(-m_t))
        res = (_rms(hh, g_ref[:, sl]) * _sigmoid(o_ref[rows, sl].astype(F32))).astype(BF16)
        if first_valid:
            out_ref[pl.ds(r0 + first_valid, t - first_valid), sl] = res[first_valid:, :]
        else:
            out_ref[rows, sl] = res

        b_last = b_r[:, t - 1:t]
        ds_r = b_last - b_r + li_r
        ds_c = b_last - b_c + li_c
        m_new = jnp.maximum(b_last + m_prev, jnp.max(ds_r, axis=1, keepdims=True))
        wk = jnp.exp(ds_c - m_new)
        gs = jnp.exp(b_last + m_prev - m_new)
        vw = (vext.astype(F32) * wk).astype(BF16)
        s_ref[hd] = gs * s_ref[hd] + _dot_tn(k, vw)
        m_ref[hd] = jnp.broadcast_to(m_new, (1, 128))


def _mlstm_kernel(*refs):
    s_ref, m_ref = refs[-2:]
    s_ref[...] = jnp.zeros_like(s_ref)
    m_ref[...] = jnp.zeros_like(m_ref)
    n_full = L // ML_T

    def body(c, carry):
        _mlstm_chunk(pl.multiple_of(c * ML_T, ML_T), 0, *refs)
        return carry

    lax.fori_loop(0, n_full, body, 0)
    _mlstm_chunk(L - ML_T, ML_T - (L - n_full * ML_T), *refs)


def _mlstm(z, zif, bias, g, bsz):
    m = bsz * L
    qb = Z_MQ // ML_DIM

    def zspec(off):
        return pl.BlockSpec((L, ML_DIM), lambda b: (b, qb + off))

    return pl.pallas_call(
        _mlstm_kernel,
        grid=(bsz,),
        in_specs=[
            zspec(0), zspec(1), zspec(2), zspec(3),
            pl.BlockSpec((L, IF_W), lambda b: (b, 0)),
            pl.BlockSpec((1, IF_W), lambda b: (0, 0)),
            pl.BlockSpec((1, ML_DIM), lambda b: (0, 0)),
        ],
        out_specs=pl.BlockSpec((L, ML_DIM), lambda b: (b, 0)),
        out_shape=jax.ShapeDtypeStruct((m, ML_DIM), BF16),
        scratch_shapes=[pltpu.VMEM((ML_HEADS, ML_DK, ML_EXT), F32),
                        pltpu.VMEM((ML_HEADS, 1, 128), F32)],
        compiler_params=_params(("parallel",)),
        name="mlstm",
    )(z, z, z, z, zif, bias, g)


def _attn_block(q, k_ref, v_ref, n_frames, frame0, lam, lam_init, g):
    rows = q.shape[0]
    ninf = -jnp.inf
    first_map = lax.broadcasted_iota(jnp.int32, (rows, DA_VD), 1) < DA_HD
    k_meta = k_ref[0:ML_T, :]
    meta_ok = lax.broadcasted_iota(jnp.int32, (1, ML_T), 1) < N_META
    if n_frames:
        k_fr = k_ref[N_META:N_META + n_frames, :]
        qf = frame0 + lax.broadcasted_iota(jnp.int32, (rows, 1), 0)
        kend = (qf // CHUNK + 1) * CHUNK
        kf = frame0 + lax.broadcasted_iota(jnp.int32, (1, n_frames - frame0), 1)
        diag_ok = kf < kend
    probs = []
    for mp in range(2):
        qm = jnp.where(first_map if mp == 0 else jnp.logical_not(first_map), q, jnp.zeros_like(q))
        pieces = [jnp.where(meta_ok, _dot_nt(qm, k_meta), ninf)]
        if n_frames:
            s_fr = _dot_nt(qm, k_fr)
            if frame0:
                pieces.append(s_fr[:, :frame0])
            pieces.append(jnp.where(diag_ok, s_fr[:, frame0:], ninf))
        s = pieces[0] if len(pieces) == 1 else jnp.concatenate(pieces, axis=1)
        p = jnp.exp(s - jnp.max(s, axis=1, keepdims=True))
        probs.append((p, jnp.sum(p, axis=1, keepdims=True)))
    a = (probs[0][0] * (1.0 / probs[0][1]) - probs[1][0] * (lam / probs[1][1])).astype(BF16)
    o = _dot(a[:, :ML_T], v_ref[0:ML_T, :])
    if n_frames:
        o = o + _dot(a[:, ML_T:], v_ref[N_META:N_META + n_frames, :])
    return (_rms(o, g) * (1.0 - lam_init)).astype(BF16)


def _attn_kernel(lam_init, q_ref, k_ref, v_ref, lam_ref, g_ref, out_ref):
    lf = lam_ref[...]
    lam = (jnp.exp(jnp.sum(lf[0:1] * lf[1:2], axis=1, keepdims=True))
           - jnp.exp(jnp.sum(lf[2:3] * lf[3:4], axis=1, keepdims=True)) + lam_init)
    scale = jnp.asarray(DA_HD ** -0.5, BF16)
    g = g_ref[...]
    out_ref[0:N_META, :] = _attn_block(q_ref[0:N_META, :] * scale, k_ref, v_ref, 0, 0, lam, lam_init, g)
    for f0 in range(0, SEQ, ATT_QB):
        r0 = N_META + f0
        q = q_ref[r0:r0 + ATT_QB, :] * scale
        out_ref[r0:r0 + ATT_QB, :] = _attn_block(q, k_ref, v_ref, f0 + ATT_QB, f0, lam, lam_init, g)


def _attn(z, lam_p, g, layer, bsz):
    lam_init = 0.8 - 0.6 * math.exp(-0.3 * layer)
    m = bsz * L
    qb = Z_DQ // DA_VD

    def zspec(off):
        return pl.BlockSpec((L, DA_VD), lambda b, h: (b, qb + off * DA_HEADS + h))

    return pl.pallas_call(
        functools.partial(_attn_kernel, lam_init),
        grid=(bsz, DA_HEADS),
        in_specs=[
            zspec(0), zspec(1), zspec(2),
            pl.BlockSpec((4, DA_HD), lambda b, h: (0, 0)),
            pl.BlockSpec((1, DA_VD), lambda b, h: (0, 0)),
        ],
        out_specs=pl.BlockSpec((L, DA_VD), lambda b, h: (b, h)),
        out_shape=jax.ShapeDtypeStruct((m, DA_DIM), BF16),
        compiler_params=_params(("parallel", "parallel")),
        name="diff_attn",
    )(z, z, z, lam_p, g)


MERGE_NC = 512


def _merge_kernel(tm, ax_ref, ab_ref, ac_ref, axh_ref, ach_ref, hm_ref, hd_ref,
                  ga_ref, gm_ref, gd_ref, h_ref, cw_ref, wa_ref, wm_ref, wd_ref, wo_ref, g_ref,
                  hn_ref, u_ref, mg_scr):
    pos = _pos_in_batch(pl.program_id(0) * tm, tm)
    cur = ac_ref[...].astype(F32) * ax_ref[...].astype(F32)
    prev = ach_ref[...].astype(F32) * axh_ref[...].astype(F32)
    a_act = (ab_ref[...].astype(F32) * _causal_conv3(prev, cur, cw_ref[...], pos)).astype(BF16)
    hm = hm_ref[...]
    hd = hd_ref[...]
    for c0 in range(0, D_MODEL, MERGE_NC):
        cs = slice(c0, c0 + MERGE_NC)
        merged = (_sigmoid(ga_ref[:, cs].astype(F32)) * _dot(a_act, wa_ref[:, cs])
                  + _sigmoid(gm_ref[:, cs].astype(F32)) * _dot(hm, wm_ref[:, cs])
                  + _sigmoid(gd_ref[:, cs].astype(F32)) * _dot(hd, wd_ref[:, cs]))
        mg_scr[:, cs] = merged.astype(BF16)
    hn = h_ref[...] + _dot(mg_scr[...], wo_ref[...])
    hn_ref[...] = hn
    u_ref[...] = _rms(hn, g_ref[...]).astype(BF16)


def _merge(z, hm, hd, h, layer, conv_a, w_br_a, w_br_m, w_br_d, w_out, g, tm=384):
    m = h.shape[0]
    axb = Z_AX // CONV_DIM
    hb = tm // HALO

    def halo(col):
        return pl.BlockSpec((HALO, CONV_DIM), lambda i: (jnp.maximum(i * hb - 1, 0), col))

    def weight(rows):
        return _resident((None, rows, D_MODEL), lambda i: (layer, 0, 0))

    return pl.pallas_call(
        functools.partial(_merge_kernel, tm),
        grid=(m // tm,),
        in_specs=[
            pl.BlockSpec((tm, CONV_DIM), lambda i: (i, axb)),
            pl.BlockSpec((tm, CONV_DIM), lambda i: (i, axb + 1)),
            pl.BlockSpec((tm, CONV_DIM), lambda i: (i, axb + 2)),
            halo(axb), halo(axb + 2),
            pl.BlockSpec((tm, ML_DIM), lambda i: (i, 0)),
            pl.BlockSpec((tm, DA_DIM), lambda i: (i, 0)),
            pl.BlockSpec((tm, D_MODEL), lambda i: (i, 0)),
            pl.BlockSpec((tm, D_MODEL), lambda i: (i, 1)),
            pl.BlockSpec((tm, D_MODEL), lambda i: (i, 2)),
            pl.BlockSpec((tm, D_MODEL), lambda i: (i, 0)),
            _resident((3, CONV_DIM), lambda i: (0, 0)),
            weight(CONV_DIM), weight(ML_DIM), weight(DA_DIM), weight(D_MODEL),
            _resident((1, D_MODEL), lambda i: (0, 0)),
        ],
        out_specs=[
            pl.BlockSpec((tm, D_MODEL), lambda i: (i, 0)),
            pl.BlockSpec((tm, D_MODEL), lambda i: (i, 0)),
        ],
        out_shape=[jax.ShapeDtypeStruct((m, D_MODEL), F32),
                   jax.ShapeDtypeStruct((m, D_MODEL), BF16)],
        scratch_shapes=[pltpu.VMEM((tm, D_MODEL), BF16)],
        compiler_params=_params(("parallel",)),
        name="merge_out_proj",
    )(z, z, z, z, z, hm, hd, z, z, z, h, conv_a, w_br_a, w_br_m, w_br_d, w_out, g)


def _ffn_up_kernel(tm, u_ref, uh_ref, wa_ref, wb_ref, cw_ref, cb_ref, out_ref, w_scr):
    @pl.when(pl.program_id(1) == 0)
    def _():
        w_scr[0] = wa_ref[...].astype(BF16)
        w_scr[1] = wb_ref[...].astype(BF16)

    pos = _pos_in_batch(pl.program_id(1) * tm, tm)
    u = u_ref[...]
    a = _dot(u, w_scr[0])
    a_prev = _dot(uh_ref[...], w_scr[0])
    conv = _causal_conv3(a_prev, a, cw_ref[...], pos) + cb_ref[...]
    gelu = 0.5 * conv * (1.0 + lax.erf(conv * (2.0 ** -0.5)))
    out_ref[...] = (gelu * _dot(u, w_scr[1])).astype(BF16)


def _ffn_up(u, w_up, layer, conv_w, conv_b, tm=2 * L // 3, tn=512):
    m = u.shape[0]
    nb = D_FF // tn
    hb = tm // HALO
    return pl.pallas_call(
        functools.partial(_ffn_up_kernel, tm),
        grid=(nb, m // tm),
        in_specs=[
            pl.BlockSpec((tm, D_MODEL), lambda j, i: (i, 0)),
            pl.BlockSpec((HALO, D_MODEL), lambda j, i: (jnp.maximum(i * hb - 1, 0), 0)),
            pl.BlockSpec((None, D_MODEL, tn), lambda j, i: (layer, 0, j)),
            pl.BlockSpec((None, D_MODEL, tn), lambda j, i: (layer, 0, nb + j)),
            pl.BlockSpec((3, tn), lambda j, i: (0, j)),
            pl.BlockSpec((1, tn), lambda j, i: (0, j)),
        ],
        out_specs=pl.BlockSpec((tm, tn), lambda j, i: (i, j)),
        out_shape=jax.ShapeDtypeStruct((m, D_FF), BF16),
        scratch_shapes=[pltpu.VMEM((2, D_MODEL, tn), BF16)],
        compiler_params=_params(("parallel", "arbitrary")),
        name="ffn_up",
    )(u, u, w_up, w_up, conv_w, conv_b)


def _ffn_down_kernel(final, act_ref, w_ref, h_ref, g_ref, *out_refs):
    hn = h_ref[...] + _dot(act_ref[...], w_ref[...])
    y = _rms(hn, g_ref[...])
    if final:
        out_refs[0][...] = y
    else:
        out_refs[0][...] = hn
        out_refs[1][...] = y.astype(BF16)


def _ffn_down(act, w_down, layer, h, g, final):
    m = h.shape[0]
    if final:
        tm = 256
        m = m // L * SEQ
        per_batch = SEQ // tm

        def stream_row(i):
            return pl.multiple_of(i * tm + N_META * (i // per_batch + 1), N_META)

        act_spec = pl.BlockSpec((pl.Element(tm), pl.Element(D_FF)), lambda i: (stream_row(i), 0))
        h_spec = pl.BlockSpec((pl.Element(tm), pl.Element(D_MODEL)), lambda i: (stream_row(i), 0))
        row = pl.BlockSpec((tm, D_MODEL), lambda i: (i, 0))
        out_specs = [row]
        out_shape = [jax.ShapeDtypeStruct((m, D_MODEL), F32)]
    else:
        tm = 384
        row = pl.BlockSpec((tm, D_MODEL), lambda i: (i, 0))
        act_spec = pl.BlockSpec((tm, D_FF), lambda i: (i, 0))
        h_spec = row
        out_specs = [row, row]
        out_shape = [jax.ShapeDtypeStruct((m, D_MODEL), F32),
                     jax.ShapeDtypeStruct((m, D_MODEL), BF16)]
    return pl.pallas_call(
        functools.partial(_ffn_down_kernel, final),
        grid=(m // tm,),
        in_specs=[
            act_spec,
            _resident((None, D_FF, D_MODEL), lambda i: (layer, 0, 0)),
            h_spec,
            _resident((1, D_MODEL), lambda i: (0, 0)),
        ],
        out_specs=out_specs,
        out_shape=out_shape,
        compiler_params=_params(("parallel",)),
        name="ffn_down_final" if final else "ffn_down",
    )(act, w_down, h, g)


def _permute_w_in(w):
    a_end = 3 * CONV_DIM
    m_end = a_end + 4 * ML_DIM
    if_end = m_end + 2 * ML_HEADS
    d_end = if_end + 3 * DA_DIM
    wz = jnp.concatenate([w[..., d_end:], w[..., a_end:m_end], w[..., :a_end], w[..., if_end:d_end]], axis=-1)
    wif = jnp.pad(w[..., m_end:if_end], ((0, 0), (0, 0), (0, IF_W - 2 * ML_HEADS)))
    return wz.astype(BF16), wif.astype(BF16)


def kernel(x, meta, norm_mix, w_in, conv_a, b_if, ml_norm, da_lambda, da_norm, w_br_a, w_br_m, w_br_d,
           w_out, norm_ffn, w_up, conv_ffn, conv_ffn_b, w_down, norm_f):
    bsz, seq, d = x.shape
    assert (seq, d) == (SEQ, D_MODEL)
    depth = w_in.shape[0]
    wz, wif = _permute_w_in(w_in)
    wa, wm, wd, wo, wdn = (w.astype(BF16) for w in (w_br_a, w_br_m, w_br_d, w_out, w_down))
    h, u = _embed(x.reshape(bsz * SEQ, D_MODEL), meta, norm_mix[0][None], bsz)
    for i in range(depth):
        z, zif = _inproj(u, wz, wif, i)
        bias = jnp.pad(b_if[i].reshape(1, 2 * ML_HEADS), ((0, 0), (0, IF_W - 2 * ML_HEADS)))
        hm = _mlstm(z, zif, bias, ml_norm[i][None], bsz)
        hd = _attn(z, da_lambda[i], da_norm[i][None], i, bsz)
        h, u = _merge(z, hm, hd, h, i, conv_a[i], wa, wm, wd, wo, norm_ffn[i][None])
        act = _ffn_up(u, w_up, i, conv_ffn[i], conv_ffn_b[i][None])
        final = i == depth - 1
        g_next = norm_f if final else norm_mix[i + 1]
        outs = _ffn_down(act, wdn, i, h, g_next[None], final)
        if final:
            y = outs[0]
        else:
            h, u = outs
    return y.reshape(bsz, SEQ, D_MODEL)
```

```python
import functools
import math

import jax
import jax.numpy as jnp
from jax import lax
from jax.experimental import pallas as pl
from jax.experimental.pallas import tpu as pltpu

F32 = jnp.float32
BF16 = jnp.bfloat16

D_MODEL = 2048
SEQ = 2048
N_META = 16
L = N_META + SEQ
CHUNK = 64
EPS = 1e-6

CONV_DIM = 512
ML_HEADS = 4
ML_DK = 256
ML_DV = 256
ML_DIM = ML_HEADS * ML_DV
DA_HEADS = 4
DA_HD = 64
DA_VD = 2 * DA_HD
DA_DIM = DA_HEADS * DA_VD
D_FF = 5632

Z_GATE = 0
Z_MQ = 3 * D_MODEL
Z_AX = Z_MQ + 4 * ML_DIM
Z_DQ = Z_AX + 3 * CONV_DIM
Z_N = Z_DQ + 3 * DA_DIM
IF_W = 128

ML_T = 128
ML_EXT = ML_DV + 128
HALO = 16
ATT_QB = 256
EMB_T = L // 3
VMEM_LIMIT = 56 * 1024 * 1024


def _dot(a, b):
    return jnp.dot(a, b, preferred_element_type=F32)


def _dot_nt(a, b):
    return lax.dot_general(a, b, (((1,), (1,)), ((), ())), preferred_element_type=F32)


def _dot_tn(a, b):
    return lax.dot_general(a, b, (((0,), (0,)), ((), ())), preferred_element_type=F32)


def _rms(x, g):
    return x * lax.rsqrt(jnp.mean(x * x, axis=-1, keepdims=True) + EPS) * g


def _sigmoid(x):
    return 1.0 / (1.0 + jnp.exp(-x))


def _pos_in_batch(r0, tm):
    pos = r0 % L + lax.broadcasted_iota(jnp.int32, (tm, 1), 0)
    return jnp.where(pos >= L, pos - L, pos)


def _causal_conv3(prev, cur, w, pos):
    cc = jnp.concatenate([prev, cur], axis=0)
    x1 = jnp.where(pos >= 1, pltpu.roll(cc, 1, 0)[HALO:], 0.0)
    x2 = jnp.where(pos >= 2, pltpu.roll(cc, 2, 0)[HALO:], 0.0)
    return w[0:1] * x2 + w[1:2] * x1 + w[2:3] * cur


def _params(sem):
    return pltpu.CompilerParams(dimension_semantics=sem, vmem_limit_bytes=VMEM_LIMIT)


def _resident(shape, index_map):
    return pl.BlockSpec(shape, index_map, pipeline_mode=pl.Buffered(1))


def _embed_kernel(x_ref, meta_ref, g_ref, h_ref, u_ref):
    j = pl.program_id(1)

    @pl.when(j == 0)
    def _():
        rows = jnp.concatenate([meta_ref[...], x_ref[0:EMB_T - N_META, :]], axis=0)
        h_ref[...] = rows
        u_ref[...] = _rms(rows, g_ref[...]).astype(BF16)

    @pl.when(j > 0)
    def _():
        rows = x_ref[...]
        h_ref[...] = rows
        u_ref[...] = _rms(rows, g_ref[...]).astype(BF16)


def _embed(x2d, meta, g, bsz):
    nb = L // EMB_T
    m = bsz * L
    return pl.pallas_call(
        _embed_kernel,
        grid=(bsz, nb),
        in_specs=[
            pl.BlockSpec((pl.Element(EMB_T), pl.Element(D_MODEL)),
                         lambda b, j: (pl.multiple_of(b * SEQ + jnp.maximum(j * EMB_T - N_META, 0), 16), 0)),
            pl.BlockSpec((N_META, D_MODEL), lambda b, j: (0, 0)),
            pl.BlockSpec((1, D_MODEL), lambda b, j: (0, 0)),
        ],
        out_specs=[
            pl.BlockSpec((EMB_T, D_MODEL), lambda b, j: (b * nb + j, 0)),
            pl.BlockSpec((EMB_T, D_MODEL), lambda b, j: (b * nb + j, 0)),
        ],
        out_shape=[jax.ShapeDtypeStruct((m, D_MODEL), F32),
                   jax.ShapeDtypeStruct((m, D_MODEL), BF16)],
        compiler_params=_params(("parallel", "arbitrary")),
        name="embed_norm",
    )(x2d, meta, g)


W_A_END = 3 * CONV_DIM
W_M_END = W_A_END + 4 * ML_DIM
W_IF_END = W_M_END + 2 * ML_HEADS
W_D_END = W_IF_END + 3 * DA_DIM
W_SHIFT = W_IF_END - W_M_END
WP_TN = 512
WP_NB = WP_TN // 128 + 1


def _wprep_kernel(*refs):
    srcs, out_ref = refs[:WP_NB], refs[WP_NB]
    t = pl.program_id(1)
    n_gate, n_m, n_a = 3 * D_MODEL // WP_TN, 4 * ML_DIM // WP_TN, 3 * CONV_DIM // WP_TN
    shifted = jnp.logical_or(t < n_gate, t >= n_gate + n_m + n_a)

    @pl.when(shifted)
    def _():
        win = jnp.concatenate([r[...] for r in srcs], axis=1)
        out_ref[...] = pltpu.roll(win, win.shape[1] - W_SHIFT, 1)[:, :WP_TN].astype(BF16)

    @pl.when(jnp.logical_not(shifted))
    def _():
        out_ref[...] = jnp.concatenate([r[...] for r in srcs[:-1]], axis=1).astype(BF16)


def _wprep(w_in):
    depth = w_in.shape[0]
    n_gate, n_m, n_a = 3 * D_MODEL // WP_TN, 4 * ML_DIM // WP_TN, 3 * CONV_DIM // WP_TN
    per = WP_TN // 128

    def src_block(t):
        gate = (W_D_END - W_SHIFT) // 128 + per * t
        mls = W_A_END // 128 + per * (t - n_gate)
        cnv = per * (t - n_gate - n_m)
        att = W_M_END // 128 + per * (t - n_gate - n_m - n_a)
        return jnp.where(t < n_gate, gate,
                         jnp.where(t < n_gate + n_m, mls, jnp.where(t < n_gate + n_m + n_a, cnv, att)))

    def spec(k):
        return pl.BlockSpec((None, D_MODEL, 128), lambda l, t: (l, 0, src_block(t) + k))

    return pl.pallas_call(
        _wprep_kernel,
        grid=(depth, Z_N // WP_TN),
        in_specs=[spec(k) for k in range(WP_NB)],
        out_specs=pl.BlockSpec((None, D_MODEL, WP_TN), lambda l, t: (l, 0, t)),
        out_shape=jax.ShapeDtypeStruct((depth, D_MODEL, Z_N), BF16),
        compiler_params=_params(("parallel", "parallel")),
        name="w_in_prep",
    )(*([w_in] * WP_NB))


def _inproj_kernel(u_ref, w_ref, wif_ref, z_ref, zif_ref):
    z_ref[...] = _dot(u_ref[...], w_ref[...]).astype(BF16)

    @pl.when(pl.program_id(1) == 0)
    def _():
        zif_ref[...] = _dot(u_ref[...], wif_ref[...].astype(BF16))


def _inproj(u, w, w_in, layer, tm=L, tn=1024):
    m = u.shape[0]
    return pl.pallas_call(
        _inproj_kernel,
        grid=(m // tm, Z_N // tn),
        in_specs=[
            pl.BlockSpec((tm, D_MODEL), lambda i, j: (i, 0)),
            pl.BlockSpec((None, D_MODEL, tn), lambda i, j: (layer, 0, j)),
            pl.BlockSpec((None, D_MODEL, IF_W), lambda i, j: (layer, 0, W_M_END // IF_W)),
        ],
        out_specs=[
            pl.BlockSpec((tm, tn), lambda i, j: (i, j)),
            pl.BlockSpec((tm, IF_W), lambda i, j: (i, 0)),
        ],
        out_shape=[jax.ShapeDtypeStruct((m, Z_N), BF16),
                   jax.ShapeDtypeStruct((m, IF_W), F32)],
        compiler_params=_params(("parallel", "arbitrary")),
        name="in_proj",
    )(u, w, w_in)


def _mlstm_chunk(r0, first_valid, q_ref, k_ref, v_ref, o_ref, if_ref, bias_ref, g_ref, out_ref, s_ref, m_ref):
    t = ML_T
    ninf = -jnp.inf
    rows = pl.ds(r0, t)

    gt = (if_ref[rows, :] + bias_ref[...]).T[0:8, :]
    lane = lax.broadcasted_iota(jnp.int32, (8, t), 1)
    row8 = lax.broadcasted_iota(jnp.int32, (8, t), 0)
    log_sig = jnp.minimum(gt, 0.0) - jnp.log1p(jnp.exp(-jnp.abs(gt)))
    is_i = row8 < ML_HEADS
    lg = jnp.where(is_i, gt, log_sig)
    if first_valid:
        lg = jnp.where(lane >= first_valid, lg, jnp.where(is_i, ninf, 0.0))
    cs = jnp.where(is_i, 0.0, lg)
    sh = 1
    while sh < t:
        cs = cs + jnp.where(lane >= sh, pltpu.roll(cs, sh, 1), 0.0)
        sh *= 2
    rowpack = jnp.where(is_i, lg, cs)
    colpack = jnp.concatenate([rowpack, jnp.zeros((t - 8, t), F32)], axis=0).T

    tri = lax.broadcasted_iota(jnp.int32, (t, t), 0) >= lax.broadcasted_iota(jnp.int32, (t, t), 1)
    ones_blk = jnp.ones((t, ML_EXT - ML_DV), BF16)

    for hd in range(ML_HEADS):
        sl = slice(hd * ML_DK, (hd + 1) * ML_DK)
        q = q_ref[rows, sl]
        k = k_ref[rows, sl] * jnp.asarray(ML_DK ** -0.5, BF16)
        vext = jnp.concatenate([v_ref[rows, sl], ones_blk], axis=1)
        li_r = rowpack[hd:hd + 1, :]
        b_r = rowpack[ML_HEADS + hd:ML_HEADS + hd + 1, :]
        li_c = colpack[:, hd:hd + 1]
        b_c = colpack[:, ML_HEADS + hd:ML_HEADS + hd + 1]
        m_prev = m_ref[hd][:, 0:1]

        dmat = jnp.where(tri, b_c - b_r + li_r, ninf)
        inter = b_c + m_prev
        m_t = jnp.maximum(jnp.max(dmat, axis=1, keepdims=True), inter)
        w = jnp.exp(dmat - m_t)
        g = jnp.exp(inter - m_t)
        s = _dot_nt(q, k) * w
        tot = g * _dot(q, s_ref[hd].astype(BF16)) + _dot(s.astype(BF16), vext)
        num = tot[:, :ML_DV]
        den = tot[:, ML_DV:ML_DV + 1]
        hh = num / jnp.maximum(jnp.abs(den), jnp.exp(-m_t))
        res = (_rms(hh, g_ref[:, sl]) * _sigmoid(o_ref[rows, sl].astype(F32))).astype(BF16)
        if first_valid:
            out_ref[pl.ds(r0 + first_valid, t - first_valid), sl] = res[first_valid:, :]
        else:
            out_ref[rows, sl] = res

        b_last = b_r[:, t - 1:t]
        ds_r = b_last - b_r + li_r
        ds_c = b_last - b_c + li_c
        m_new = jnp.maximum(b_last + m_prev, jnp.max(ds_r, axis=1, keepdims=True))
        wk = jnp.exp(ds_c - m_new)
        gs = jnp.exp(b_last + m_prev - m_new)
        vw = (vext.astype(F32) * wk).astype(BF16)
        s_ref[hd] = gs * s_ref[hd] + _dot_tn(k, vw)
        m_ref[hd] = jnp.broadcast_to(m_new, (1, 128))


def _mlstm_kernel(*refs):
    s_ref, m_ref = refs[-2:]
    s_ref[...] = jnp.zeros_like(s_ref)
    m_ref[...] = jnp.zeros_like(m_ref)
    n_full = L // ML_T

    def body(c, carry):
        _mlstm_chunk(pl.multiple_of(c * ML_T, ML_T), 0, *refs)
        return carry

    lax.fori_loop(0, n_full, body, 0)
    _mlstm_chunk(L - ML_T, ML_T - (L - n_full * ML_T), *refs)


def _mlstm(z, zif, bias, g, bsz):
    m = bsz * L
    qb = Z_MQ // ML_DIM

    def zspec(off):
        return pl.BlockSpec((L, ML_DIM), lambda b: (b, qb + off))

    return pl.pallas_call(
        _mlstm_kernel,
        grid=(bsz,),
        in_specs=[
            zspec(0), zspec(1), zspec(2), zspec(3),
            pl.BlockSpec((L, IF_W), lambda b: (b, 0)),
            pl.BlockSpec((1, IF_W), lambda b: (0, 0)),
            pl.BlockSpec((1, ML_DIM), lambda b: (0, 0)),
        ],
        out_specs=pl.BlockSpec((L, ML_DIM), lambda b: (b, 0)),
        out_shape=jax.ShapeDtypeStruct((m, ML_DIM), BF16),
        scratch_shapes=[pltpu.VMEM((ML_HEADS, ML_DK, ML_EXT), F32),
                        pltpu.VMEM((ML_HEADS, 1, 128), F32)],
        compiler_params=_params(("parallel",)),
        name="mlstm",
    )(z, z, z, z, zif, bias, g)


def _attn_block(q, k_ref, v_ref, n_frames, frame0, lam, lam_init, g):
    rows = q.shape[0]
    ninf = -jnp.inf
    first_map = lax.broadcasted_iota(jnp.int32, (rows, DA_VD), 1) < DA_HD
    k_meta = k_ref[0:ML_T, :]
    meta_ok = lax.broadcasted_iota(jnp.int32, (1, ML_T), 1) < N_META
    if n_frames:
        k_fr = k_ref[N_META:N_META + n_frames, :]
        qf = frame0 + lax.broadcasted_iota(jnp.int32, (rows, 1), 0)
        kend = (qf // CHUNK + 1) * CHUNK
        kf = frame0 + lax.broadcasted_iota(jnp.int32, (1, n_frames - frame0), 1)
        diag_ok = kf < kend
    outs = []
    for mp in range(2):
        qm = jnp.where(first_map if mp == 0 else jnp.logical_not(first_map), q, jnp.zeros_like(q))
        pieces = [jnp.where(meta_ok, _dot_nt(qm, k_meta), ninf)]
        if n_frames:
            s_fr = _dot_nt(qm, k_fr)
            if frame0:
                pieces.append(s_fr[:, :frame0])
            pieces.append(jnp.where(diag_ok, s_fr[:, frame0:], ninf))
        s = pieces[0] if len(pieces) == 1 else jnp.concatenate(pieces, axis=1)
        p = jnp.exp2(s - jnp.max(s, axis=1, keepdims=True))
        denom = jnp.sum(p, axis=1, keepdims=True)
        p = p.astype(BF16)
        o = _dot(p[:, :ML_T], v_ref[0:ML_T, :])
        if n_frames:
            o = o + _dot(p[:, ML_T:], v_ref[N_META:N_META + n_frames, :])
        outs.append((o, denom))
    o = outs[0][0] * (1.0 / outs[0][1]) - outs[1][0] * (lam / outs[1][1])
    return (_rms(o, g) * (1.0 - lam_init)).astype(BF16)


def _attn_kernel(lam_init, q_ref, k_ref, v_ref, lam_ref, g_ref, out_ref):
    lf = lam_ref[...]
    lam = (jnp.exp(jnp.sum(lf[0:1] * lf[1:2], axis=1, keepdims=True))
           - jnp.exp(jnp.sum(lf[2:3] * lf[3:4], axis=1, keepdims=True)) + lam_init)
    scale = DA_HD ** -0.5 * math.log2(math.e)
    g = g_ref[...]

    def scaled(rows):
        return (q_ref[rows, :].astype(F32) * scale).astype(BF16)

    out_ref[0:N_META, :] = _attn_block(scaled(slice(0, N_META)), k_ref, v_ref, 0, 0, lam, lam_init, g)
    for f0 in range(0, SEQ, ATT_QB):
        rows = slice(N_META + f0, N_META + f0 + ATT_QB)
        out_ref[rows, :] = _attn_block(scaled(rows), k_ref, v_ref, f0 + ATT_QB, f0, lam, lam_init, g)


def _attn(z, lam_p, g, layer, bsz):
    lam_init = 0.8 - 0.6 * math.exp(-0.3 * layer)
    m = bsz * L
    qb = Z_DQ // DA_VD

    def zspec(off):
        return pl.BlockSpec((L, DA_VD), lambda b, h: (b, qb + off * DA_HEADS + h))

    return pl.pallas_call(
        functools.partial(_attn_kernel, lam_init),
        grid=(bsz, DA_HEADS),
        in_specs=[
            zspec(0), zspec(1), zspec(2),
            pl.BlockSpec((4, DA_HD), lambda b, h: (0, 0)),
            pl.BlockSpec((1, DA_VD), lambda b, h: (0, 0)),
        ],
        out_specs=pl.BlockSpec((L, DA_VD), lambda b, h: (b, h)),
        out_shape=jax.ShapeDtypeStruct((m, DA_DIM), BF16),
        compiler_params=_params(("parallel", "parallel")),
        name="diff_attn",
    )(z, z, z, lam_p, g)


MERGE_NC = 512


def _merge_kernel(tm, ax_ref, ab_ref, ac_ref, axh_ref, ach_ref, hm_ref, hd_ref,
                  ga_ref, gm_ref, gd_ref, h_ref, cw_ref, wa_ref, wm_ref, wd_ref, wo_ref, g_ref,
                  hn_ref, u_ref, mg_scr):
    pos = _pos_in_batch(pl.program_id(0) * tm, tm)
    cur = ac_ref[...].astype(F32) * ax_ref[...].astype(F32)
    prev = ach_ref[...].astype(F32) * axh_ref[...].astype(F32)
    a_act = (ab_ref[...].astype(F32) * _causal_conv3(prev, cur, cw_ref[...], pos)).astype(BF16)
    hm = hm_ref[...]
    hd = hd_ref[...]
    for c0 in range(0, D_MODEL, MERGE_NC):
        cs = slice(c0, c0 + MERGE_NC)
        merged = (_sigmoid(ga_ref[:, cs].astype(F32)) * _dot(a_act, wa_ref[:, cs])
                  + _sigmoid(gm_ref[:, cs].astype(F32)) * _dot(hm, wm_ref[:, cs])
                  + _sigmoid(gd_ref[:, cs].astype(F32)) * _dot(hd, wd_ref[:, cs]))
        mg_scr[:, cs] = merged.astype(BF16)
    hn = h_ref[...] + _dot(mg_scr[...], wo_ref[...])
    hn_ref[...] = hn
    u_ref[...] = _rms(hn, g_ref[...]).astype(BF16)


def _merge(z, hm, hd, h, layer, conv_a, w_br_a, w_br_m, w_br_d, w_out, g, tm=384):
    m = h.shape[0]
    axb = Z_AX // CONV_DIM
    hb = tm // HALO

    def halo(col):
        return pl.BlockSpec((HALO, CONV_DIM), lambda i: (jnp.maximum(i * hb - 1, 0), col))

    def weight(rows):
        return _resident((None, rows, D_MODEL), lambda i: (layer, 0, 0))

    return pl.pallas_call(
        functools.partial(_merge_kernel, tm),
        grid=(m // tm,),
        in_specs=[
            pl.BlockSpec((tm, CONV_DIM), lambda i: (i, axb)),
            pl.BlockSpec((tm, CONV_DIM), lambda i: (i, axb + 1)),
            pl.BlockSpec((tm, CONV_DIM), lambda i: (i, axb + 2)),
            halo(axb), halo(axb + 2),
            pl.BlockSpec((tm, ML_DIM), lambda i: (i, 0)),
            pl.BlockSpec((tm, DA_DIM), lambda i: (i, 0)),
            pl.BlockSpec((tm, D_MODEL), lambda i: (i, 0)),
            pl.BlockSpec((tm, D_MODEL), lambda i: (i, 1)),
            pl.BlockSpec((tm, D_MODEL), lambda i: (i, 2)),
            pl.BlockSpec((tm, D_MODEL), lambda i: (i, 0)),
            _resident((3, CONV_DIM), lambda i: (0, 0)),
            weight(CONV_DIM), weight(ML_DIM), weight(DA_DIM), weight(D_MODEL),
            _resident((1, D_MODEL), lambda i: (0, 0)),
        ],
        out_specs=[
            pl.BlockSpec((tm, D_MODEL), lambda i: (i, 0)),
            pl.BlockSpec((tm, D_MODEL), lambda i: (i, 0)),
        ],
        out_shape=[jax.ShapeDtypeStruct((m, D_MODEL), F32),
                   jax.ShapeDtypeStruct((m, D_MODEL), BF16)],
        scratch_shapes=[pltpu.VMEM((tm, D_MODEL), BF16)],
        compiler_params=_params(("parallel",)),
        name="merge_out_proj",
    )(z, z, z, z, z, hm, hd, z, z, z, h, conv_a, w_br_a, w_br_m, w_br_d, w_out, g)


FFN_ROWS = 4


def _row_blocks(tm, n):
    units = tm // HALO
    edges = [HALO * (units * k // n) for k in range(n + 1)]
    return list(zip(edges[:-1], edges[1:]))


def _ffn_up_kernel(tm, u_ref, uh_ref, wa_ref, wb_ref, cw_ref, cb_ref, out_ref, w_scr):
    @pl.when(pl.program_id(1) == 0)
    def _():
        w_scr[0] = wa_ref[...].astype(BF16)
        w_scr[1] = wb_ref[...].astype(BF16)

    r_tile = pl.program_id(1) * tm
    a_prev = _dot(uh_ref[...], w_scr[0])
    for r0, r1 in _row_blocks(tm, FFN_ROWS):
        u = u_ref[r0:r1, :]
        a = _dot(u, w_scr[0])
        conv = _causal_conv3(a_prev, a, cw_ref[...], _pos_in_batch(r_tile + r0, r1 - r0)) + cb_ref[...]
        gelu = 0.5 * conv * (1.0 + lax.erf(conv * (2.0 ** -0.5)))
        out_ref[r0:r1, :] = (gelu * _dot(u, w_scr[1])).astype(BF16)
        a_prev = a[r1 - r0 - HALO:, :]


def _ffn_up(u, w_up, layer, conv_w, conv_b, tm=2 * L // 3, tn=512):
    m = u.shape[0]
    nb = D_FF // tn
    hb = tm // HALO
    return pl.pallas_call(
        functools.partial(_ffn_up_kernel, tm),
        grid=(nb, m // tm),
        in_specs=[
            pl.BlockSpec((tm, D_MODEL), lambda j, i: (i, 0)),
            pl.BlockSpec((HALO, D_MODEL), lambda j, i: (jnp.maximum(i * hb - 1, 0), 0)),
            pl.BlockSpec((None, D_MODEL, tn), lambda j, i: (layer, 0, j)),
            pl.BlockSpec((None, D_MODEL, tn), lambda j, i: (layer, 0, nb + j)),
            pl.BlockSpec((3, tn), lambda j, i: (0, j)),
            pl.BlockSpec((1, tn), lambda j, i: (0, j)),
        ],
        out_specs=pl.BlockSpec((tm, tn), lambda j, i: (i, j)),
        out_shape=jax.ShapeDtypeStruct((m, D_FF), BF16),
        scratch_shapes=[pltpu.VMEM((2, D_MODEL, tn), BF16)],
        compiler_params=_params(("parallel", "arbitrary")),
        name="ffn_up",
    )(u, u, w_up, w_up, conv_w, conv_b)


def _ffn_down_kernel(final, act_ref, w_ref, h_ref, g_ref, *out_refs):
    hn = h_ref[...] + _dot(act_ref[...], w_ref[...])
    y = _rms(hn, g_ref[...])
    if final:
        out_refs[0][...] = y
    else:
        out_refs[0][...] = hn
        out_refs[1][...] = y.astype(BF16)


def _ffn_down(act, w_down, layer, h, g, final):
    m = h.shape[0]
    if final:
        tm = 256
        m = m // L * SEQ
        per_batch = SEQ // tm

        def stream_row(i):
            return pl.multiple_of(i * tm + N_META * (i // per_batch + 1), N_META)

        act_spec = pl.BlockSpec((pl.Element(tm), pl.Element(D_FF)), lambda i: (stream_row(i), 0))
        h_spec = pl.BlockSpec((pl.Element(tm), pl.Element(D_MODEL)), lambda i: (stream_row(i), 0))
        row = pl.BlockSpec((tm, D_MODEL), lambda i: (i, 0))
        out_specs = [row]
        out_shape = [jax.ShapeDtypeStruct((m, D_MODEL), F32)]
    else:
        tm = 384
        row = pl.BlockSpec((tm, D_MODEL), lambda i: (i, 0))
        act_spec = pl.BlockSpec((tm, D_FF), lambda i: (i, 0))
        h_spec = row
        out_specs = [row, row]
        out_shape = [jax.ShapeDtypeStruct((m, D_MODEL), F32),
                     jax.ShapeDtypeStruct((m, D_MODEL), BF16)]
    return pl.pallas_call(
        functools.partial(_ffn_down_kernel, final),
        grid=(m // tm,),
        in_specs=[
            act_spec,
            _resident((None, D_FF, D_MODEL), lambda i: (layer, 0, 0)),
            h_spec,
            _resident((1, D_MODEL), lambda i: (0, 0)),
        ],
        out_specs=out_specs,
        out_shape=out_shape,
        compiler_params=_params(("parallel",)),
        name="ffn_down_final" if final else "ffn_down",
    )(act, w_down, h, g)


def kernel(x, meta, norm_mix, w_in, conv_a, b_if, ml_norm, da_lambda, da_norm, w_br_a, w_br_m, w_br_d,
           w_out, norm_ffn, w_up, conv_ffn, conv_ffn_b, w_down, norm_f):
    bsz, seq, d = x.shape
    assert (seq, d) == (SEQ, D_MODEL)
    depth = w_in.shape[0]
    wz = _wprep(w_in)
    wa, wm, wd, wo, wdn = (w.astype(BF16) for w in (w_br_a, w_br_m, w_br_d, w_out, w_down))
    h, u = _embed(x.reshape(bsz * SEQ, D_MODEL), meta, norm_mix[0][None], bsz)
    for i in range(depth):
        z, zif = _inproj(u, wz, w_in, i)
        bias = jnp.pad(b_if[i].reshape(1, 2 * ML_HEADS), ((0, 0), (0, IF_W - 2 * ML_HEADS)))
        hm = _mlstm(z, zif, bias, ml_norm[i][None], bsz)
        hd = _attn(z, da_lambda[i], da_norm[i][None], i, bsz)
        h, u = _merge(z, hm, hd, h, i, conv_a[i], wa, wm, wd, wo, norm_ffn[i][None])
        act = _ffn_up(u, w_up, i, conv_ffn[i], conv_ffn_b[i][None])
        final = i == depth - 1
        g_next = norm_f if final else norm_mix[i + 1]
        outs = _ffn_down(act, wdn, i, h, g_next[None], final)
        if final:
            y = outs[0]
        else:
            h, u = outs
    return y.reshape(bsz, SEQ, D_MODEL)
```

```python
import functools
import math

import jax
import jax.numpy as jnp
from jax import lax
from jax.experimental import pallas as pl
from jax.experimental.pallas import tpu as pltpu

F32 = jnp.float32
BF16 = jnp.bfloat16

D_MODEL = 2048
SEQ = 2048
N_META = 16
L = N_META + SEQ
CHUNK = 64
EPS = 1e-6

CONV_DIM = 512
ML_HEADS = 4
ML_DK = 256
ML_DV = 256
ML_DIM = ML_HEADS * ML_DV
DA_HEADS = 4
DA_HD = 64
DA_VD = 2 * DA_HD
DA_DIM = DA_HEADS * DA_VD
D_FF = 5632

Z_GATE = 0
Z_MQ = 3 * D_MODEL
Z_AX = Z_MQ + 4 * ML_DIM
Z_DQ = Z_AX + 3 * CONV_DIM
Z_N = Z_DQ + 3 * DA_DIM
IF_W = 128

ML_T = 128
ML_EXT = ML_DV + 128
HALO = 16
ATT_QB = 256
EMB_T = L // 3
VMEM_LIMIT = 56 * 1024 * 1024


def _dot(a, b):
    return jnp.dot(a, b, preferred_element_type=F32)


def _dot_nt(a, b):
    return lax.dot_general(a, b, (((1,), (1,)), ((), ())), preferred_element_type=F32)


def _dot_tn(a, b):
    return lax.dot_general(a, b, (((0,), (0,)), ((), ())), preferred_element_type=F32)


def _rms(x, g):
    return x * lax.rsqrt(jnp.mean(x * x, axis=-1, keepdims=True) + EPS) * g


def _sigmoid(x):
    return 1.0 / (1.0 + jnp.exp(-x))


def _pos_in_batch(r0, tm):
    pos = r0 % L + lax.broadcasted_iota(jnp.int32, (tm, 1), 0)
    return jnp.where(pos >= L, pos - L, pos)


def _causal_conv3(prev, cur, w, pos):
    cc = jnp.concatenate([prev, cur], axis=0)
    x1 = jnp.where(pos >= 1, pltpu.roll(cc, 1, 0)[HALO:], 0.0)
    x2 = jnp.where(pos >= 2, pltpu.roll(cc, 2, 0)[HALO:], 0.0)
    return w[0:1] * x2 + w[1:2] * x1 + w[2:3] * cur


def _params(sem):
    return pltpu.CompilerParams(dimension_semantics=sem, vmem_limit_bytes=VMEM_LIMIT)


def _resident(shape, index_map):
    return pl.BlockSpec(shape, index_map, pipeline_mode=pl.Buffered(1))


def _embed_kernel(x_ref, meta_ref, g_ref, h_ref, u_ref):
    j = pl.program_id(1)

    @pl.when(j == 0)
    def _():
        rows = jnp.concatenate([meta_ref[...], x_ref[0:EMB_T - N_META, :]], axis=0)
        h_ref[...] = rows
        u_ref[...] = _rms(rows, g_ref[...]).astype(BF16)

    @pl.when(j > 0)
    def _():
        rows = x_ref[...]
        h_ref[...] = rows
        u_ref[...] = _rms(rows, g_ref[...]).astype(BF16)


def _embed(x2d, meta, g, bsz):
    nb = L // EMB_T
    m = bsz * L
    return pl.pallas_call(
        _embed_kernel,
        grid=(bsz, nb),
        in_specs=[
            pl.BlockSpec((pl.Element(EMB_T), pl.Element(D_MODEL)),
                         lambda b, j: (pl.multiple_of(b * SEQ + jnp.maximum(j * EMB_T - N_META, 0), 16), 0)),
            pl.BlockSpec((N_META, D_MODEL), lambda b, j: (0, 0)),
            pl.BlockSpec((1, D_MODEL), lambda b, j: (0, 0)),
        ],
        out_specs=[
            pl.BlockSpec((EMB_T, D_MODEL), lambda b, j: (b * nb + j, 0)),
            pl.BlockSpec((EMB_T, D_MODEL), lambda b, j: (b * nb + j, 0)),
        ],
        out_shape=[jax.ShapeDtypeStruct((m, D_MODEL), F32),
                   jax.ShapeDtypeStruct((m, D_MODEL), BF16)],
        compiler_params=_params(("parallel", "arbitrary")),
        name="embed_norm",
    )(x2d, meta, g)


W_A_END = 3 * CONV_DIM
W_M_END = W_A_END + 4 * ML_DIM
W_IF_END = W_M_END + 2 * ML_HEADS
W_D_END = W_IF_END + 3 * DA_DIM
W_IN = W_D_END + 3 * D_MODEL


IN_SRC = 512
IN_NSRC = 2


def _inproj_kernel(u_ref, *refs):
    wt_refs, wift_ref, z_ref, zif_ref = refs[:IN_NSRC], refs[IN_NSRC], refs[IN_NSRC + 1], refs[IN_NSRC + 2]
    u = u_ref[...]
    for k, wt_ref in enumerate(wt_refs):
        z_ref[:, k * IN_SRC:(k + 1) * IN_SRC] = _dot_nt(u, wt_ref[...].astype(BF16)).astype(BF16)

    @pl.when(pl.program_id(1) == 0)
    def _():
        zif_ref[...] = _dot_nt(u, wift_ref[...].astype(BF16))


def _inproj(u, w_in_t, layer, tm=L):
    m = u.shape[0]
    tn = IN_NSRC * IN_SRC
    n_gate, n_m, n_a = 3 * D_MODEL // IN_SRC, 4 * ML_DIM // IN_SRC, 3 * CONV_DIM // IN_SRC

    def src_row(t):
        gate = W_D_END + IN_SRC * t
        mls = W_A_END + IN_SRC * (t - n_gate)
        cnv = IN_SRC * (t - n_gate - n_m)
        att = W_IF_END + IN_SRC * (t - n_gate - n_m - n_a)
        row = jnp.where(t < n_gate, gate,
                        jnp.where(t < n_gate + n_m, mls, jnp.where(t < n_gate + n_m + n_a, cnv, att)))
        return pl.multiple_of(layer * W_IN + row, 8)

    def src_spec(k):
        return pl.BlockSpec((pl.Element(IN_SRC), pl.Element(D_MODEL)),
                            lambda i, j: (src_row(IN_NSRC * j + k), 0))

    return pl.pallas_call(
        _inproj_kernel,
        grid=(m // tm, Z_N // tn),
        in_specs=[pl.BlockSpec((tm, D_MODEL), lambda i, j: (i, 0), pipeline_mode=pl.Buffered(1))]
        + [src_spec(k) for k in range(IN_NSRC)]
        + [_resident((pl.Element(IF_W), pl.Element(D_MODEL)), lambda i, j: (layer * W_IN + W_M_END, 0))],
        out_specs=[
            pl.BlockSpec((tm, tn), lambda i, j: (i, j)),
            pl.BlockSpec((tm, IF_W), lambda i, j: (i, 0)),
        ],
        out_shape=[jax.ShapeDtypeStruct((m, Z_N), BF16),
                   jax.ShapeDtypeStruct((m, IF_W), F32)],
        compiler_params=_params(("parallel", "arbitrary")),
        name="in_proj",
    )(u, *([w_in_t] * (IN_NSRC + 1)))


def _mlstm_chunk(r0, first_valid, q_ref, k_ref, v_ref, o_ref, if_ref, bias_ref, g_ref, out_ref, s_ref, m_ref):
    t = ML_T
    ninf = -jnp.inf
    rows = pl.ds(r0, t)

    gt = (if_ref[rows, :] + bias_ref[...]).T[0:8, :]
    lane = lax.broadcasted_iota(jnp.int32, (8, t), 1)
    row8 = lax.broadcasted_iota(jnp.int32, (8, t), 0)
    log_sig = jnp.minimum(gt, 0.0) - jnp.log1p(jnp.exp(-jnp.abs(gt)))
    is_i = row8 < ML_HEADS
    lg = jnp.where(is_i, gt, log_sig)
    if first_valid:
        lg = jnp.where(lane >= first_valid, lg, jnp.where(is_i, ninf, 0.0))
    cs = jnp.where(is_i, 0.0, lg)
    sh = 1
    while sh < t:
        cs = cs + jnp.where(lane >= sh, pltpu.roll(cs, sh, 1), 0.0)
        sh *= 2
    rowpack = jnp.where(is_i, lg, cs)
    colpack = jnp.concatenate([rowpack, jnp.zeros((t - 8, t), F32)], axis=0).T

    tri = lax.broadcasted_iota(jnp.int32, (t, t), 0) >= lax.broadcasted_iota(jnp.int32, (t, t), 1)
    ones_blk = jnp.ones((t, ML_EXT - ML_DV), BF16)

    for hd in range(ML_HEADS):
        sl = slice(hd * ML_DK, (hd + 1) * ML_DK)
        q = q_ref[rows, sl]
        k = k_ref[rows, sl] * jnp.asarray(ML_DK ** -0.5, BF16)
        vext = jnp.concatenate([v_ref[rows, sl], ones_blk], axis=1)
        li_r = rowpack[hd:hd + 1, :]
        b_r = rowpack[ML_HEADS + hd:ML_HEADS + hd + 1, :]
        li_c = colpack[:, hd:hd + 1]
        b_c = colpack[:, ML_HEADS + hd:ML_HEADS + hd + 1]
        m_prev = m_ref[hd][:, 0:1]

        dmat = jnp.where(tri, b_c - b_r + li_r, ninf)
        inter = b_c + m_prev
        m_t = jnp.maximum(jnp.max(dmat, axis=1, keepdims=True), inter)
        w = jnp.exp(dmat - m_t)
        g = jnp.exp(inter - m_t)
        s = _dot_nt(q, k) * w
        tot = g * _dot(q, s_ref[hd].astype(BF16)) + _dot(s.astype(BF16), vext)
        num = tot[:, :ML_DV]
        den = tot[:, ML_DV:ML_DV + 1]
        hh = num / jnp.maximum(jnp.abs(den), jnp.exp(-m_t))
        res = (_rms(hh, g_ref[:, sl]) * _sigmoid(o_ref[rows, sl].astype(F32))).astype(BF16)
        if first_valid:
            out_ref[pl.ds(r0 + first_valid, t - first_valid), sl] = res[first_valid:, :]
        else:
            out_ref[rows, sl] = res

        b_last = b_r[:, t - 1:t]
        ds_r = b_last - b_r + li_r
        ds_c = b_last - b_c + li_c
        m_new = jnp.maximum(b_last + m_prev, jnp.max(ds_r, axis=1, keepdims=True))
        wk = jnp.exp(ds_c - m_new)
        gs = jnp.exp(b_last + m_prev - m_new)
        vw = (vext.astype(F32) * wk).astype(BF16)
        s_ref[hd] = gs * s_ref[hd] + _dot_tn(k, vw)
        m_ref[hd] = jnp.broadcast_to(m_new, (1, 128))


def _mlstm_kernel(*refs):
    s_ref, m_ref = refs[-2:]
    s_ref[...] = jnp.zeros_like(s_ref)
    m_ref[...] = jnp.zeros_like(m_ref)
    n_full = L // ML_T

    def body(c, carry):
        _mlstm_chunk(pl.multiple_of(c * ML_T, ML_T), 0, *refs)
        return carry

    lax.fori_loop(0, n_full, body, 0)
    _mlstm_chunk(L - ML_T, ML_T - (L - n_full * ML_T), *refs)


def _mlstm(z, zif, bias, g, bsz):
    m = bsz * L
    qb = Z_MQ // ML_DIM

    def zspec(off):
        return pl.BlockSpec((L, ML_DIM), lambda b: (b, qb + off))

    return pl.pallas_call(
        _mlstm_kernel,
        grid=(bsz,),
        in_specs=[
            zspec(0), zspec(1), zspec(2), zspec(3),
            pl.BlockSpec((L, IF_W), lambda b: (b, 0)),
            pl.BlockSpec((1, IF_W), lambda b: (0, 0)),
            pl.BlockSpec((1, ML_DIM), lambda b: (0, 0)),
        ],
        out_specs=pl.BlockSpec((L, ML_DIM), lambda b: (b, 0)),
        out_shape=jax.ShapeDtypeStruct((m, ML_DIM), BF16),
        scratch_shapes=[pltpu.VMEM((ML_HEADS, ML_DK, ML_EXT), F32),
                        pltpu.VMEM((ML_HEADS, 1, 128), F32)],
        compiler_params=_params(("parallel",)),
        name="mlstm",
    )(z, z, z, z, zif, bias, g)


def _attn_block(q, k_ref, v_ref, n_frames, frame0, lam, lam_init, g):
    rows = q.shape[0]
    ninf = -jnp.inf
    first_map = lax.broadcasted_iota(jnp.int32, (rows, DA_VD), 1) < DA_HD
    k_meta = k_ref[0:ML_T, :]
    meta_ok = lax.broadcasted_iota(jnp.int32, (1, ML_T), 1) < N_META
    if n_frames:
        k_fr = k_ref[N_META:N_META + n_frames, :]
        qf = frame0 + lax.broadcasted_iota(jnp.int32, (rows, 1), 0)
        kend = (qf // CHUNK + 1) * CHUNK
        kf = frame0 + lax.broadcasted_iota(jnp.int32, (1, n_frames - frame0), 1)
        diag_ok = kf < kend
    probs = []
    for mp in range(2):
        qm = jnp.where(first_map if mp == 0 else jnp.logical_not(first_map), q, jnp.zeros_like(q))
        pieces = [jnp.where(meta_ok, _dot_nt(qm, k_meta), ninf)]
        if n_frames:
            s_fr = _dot_nt(qm, k_fr)
            if frame0:
                pieces.append(s_fr[:, :frame0])
            pieces.append(jnp.where(diag_ok, s_fr[:, frame0:], ninf))
        s = pieces[0] if len(pieces) == 1 else jnp.concatenate(pieces, axis=1)
        p = jnp.exp2(s - jnp.max(s, axis=1, keepdims=True))
        probs.append((p, jnp.sum(p, axis=1, keepdims=True)))
    a = (probs[0][0] * (1.0 / probs[0][1]) - probs[1][0] * (lam / probs[1][1])).astype(BF16)
    o = _dot(a[:, :ML_T], v_ref[0:ML_T, :])
    if n_frames:
        o = o + _dot(a[:, ML_T:], v_ref[N_META:N_META + n_frames, :])
    return (_rms(o, g) * (1.0 - lam_init)).astype(BF16)


def _attn_kernel(lam_init, q_ref, k_ref, v_ref, lam_ref, g_ref, out_ref):
    lf = lam_ref[...]
    lam = (jnp.exp(jnp.sum(lf[0:1] * lf[1:2], axis=1, keepdims=True))
           - jnp.exp(jnp.sum(lf[2:3] * lf[3:4], axis=1, keepdims=True)) + lam_init)
    scale = DA_HD ** -0.5 * math.log2(math.e)
    g = g_ref[...]

    def scaled(rows):
        return (q_ref[rows, :].astype(F32) * scale).astype(BF16)

    out_ref[0:N_META, :] = _attn_block(scaled(slice(0, N_META)), k_ref, v_ref, 0, 0, lam, lam_init, g)
    for f0 in range(0, SEQ, ATT_QB):
        rows = slice(N_META + f0, N_META + f0 + ATT_QB)
        out_ref[rows, :] = _attn_block(scaled(rows), k_ref, v_ref, f0 + ATT_QB, f0, lam, lam_init, g)


def _attn(z, lam_p, g, layer, bsz):
    lam_init = 0.8 - 0.6 * math.exp(-0.3 * layer)
    m = bsz * L
    qb = Z_DQ // DA_VD

    def zspec(off):
        return pl.BlockSpec((L, DA_VD), lambda b, h: (b, qb + off * DA_HEADS + h))

    return pl.pallas_call(
        functools.partial(_attn_kernel, lam_init),
        grid=(bsz, DA_HEADS),
        in_specs=[
            zspec(0), zspec(1), zspec(2),
            pl.BlockSpec((4, DA_HD), lambda b, h: (0, 0)),
            pl.BlockSpec((1, DA_VD), lambda b, h: (0, 0)),
        ],
        out_specs=pl.BlockSpec((L, DA_VD), lambda b, h: (b, h)),
        out_shape=jax.ShapeDtypeStruct((m, DA_DIM), BF16),
        compiler_params=_params(("parallel", "parallel")),
        name="diff_attn",
    )(z, z, z, lam_p, g)


MERGE_NC = 512


def _merge_kernel(tm, ax_ref, ab_ref, ac_ref, axh_ref, ach_ref, hm_ref, hd_ref,
                  ga_ref, gm_ref, gd_ref, h_ref, cw_ref, wa_ref, wm_ref, wd_ref, wo_ref, g_ref,
                  hn_ref, u_ref, mg_scr):
    pos = _pos_in_batch(pl.program_id(0) * tm, tm)
    cur = ac_ref[...].astype(F32) * ax_ref[...].astype(F32)
    prev = ach_ref[...].astype(F32) * axh_ref[...].astype(F32)
    a_act = (ab_ref[...].astype(F32) * _causal_conv3(prev, cur, cw_ref[...], pos)).astype(BF16)
    hm = hm_ref[...]
    hd = hd_ref[...]
    for c0 in range(0, D_MODEL, MERGE_NC):
        cs = slice(c0, c0 + MERGE_NC)
        merged = (_sigmoid(ga_ref[:, cs].astype(F32)) * _dot(a_act, wa_ref[:, cs])
                  + _sigmoid(gm_ref[:, cs].astype(F32)) * _dot(hm, wm_ref[:, cs])
                  + _sigmoid(gd_ref[:, cs].astype(F32)) * _dot(hd, wd_ref[:, cs]))
        mg_scr[:, cs] = merged.astype(BF16)
    hn = h_ref[...] + _dot(mg_scr[...], wo_ref[...])
    hn_ref[...] = hn
    u_ref[...] = _rms(hn, g_ref[...]).astype(BF16)


def _merge(z, hm, hd, h, layer, conv_a, w_br_a, w_br_m, w_br_d, w_out, g, tm=384):
    m = h.shape[0]
    axb = Z_AX // CONV_DIM
    hb = tm // HALO

    def halo(col):
        return pl.BlockSpec((HALO, CONV_DIM), lambda i: (jnp.maximum(i * hb - 1, 0), col))

    def weight(rows):
        return _resident((None, rows, D_MODEL), lambda i: (layer, 0, 0))

    return pl.pallas_call(
        functools.partial(_merge_kernel, tm),
        grid=(m // tm,),
        in_specs=[
            pl.BlockSpec((tm, CONV_DIM), lambda i: (i, axb)),
            pl.BlockSpec((tm, CONV_DIM), lambda i: (i, axb + 1)),
            pl.BlockSpec((tm, CONV_DIM), lambda i: (i, axb + 2)),
            halo(axb), halo(axb + 2),
            pl.BlockSpec((tm, ML_DIM), lambda i: (i, 0)),
            pl.BlockSpec((tm, DA_DIM), lambda i: (i, 0)),
            pl.BlockSpec((tm, D_MODEL), lambda i: (i, 0)),
            pl.BlockSpec((tm, D_MODEL), lambda i: (i, 1)),
            pl.BlockSpec((tm, D_MODEL), lambda i: (i, 2)),
            pl.BlockSpec((tm, D_MODEL), lambda i: (i, 0)),
            _resident((3, CONV_DIM), lambda i: (0, 0)),
            weight(CONV_DIM), weight(ML_DIM), weight(DA_DIM), weight(D_MODEL),
            _resident((1, D_MODEL), lambda i: (0, 0)),
        ],
        out_specs=[
            pl.BlockSpec((tm, D_MODEL), lambda i: (i, 0)),
            pl.BlockSpec((tm, D_MODEL), lambda i: (i, 0)),
        ],
        out_shape=[jax.ShapeDtypeStruct((m, D_MODEL), F32),
                   jax.ShapeDtypeStruct((m, D_MODEL), BF16)],
        scratch_shapes=[pltpu.VMEM((tm, D_MODEL), BF16)],
        compiler_params=_params(("parallel",)),
        name="merge_out_proj",
    )(z, z, z, z, z, hm, hd, z, z, z, h, conv_a, w_br_a, w_br_m, w_br_d, w_out, g)


FFN_ROWS = 4


def _row_blocks(tm, n):
    units = tm // HALO
    edges = [HALO * (units * k // n) for k in range(n + 1)]
    return list(zip(edges[:-1], edges[1:]))


def _ffn_up_kernel(tm, u_ref, uh_ref, wa_ref, wb_ref, cw_ref, cb_ref, out_ref, w_scr):
    @pl.when(pl.program_id(1) == 0)
    def _():
        w_scr[0] = wa_ref[...].astype(BF16)
        w_scr[1] = wb_ref[...].astype(BF16)

    r_tile = pl.program_id(1) * tm
    a_prev = _dot(uh_ref[...], w_scr[0])
    for r0, r1 in _row_blocks(tm, FFN_ROWS):
        u = u_ref[r0:r1, :]
        a = _dot(u, w_scr[0])
        conv = _causal_conv3(a_prev, a, cw_ref[...], _pos_in_batch(r_tile + r0, r1 - r0)) + cb_ref[...]
        gelu = 0.5 * conv * (1.0 + lax.erf(conv * (2.0 ** -0.5)))
        out_ref[r0:r1, :] = (gelu * _dot(u, w_scr[1])).astype(BF16)
        a_prev = a[r1 - r0 - HALO:, :]


def _ffn_up(u, w_up, layer, conv_w, conv_b, tm=2 * L // 3, tn=512):
    m = u.shape[0]
    nb = D_FF // tn
    hb = tm // HALO
    return pl.pallas_call(
        functools.partial(_ffn_up_kernel, tm),
        grid=(nb, m // tm),
        in_specs=[
            pl.BlockSpec((tm, D_MODEL), lambda j, i: (i, 0)),
            pl.BlockSpec((HALO, D_MODEL), lambda j, i: (jnp.maximum(i * hb - 1, 0), 0)),
            pl.BlockSpec((None, D_MODEL, tn), lambda j, i: (layer, 0, j)),
            pl.BlockSpec((None, D_MODEL, tn), lambda j, i: (layer, 0, nb + j)),
            pl.BlockSpec((3, tn), lambda j, i: (0, j)),
            pl.BlockSpec((1, tn), lambda j, i: (0, j)),
        ],
        out_specs=pl.BlockSpec((tm, tn), lambda j, i: (i, j)),
        out_shape=jax.ShapeDtypeStruct((m, D_FF), BF16),
        scratch_shapes=[pltpu.VMEM((2, D_MODEL, tn), BF16)],
        compiler_params=_params(("parallel", "arbitrary")),
        name="ffn_up",
    )(u, u, w_up, w_up, conv_w, conv_b)


def _ffn_down_kernel(final, act_ref, w_ref, h_ref, g_ref, *out_refs):
    hn = h_ref[...] + _dot(act_ref[...], w_ref[...])
    y = _rms(hn, g_ref[...])
    if final:
        out_refs[0][...] = y
    else:
        out_refs[0][...] = hn
        out_refs[1][...] = y.astype(BF16)


def _ffn_down(act, w_down, layer, h, g, final):
    m = h.shape[0]
    if final:
        tm = 256
        m = m // L * SEQ
        per_batch = SEQ // tm

        def stream_row(i):
            return pl.multiple_of(i * tm + N_META * (i // per_batch + 1), N_META)

        act_spec = pl.BlockSpec((pl.Element(tm), pl.Element(D_FF)), lambda i: (stream_row(i), 0))
        h_spec = pl.BlockSpec((pl.Element(tm), pl.Element(D_MODEL)), lambda i: (stream_row(i), 0))
        row = pl.BlockSpec((tm, D_MODEL), lambda i: (i, 0))
        out_specs = [row]
        out_shape = [jax.ShapeDtypeStruct((m, D_MODEL), F32)]
    else:
        tm = 384
        row = pl.BlockSpec((tm, D_MODEL), lambda i: (i, 0))
        act_spec = pl.BlockSpec((tm, D_FF), lambda i: (i, 0))
        h_spec = row
        out_specs = [row, row]
        out_shape = [jax.ShapeDtypeStruct((m, D_MODEL), F32),
                     jax.ShapeDtypeStruct((m, D_MODEL), BF16)]
    return pl.pallas_call(
        functools.partial(_ffn_down_kernel, final),
        grid=(m // tm,),
        in_specs=[
            act_spec,
            _resident((None, D_FF, D_MODEL), lambda i: (layer, 0, 0)),
            h_spec,
            _resident((1, D_MODEL), lambda i: (0, 0)),
        ],
        out_specs=out_specs,
        out_shape=out_shape,
        compiler_params=_params(("parallel",)),
        name="ffn_down_final" if final else "ffn_down",
    )(act, w_down, h, g)


def kernel(x, meta, norm_mix, w_in, conv_a, b_if, ml_norm, da_lambda, da_norm, w_br_a, w_br_m, w_br_d,
           w_out, norm_ffn, w_up, conv_ffn, conv_ffn_b, w_down, norm_f):
    bsz, seq, d = x.shape
    assert (seq, d) == (SEQ, D_MODEL)
    depth = w_in.shape[0]
    w_in_t = jnp.swapaxes(w_in, 1, 2).reshape(depth * W_IN, D_MODEL)
    wa, wm, wd, wo, wdn = (w.astype(BF16) for w in (w_br_a, w_br_m, w_br_d, w_out, w_down))
    h, u = _embed(x.reshape(bsz * SEQ, D_MODEL), meta, norm_mix[0][None], bsz)
    for i in range(depth):
        z, zif = _inproj(u, w_in_t, i)
        bias = jnp.pad(b_if[i].reshape(1, 2 * ML_HEADS), ((0, 0), (0, IF_W - 2 * ML_HEADS)))
        hm = _mlstm(z, zif, bias, ml_norm[i][None], bsz)
        hd = _attn(z, da_lambda[i], da_norm[i][None], i, bsz)
        h, u = _merge(z, hm, hd, h, i, conv_a[i], wa, wm, wd, wo, norm_ffn[i][None])
        act = _ffn_up(u, w_up, i, conv_ffn[i], conv_ffn_b[i][None])
        final = i == depth - 1
        g_next = norm_f if final else norm_mix[i + 1]
        outs = _ffn_down(act, wdn, i, h, g_next[None], final)
        if final:
            y = outs[0]
        else:
            h, u = outs
    return y.reshape(bsz, SEQ, D_MODEL)
```

```python
import functools
import math

import jax
import jax.numpy as jnp
from jax import lax
from jax.experimental import pallas as pl
from jax.experimental.pallas import tpu as pltpu

F32 = jnp.float32
BF16 = jnp.bfloat16

D_MODEL = 2048
SEQ = 2048
N_META = 16
L = N_META + SEQ
CHUNK = 64
EPS = 1e-6

CONV_DIM = 512
ML_HEADS = 4
ML_DK = 256
ML_DV = 256
ML_DIM = ML_HEADS * ML_DV
DA_HEADS = 4
DA_HD = 64
DA_VD = 2 * DA_HD
DA_DIM = DA_HEADS * DA_VD
D_FF = 5632

Z_GATE = 0
Z_MQ = 3 * D_MODEL
Z_AX = Z_MQ + 4 * ML_DIM
Z_DQ = Z_AX + 3 * CONV_DIM
Z_N = Z_DQ + 3 * DA_DIM
IF_W = 128

ML_T = 128
ML_EXT = ML_DV + 128
HALO = 16
ATT_QB = 256
EMB_T = L // 3
VMEM_LIMIT = 56 * 1024 * 1024


def _dot(a, b):
    return jnp.dot(a, b, preferred_element_type=F32)


def _dot_nt(a, b):
    return lax.dot_general(a, b, (((1,), (1,)), ((), ())), preferred_element_type=F32)


def _dot_tn(a, b):
    return lax.dot_general(a, b, (((0,), (0,)), ((), ())), preferred_element_type=F32)


def _bdot(a, b, ca, cb):
    return lax.dot_general(a, b, (((ca,), (cb,)), ((0,), (0,))), preferred_element_type=F32)


def _rms(x, g):
    return x * lax.rsqrt(jnp.mean(x * x, axis=-1, keepdims=True) + EPS) * g


def _sigmoid(x):
    return 1.0 / (1.0 + jnp.exp(-x))


def _pos_in_batch(r0, tm):
    pos = r0 % L + lax.broadcasted_iota(jnp.int32, (tm, 1), 0)
    return jnp.where(pos >= L, pos - L, pos)


def _causal_conv3(prev, cur, w, pos):
    cc = jnp.concatenate([prev, cur], axis=0)
    x1 = jnp.where(pos >= 1, pltpu.roll(cc, 1, 0)[HALO:], 0.0)
    x2 = jnp.where(pos >= 2, pltpu.roll(cc, 2, 0)[HALO:], 0.0)
    return w[0:1] * x2 + w[1:2] * x1 + w[2:3] * cur


def _params(sem):
    return pltpu.CompilerParams(dimension_semantics=sem, vmem_limit_bytes=VMEM_LIMIT)


def _resident(shape, index_map):
    return pl.BlockSpec(shape, index_map, pipeline_mode=pl.Buffered(1))


def _embed_kernel(x_ref, meta_ref, g_ref, h_ref, u_ref):
    j = pl.program_id(1)

    @pl.when(j == 0)
    def _():
        rows = jnp.concatenate([meta_ref[...], x_ref[0:EMB_T - N_META, :]], axis=0)
        h_ref[...] = rows
        u_ref[...] = _rms(rows, g_ref[...]).astype(BF16)

    @pl.when(j > 0)
    def _():
        rows = x_ref[...]
        h_ref[...] = rows
        u_ref[...] = _rms(rows, g_ref[...]).astype(BF16)


def _embed(x2d, meta, g, bsz):
    nb = L // EMB_T
    m = bsz * L
    return pl.pallas_call(
        _embed_kernel,
        grid=(bsz, nb),
        in_specs=[
            pl.BlockSpec((pl.Element(EMB_T), pl.Element(D_MODEL)),
                         lambda b, j: (pl.multiple_of(b * SEQ + jnp.maximum(j * EMB_T - N_META, 0), 16), 0)),
            pl.BlockSpec((N_META, D_MODEL), lambda b, j: (0, 0)),
            pl.BlockSpec((1, D_MODEL), lambda b, j: (0, 0)),
        ],
        out_specs=[
            pl.BlockSpec((EMB_T, D_MODEL), lambda b, j: (b * nb + j, 0)),
            pl.BlockSpec((EMB_T, D_MODEL), lambda b, j: (b * nb + j, 0)),
        ],
        out_shape=[jax.ShapeDtypeStruct((m, D_MODEL), F32),
                   jax.ShapeDtypeStruct((m, D_MODEL), BF16)],
        compiler_params=_params(("parallel", "arbitrary")),
        name="embed_norm",
    )(x2d, meta, g)


W_A_END = 3 * CONV_DIM
W_M_END = W_A_END + 4 * ML_DIM
W_IF_END = W_M_END + 2 * ML_HEADS
W_D_END = W_IF_END + 3 * DA_DIM
W_IN = W_D_END + 3 * D_MODEL


IN_SRC = 512
IN_NSRC = 2


def _inproj_kernel(u_ref, *refs):
    wt_refs, wift_ref, z_ref, zif_ref = refs[:IN_NSRC], refs[IN_NSRC], refs[IN_NSRC + 1], refs[IN_NSRC + 2]
    u = u_ref[...]
    for k, wt_ref in enumerate(wt_refs):
        z_ref[:, k * IN_SRC:(k + 1) * IN_SRC] = _dot_nt(u, wt_ref[...].astype(BF16)).astype(BF16)

    @pl.when(pl.program_id(1) == 0)
    def _():
        zif_ref[...] = _dot_nt(u, wift_ref[...].astype(BF16))


def _inproj(u, w_in_t, layer, tm=L):
    m = u.shape[0]
    tn = IN_NSRC * IN_SRC
    n_gate, n_m, n_a = 3 * D_MODEL // IN_SRC, 4 * ML_DIM // IN_SRC, 3 * CONV_DIM // IN_SRC

    def src_row(t):
        gate = W_D_END + IN_SRC * t
        mls = W_A_END + IN_SRC * (t - n_gate)
        cnv = IN_SRC * (t - n_gate - n_m)
        att = W_IF_END + IN_SRC * (t - n_gate - n_m - n_a)
        row = jnp.where(t < n_gate, gate,
                        jnp.where(t < n_gate + n_m, mls, jnp.where(t < n_gate + n_m + n_a, cnv, att)))
        return pl.multiple_of(layer * W_IN + row, 8)

    def src_spec(k):
        return pl.BlockSpec((pl.Element(IN_SRC), pl.Element(D_MODEL)),
                            lambda i, j: (src_row(IN_NSRC * j + k), 0))

    return pl.pallas_call(
        _inproj_kernel,
        grid=(m // tm, Z_N // tn),
        in_specs=[pl.BlockSpec((tm, D_MODEL), lambda i, j: (i, 0), pipeline_mode=pl.Buffered(1))]
        + [src_spec(k) for k in range(IN_NSRC)]
        + [_resident((pl.Element(IF_W), pl.Element(D_MODEL)), lambda i, j: (layer * W_IN + W_M_END, 0))],
        out_specs=[
            pl.BlockSpec((tm, tn), lambda i, j: (i, j)),
            pl.BlockSpec((tm, IF_W), lambda i, j: (i, 0)),
        ],
        out_shape=[jax.ShapeDtypeStruct((m, Z_N), BF16),
                   jax.ShapeDtypeStruct((m, IF_W), F32)],
        compiler_params=_params(("parallel", "arbitrary")),
        name="in_proj",
    )(u, *([w_in_t] * (IN_NSRC + 1)))


ML_NCH = -(-L // ML_T)
ML_TAIL_VALID = ML_T - (L - (ML_NCH - 1) * ML_T)


def _mlstm_gates(if_ref, bias_ref, row_scr, col_scr):
    t = ML_T
    ninf = -jnp.inf
    bias = bias_ref[...]
    starts = [c * t for c in range(ML_NCH - 1)] + [L - t]
    gt = jnp.concatenate([(if_ref[r0:r0 + t, :] + bias).T[0:8, :] for r0 in starts], axis=0)
    shape = (8 * ML_NCH, t)
    lane = lax.broadcasted_iota(jnp.int32, shape, 1)
    row = lax.broadcasted_iota(jnp.int32, shape, 0)
    is_i = (row & 7) < ML_HEADS
    log_sig = jnp.minimum(gt, 0.0) - jnp.log1p(jnp.exp(-jnp.abs(gt)))
    lg = jnp.where(is_i, gt, log_sig)
    weightless = jnp.logical_and(row >= 8 * (ML_NCH - 1), lane < ML_TAIL_VALID)
    lg = jnp.where(weightless, jnp.where(is_i, ninf, 0.0), lg)
    cs = jnp.where(is_i, 0.0, lg)
    sh = 1
    while sh < t:
        cs = cs + jnp.where(lane >= sh, pltpu.roll(cs, sh, 1), 0.0)
        sh *= 2
    rowpack = jnp.where(is_i, lg, cs)
    row_scr[...] = rowpack
    pad = jnp.zeros((t - 8, t), F32)
    for c in range(ML_NCH):
        col_scr[c] = jnp.concatenate([rowpack[8 * c:8 * c + 8, :], pad], axis=0).T


def _mlstm_chunk(c, r0, first_valid, q_ref, k_ref, v_ref, o_ref, g_ref, out_ref, s_ref, m_ref, row_scr, col_scr):
    t = ML_T
    ninf = -jnp.inf
    rows = pl.ds(r0, t)
    row0 = c * 8 if isinstance(c, int) else pl.multiple_of(c * 8, 8)
    rowpack = row_scr[pl.ds(row0, 8), :]
    colpack = col_scr[c]

    tri = lax.broadcasted_iota(jnp.int32, (t, t), 0) >= lax.broadcasted_iota(jnp.int32, (t, t), 1)
    nh = ML_HEADS

    def heads(x, d):
        return jnp.stack([x[:, h * d:(h + 1) * d] for h in range(nh)], axis=0)

    q = heads(q_ref[rows, :], ML_DK)
    k = heads(k_ref[rows, :], ML_DK) * jnp.asarray(ML_DK ** -0.5, BF16)
    vext = jnp.concatenate([heads(v_ref[rows, :], ML_DV), jnp.ones((nh, t, ML_EXT - ML_DV), BF16)], axis=2)
    li_r = jnp.stack([rowpack[h:h + 1, :] for h in range(nh)], axis=0)
    b_r = jnp.stack([rowpack[nh + h:nh + h + 1, :] for h in range(nh)], axis=0)
    li_c = jnp.stack([colpack[:, h:h + 1] for h in range(nh)], axis=0)
    b_c = jnp.stack([colpack[:, nh + h:nh + h + 1] for h in range(nh)], axis=0)
    m_prev = m_ref[...][:, :, 0:1]

    dmat = jnp.where(tri[None], b_c - b_r + li_r, ninf)
    inter = b_c + m_prev
    m_t = jnp.maximum(jnp.max(dmat, axis=2, keepdims=True), inter)
    w = jnp.exp(dmat - m_t)
    g = jnp.exp(inter - m_t)
    s = _bdot(q, k, 2, 2) * w
    tot = g * _bdot(q, s_ref[...].astype(BF16), 2, 1) + _bdot(s.astype(BF16), vext, 2, 1)
    num = tot[:, :, :ML_DV]
    den = tot[:, :, ML_DV:ML_DV + 1]
    hh = num / jnp.maximum(jnp.abs(den), jnp.exp(-m_t))
    hn = hh * lax.rsqrt(jnp.mean(hh * hh, axis=2, keepdims=True) + EPS)
    for h in range(nh):
        sl = slice(h * ML_DV, (h + 1) * ML_DV)
        res = (hn[h] * g_ref[:, sl] * _sigmoid(o_ref[rows, sl].astype(F32))).astype(BF16)
        if first_valid:
            out_ref[pl.ds(r0 + first_valid, t - first_valid), sl] = res[first_valid:, :]
        else:
            out_ref[rows, sl] = res

    b_last = b_r[:, :, t - 1:t]
    ds_r = b_last - b_r + li_r
    ds_c = b_last - b_c + li_c
    m_new = jnp.maximum(b_last + m_prev, jnp.max(ds_r, axis=2, keepdims=True))
    wk = jnp.exp(ds_c - m_new)
    gs = jnp.exp(b_last + m_prev - m_new)
    vw = (vext.astype(F32) * wk).astype(BF16)
    s_ref[...] = gs * s_ref[...] + _bdot(k, vw, 1, 1)
    m_ref[...] = jnp.broadcast_to(m_new, (nh, 1, 128))


def _mlstm_kernel(q_ref, k_ref, v_ref, o_ref, if_ref, bias_ref, g_ref, out_ref, s_ref, m_ref, row_scr, col_scr):
    s_ref[...] = jnp.zeros_like(s_ref)
    m_ref[...] = jnp.zeros_like(m_ref)
    _mlstm_gates(if_ref, bias_ref, row_scr, col_scr)
    refs = (q_ref, k_ref, v_ref, o_ref, g_ref, out_ref, s_ref, m_ref, row_scr, col_scr)

    def body(c, carry):
        _mlstm_chunk(c, pl.multiple_of(c * ML_T, ML_T), 0, *refs)
        return carry

    lax.fori_loop(0, ML_NCH - 1, body, 0)
    _mlstm_chunk(ML_NCH - 1, L - ML_T, ML_TAIL_VALID, *refs)


def _mlstm(z, zif, bias, g, bsz):
    m = bsz * L
    qb = Z_MQ // ML_DIM

    def zspec(off):
        return pl.BlockSpec((L, ML_DIM), lambda b: (b, qb + off))

    return pl.pallas_call(
        _mlstm_kernel,
        grid=(bsz,),
        in_specs=[
            zspec(0), zspec(1), zspec(2), zspec(3),
            pl.BlockSpec((L, IF_W), lambda b: (b, 0)),
            pl.BlockSpec((1, IF_W), lambda b: (0, 0)),
            pl.BlockSpec((1, ML_DIM), lambda b: (0, 0)),
        ],
        out_specs=pl.BlockSpec((L, ML_DIM), lambda b: (b, 0)),
        out_shape=jax.ShapeDtypeStruct((m, ML_DIM), BF16),
        scratch_shapes=[pltpu.VMEM((ML_HEADS, ML_DK, ML_EXT), F32),
                        pltpu.VMEM((ML_HEADS, 1, 128), F32),
                        pltpu.VMEM((8 * ML_NCH, ML_T), F32),
                        pltpu.VMEM((ML_NCH, ML_T, 128), F32)],
        compiler_params=_params(("parallel",)),
        name="mlstm",
    )(z, z, z, z, zif, bias, g)


def _attn_block(q, k_ref, v_ref, n_frames, frame0, lam, lam_init, g):
    rows = q.shape[0]
    ninf = -jnp.inf
    first_map = lax.broadcasted_iota(jnp.int32, (rows, DA_VD), 1) < DA_HD
    k_meta = k_ref[0:ML_T, :]
    meta_ok = lax.broadcasted_iota(jnp.int32, (1, ML_T), 1) < N_META
    if n_frames:
        k_fr = k_ref[N_META:N_META + n_frames, :]
        qf = frame0 + lax.broadcasted_iota(jnp.int32, (rows, 1), 0)
        kend = (qf // CHUNK + 1) * CHUNK
        kf = frame0 + lax.broadcasted_iota(jnp.int32, (1, n_frames - frame0), 1)
        diag_ok = kf < kend
    probs = []
    for mp in range(2):
        qm = jnp.where(first_map if mp == 0 else jnp.logical_not(first_map), q, jnp.zeros_like(q))
        pieces = [jnp.where(meta_ok, _dot_nt(qm, k_meta), ninf)]
        if n_frames:
            s_fr = _dot_nt(qm, k_fr)
            if frame0:
                pieces.append(s_fr[:, :frame0])
            pieces.append(jnp.where(diag_ok, s_fr[:, frame0:], ninf))
        s = pieces[0] if len(pieces) == 1 else jnp.concatenate(pieces, axis=1)
        p = jnp.exp2(s - jnp.max(s, axis=1, keepdims=True))
        probs.append((p, jnp.sum(p, axis=1, keepdims=True)))
    a = (probs[0][0] * (1.0 / probs[0][1]) - probs[1][0] * (lam / probs[1][1])).astype(BF16)
    o = _dot(a[:, :ML_T], v_ref[0:ML_T, :])
    if n_frames:
        o = o + _dot(a[:, ML_T:], v_ref[N_META:N_META + n_frames, :])
    return (_rms(o, g) * (1.0 - lam_init)).astype(BF16)


def _attn_kernel(lam_init, q_ref, k_ref, v_ref, lam_ref, g_ref, out_ref):
    lf = lam_ref[...]
    lam = (jnp.exp(jnp.sum(lf[0:1] * lf[1:2], axis=1, keepdims=True))
           - jnp.exp(jnp.sum(lf[2:3] * lf[3:4], axis=1, keepdims=True)) + lam_init)
    scale = DA_HD ** -0.5 * math.log2(math.e)
    g = g_ref[...]

    def scaled(rows):
        return (q_ref[rows, :].astype(F32) * scale).astype(BF16)

    out_ref[0:N_META, :] = _attn_block(scaled(slice(0, N_META)), k_ref, v_ref, 0, 0, lam, lam_init, g)
    for f0 in range(0, SEQ, ATT_QB):
        rows = slice(N_META + f0, N_META + f0 + ATT_QB)
        out_ref[rows, :] = _attn_block(scaled(rows), k_ref, v_ref, f0 + ATT_QB, f0, lam, lam_init, g)


def _attn(z, lam_p, g, layer, bsz):
    lam_init = 0.8 - 0.6 * math.exp(-0.3 * layer)
    m = bsz * L
    qb = Z_DQ // DA_VD

    def zspec(off):
        return pl.BlockSpec((L, DA_VD), lambda b, h: (b, qb + off * DA_HEADS + h))

    return pl.pallas_call(
        functools.partial(_attn_kernel, lam_init),
        grid=(bsz, DA_HEADS),
        in_specs=[
            zspec(0), zspec(1), zspec(2),
            pl.BlockSpec((4, DA_HD), lambda b, h: (0, 0)),
            pl.BlockSpec((1, DA_VD), lambda b, h: (0, 0)),
        ],
        out_specs=pl.BlockSpec((L, DA_VD), lambda b, h: (b, h)),
        out_shape=jax.ShapeDtypeStruct((m, DA_DIM), BF16),
        compiler_params=_params(("parallel", "parallel")),
        name="diff_attn",
    )(z, z, z, lam_p, g)


MERGE_NC = 512


def _merge_kernel(tm, ax_ref, ab_ref, ac_ref, axh_ref, ach_ref, hm_ref, hd_ref,
                  ga_ref, gm_ref, gd_ref, h_ref, cw_ref, wa_ref, wm_ref, wd_ref, wo_ref, g_ref,
                  hn_ref, u_ref, mg_scr):
    pos = _pos_in_batch(pl.program_id(0) * tm, tm)
    cur = ac_ref[...].astype(F32) * ax_ref[...].astype(F32)
    prev = ach_ref[...].astype(F32) * axh_ref[...].astype(F32)
    a_act = (ab_ref[...].astype(F32) * _causal_conv3(prev, cur, cw_ref[...], pos)).astype(BF16)
    hm = hm_ref[...]
    hd = hd_ref[...]
    for c0 in range(0, D_MODEL, MERGE_NC):
        cs = slice(c0, c0 + MERGE_NC)
        merged = (_sigmoid(ga_ref[:, cs].astype(F32)) * _dot(a_act, wa_ref[:, cs])
                  + _sigmoid(gm_ref[:, cs].astype(F32)) * _dot(hm, wm_ref[:, cs])
                  + _sigmoid(gd_ref[:, cs].astype(F32)) * _dot(hd, wd_ref[:, cs]))
        mg_scr[:, cs] = merged.astype(BF16)
    hn = h_ref[...] + _dot(mg_scr[...], wo_ref[...])
    hn_ref[...] = hn
    u_ref[...] = _rms(hn, g_ref[...]).astype(BF16)


def _merge(z, hm, hd, h, layer, conv_a, w_br_a, w_br_m, w_br_d, w_out, g, tm=384):
    m = h.shape[0]
    axb = Z_AX // CONV_DIM
    hb = tm // HALO

    def halo(col):
        return pl.BlockSpec((HALO, CONV_DIM), lambda i: (jnp.maximum(i * hb - 1, 0), col))

    def weight(rows):
        return _resident((None, rows, D_MODEL), lambda i: (layer, 0, 0))

    return pl.pallas_call(
        functools.partial(_merge_kernel, tm),
        grid=(m // tm,),
        in_specs=[
            pl.BlockSpec((tm, CONV_DIM), lambda i: (i, axb)),
            pl.BlockSpec((tm, CONV_DIM), lambda i: (i, axb + 1)),
            pl.BlockSpec((tm, CONV_DIM), lambda i: (i, axb + 2)),
            halo(axb), halo(axb + 2),
            pl.BlockSpec((tm, ML_DIM), lambda i: (i, 0)),
            pl.BlockSpec((tm, DA_DIM), lambda i: (i, 0)),
            pl.BlockSpec((tm, D_MODEL), lambda i: (i, 0)),
            pl.BlockSpec((tm, D_MODEL), lambda i: (i, 1)),
            pl.BlockSpec((tm, D_MODEL), lambda i: (i, 2)),
            pl.BlockSpec((tm, D_MODEL), lambda i: (i, 0)),
            _resident((3, CONV_DIM), lambda i: (0, 0)),
            weight(CONV_DIM), weight(ML_DIM), weight(DA_DIM), weight(D_MODEL),
            _resident((1, D_MODEL), lambda i: (0, 0)),
        ],
        out_specs=[
            pl.BlockSpec((tm, D_MODEL), lambda i: (i, 0)),
            pl.BlockSpec((tm, D_MODEL), lambda i: (i, 0)),
        ],
        out_shape=[jax.ShapeDtypeStruct((m, D_MODEL), F32),
                   jax.ShapeDtypeStruct((m, D_MODEL), BF16)],
        scratch_shapes=[pltpu.VMEM((tm, D_MODEL), BF16)],
        compiler_params=_params(("parallel",)),
        name="merge_out_proj",
    )(z, z, z, z, z, hm, hd, z, z, z, h, conv_a, w_br_a, w_br_m, w_br_d, w_out, g)


FFN_ROWS = 4


def _row_blocks(tm, n):
    units = tm // HALO
    edges = [HALO * (units * k // n) for k in range(n + 1)]
    return list(zip(edges[:-1], edges[1:]))


def _ffn_up_kernel(tm, u_ref, uh_ref, wa_ref, wb_ref, cw_ref, cb_ref, out_ref, w_scr):
    @pl.when(pl.program_id(1) == 0)
    def _():
        w_scr[0] = wa_ref[...].astype(BF16)
        w_scr[1] = wb_ref[...].astype(BF16)

    r_tile = pl.program_id(1) * tm
    a_prev = _dot(uh_ref[...], w_scr[0])
    for r0, r1 in _row_blocks(tm, FFN_ROWS):
        u = u_ref[r0:r1, :]
        a = _dot(u, w_scr[0])
        conv = _causal_conv3(a_prev, a, cw_ref[...], _pos_in_batch(r_tile + r0, r1 - r0)) + cb_ref[...]
        gelu = 0.5 * conv * (1.0 + lax.erf(conv * (2.0 ** -0.5)))
        out_ref[r0:r1, :] = (gelu * _dot(u, w_scr[1])).astype(BF16)
        a_prev = a[r1 - r0 - HALO:, :]


def _ffn_up(u, w_up, layer, conv_w, conv_b, tm=2 * L // 3, tn=512):
    m = u.shape[0]
    nb = D_FF // tn
    hb = tm // HALO
    return pl.pallas_call(
        functools.partial(_ffn_up_kernel, tm),
        grid=(nb, m // tm),
        in_specs=[
            pl.BlockSpec((tm, D_MODEL), lambda j, i: (i, 0)),
            pl.BlockSpec((HALO, D_MODEL), lambda j, i: (jnp.maximum(i * hb - 1, 0), 0)),
            pl.BlockSpec((None, D_MODEL, tn), lambda j, i: (layer, 0, j)),
            pl.BlockSpec((None, D_MODEL, tn), lambda j, i: (layer, 0, nb + j)),
            pl.BlockSpec((3, tn), lambda j, i: (0, j)),
            pl.BlockSpec((1, tn), lambda j, i: (0, j)),
        ],
        out_specs=pl.BlockSpec((tm, tn), lambda j, i: (i, j)),
        out_shape=jax.ShapeDtypeStruct((m, D_FF), BF16),
        scratch_shapes=[pltpu.VMEM((2, D_MODEL, tn), BF16)],
        compiler_params=_params(("parallel", "arbitrary")),
        name="ffn_up",
    )(u, u, w_up, w_up, conv_w, conv_b)


def _ffn_down_kernel(final, act_ref, w_ref, h_ref, g_ref, *out_refs):
    hn = h_ref[...] + _dot(act_ref[...], w_ref[...])
    y = _rms(hn, g_ref[...])
    if final:
        out_refs[0][...] = y
    else:
        out_refs[0][...] = hn
        out_refs[1][...] = y.astype(BF16)


def _ffn_down(act, w_down, layer, h, g, final):
    m = h.shape[0]
    if final:
        tm = 256
        m = m // L * SEQ
        per_batch = SEQ // tm

        def stream_row(i):
            return pl.multiple_of(i * tm + N_META * (i // per_batch + 1), N_META)

        act_spec = pl.BlockSpec((pl.Element(tm), pl.Element(D_FF)), lambda i: (stream_row(i), 0))
        h_spec = pl.BlockSpec((pl.Element(tm), pl.Element(D_MODEL)), lambda i: (stream_row(i), 0))
        row = pl.BlockSpec((tm, D_MODEL), lambda i: (i, 0))
        out_specs = [row]
        out_shape = [jax.ShapeDtypeStruct((m, D_MODEL), F32)]
    else:
        tm = 384
        row = pl.BlockSpec((tm, D_MODEL), lambda i: (i, 0))
        act_spec = pl.BlockSpec((tm, D_FF), lambda i: (i, 0))
        h_spec = row
        out_specs = [row, row]
        out_shape = [jax.ShapeDtypeStruct((m, D_MODEL), F32),
                     jax.ShapeDtypeStruct((m, D_MODEL), BF16)]
    return pl.pallas_call(
        functools.partial(_ffn_down_kernel, final),
        grid=(m // tm,),
        in_specs=[
            act_spec,
            _resident((None, D_FF, D_MODEL), lambda i: (layer, 0, 0)),
            h_spec,
            _resident((1, D_MODEL), lambda i: (0, 0)),
        ],
        out_specs=out_specs,
        out_shape=out_shape,
        compiler_params=_params(("parallel",)),
        name="ffn_down_final" if final else "ffn_down",
    )(act, w_down, h, g)


def kernel(x, meta, norm_mix, w_in, conv_a, b_if, ml_norm, da_lambda, da_norm, w_br_a, w_br_m, w_br_d,
           w_out, norm_ffn, w_up, conv_ffn, conv_ffn_b, w_down, norm_f):
    bsz, seq, d = x.shape
    assert (seq, d) == (SEQ, D_MODEL)
    depth = w_in.shape[0]
    w_in_t = jnp.swapaxes(w_in, 1, 2).reshape(depth * W_IN, D_MODEL)
    wa, wm, wd, wo, wdn = (w.astype(BF16) for w in (w_br_a, w_br_m, w_br_d, w_out, w_down))
    h, u = _embed(x.reshape(bsz * SEQ, D_MODEL), meta, norm_mix[0][None], bsz)
    for i in range(depth):
        z, zif = _inproj(u, w_in_t, i)
        bias = jnp.pad(b_if[i].reshape(1, 2 * ML_HEADS), ((0, 0), (0, IF_W - 2 * ML_HEADS)))
        hm = _mlstm(z, zif, bias, ml_norm[i][None], bsz)
        hd = _attn(z, da_lambda[i], da_norm[i][None], i, bsz)
        h, u = _merge(z, hm, hd, h, i, conv_a[i], wa, wm, wd, wo, norm_ffn[i][None])
        act = _ffn_up(u, w_up, i, conv_ffn[i], conv_ffn_b[i][None])
        final = i == depth - 1
        g_next = norm_f if final else norm_mix[i + 1]
        outs = _ffn_down(act, wdn, i, h, g_next[None], final)
        if final:
            y = outs[0]
        else:
            h, u = outs
    return y.reshape(bsz, SEQ, D_MODEL)
```

```python
import functools
import math

import jax
import jax.numpy as jnp
from jax import lax
from jax.experimental import pallas as pl
from jax.experimental.pallas import tpu as pltpu

F32 = jnp.float32
BF16 = jnp.bfloat16

D_MODEL = 2048
SEQ = 2048
N_META = 16
L = N_META + SEQ
CHUNK = 64
EPS = 1e-6

CONV_DIM = 512
ML_HEADS = 4
ML_DK = 256
ML_DV = 256
ML_DIM = ML_HEADS * ML_DV
DA_HEADS = 4
DA_HD = 64
DA_VD = 2 * DA_HD
DA_DIM = DA_HEADS * DA_VD
D_FF = 5632

Z_GATE = 0
Z_MQ = 3 * D_MODEL
Z_AX = Z_MQ + 4 * ML_DIM
Z_DQ = Z_AX + 3 * CONV_DIM
Z_N = Z_DQ + 3 * DA_DIM
IF_W = 128

ML_T = 128
ML_EXT = ML_DV + 128
HALO = 16
ATT_QB = 256
EMB_T = L // 3
VMEM_LIMIT = 56 * 1024 * 1024


def _dot(a, b):
    return jnp.dot(a, b, preferred_element_type=F32)


def _dot_nt(a, b):
    return lax.dot_general(a, b, (((1,), (1,)), ((), ())), preferred_element_type=F32)


def _dot_tn(a, b):
    return lax.dot_general(a, b, (((0,), (0,)), ((), ())), preferred_element_type=F32)


def _bdot(a, b, ca, cb):
    return lax.dot_general(a, b, (((ca,), (cb,)), ((0,), (0,))), preferred_element_type=F32)


def _rms(x, g):
    return x * lax.rsqrt(jnp.mean(x * x, axis=-1, keepdims=True) + EPS) * g


def _sigmoid(x):
    return 1.0 / (1.0 + jnp.exp(-x))


def _pos_in_batch(r0, tm):
    pos = r0 % L + lax.broadcasted_iota(jnp.int32, (tm, 1), 0)
    return jnp.where(pos >= L, pos - L, pos)


def _causal_conv3(prev, cur, w, pos):
    cc = jnp.concatenate([prev, cur], axis=0)
    x1 = jnp.where(pos >= 1, pltpu.roll(cc, 1, 0)[HALO:], 0.0)
    x2 = jnp.where(pos >= 2, pltpu.roll(cc, 2, 0)[HALO:], 0.0)
    return w[0:1] * x2 + w[1:2] * x1 + w[2:3] * cur


def _params(sem):
    return pltpu.CompilerParams(dimension_semantics=sem, vmem_limit_bytes=VMEM_LIMIT)


def _resident(shape, index_map):
    return pl.BlockSpec(shape, index_map, pipeline_mode=pl.Buffered(1))


def _embed_kernel(x_ref, meta_ref, g_ref, h_ref, u_ref):
    j = pl.program_id(1)

    @pl.when(j == 0)
    def _():
        rows = jnp.concatenate([meta_ref[...], x_ref[0:EMB_T - N_META, :]], axis=0)
        h_ref[...] = rows
        u_ref[...] = _rms(rows, g_ref[...]).astype(BF16)

    @pl.when(j > 0)
    def _():
        rows = x_ref[...]
        h_ref[...] = rows
        u_ref[...] = _rms(rows, g_ref[...]).astype(BF16)


def _embed(x2d, meta, g, bsz):
    nb = L // EMB_T
    m = bsz * L
    return pl.pallas_call(
        _embed_kernel,
        grid=(bsz, nb),
        in_specs=[
            pl.BlockSpec((pl.Element(EMB_T), pl.Element(D_MODEL)),
                         lambda b, j: (pl.multiple_of(b * SEQ + jnp.maximum(j * EMB_T - N_META, 0), 16), 0)),
            pl.BlockSpec((N_META, D_MODEL), lambda b, j: (0, 0)),
            pl.BlockSpec((1, D_MODEL), lambda b, j: (0, 0)),
        ],
        out_specs=[
            pl.BlockSpec((EMB_T, D_MODEL), lambda b, j: (b * nb + j, 0)),
            pl.BlockSpec((EMB_T, D_MODEL), lambda b, j: (b * nb + j, 0)),
        ],
        out_shape=[jax.ShapeDtypeStruct((m, D_MODEL), F32),
                   jax.ShapeDtypeStruct((m, D_MODEL), BF16)],
        compiler_params=_params(("parallel", "arbitrary")),
        name="embed_norm",
    )(x2d, meta, g)


W_A_END = 3 * CONV_DIM
W_M_END = W_A_END + 4 * ML_DIM
W_IF_END = W_M_END + 2 * ML_HEADS
W_D_END = W_IF_END + 3 * DA_DIM
W_IN = W_D_END + 3 * D_MODEL


IN_SRC = 512
IN_NSRC = 2


def _inproj_kernel(u_ref, *refs):
    wt_refs, wift_ref, z_ref, zif_ref = refs[:IN_NSRC], refs[IN_NSRC], refs[IN_NSRC + 1], refs[IN_NSRC + 2]
    u = u_ref[...]
    for k, wt_ref in enumerate(wt_refs):
        z_ref[:, k * IN_SRC:(k + 1) * IN_SRC] = _dot_nt(u, wt_ref[...].astype(BF16)).astype(BF16)

    @pl.when(pl.program_id(1) == 0)
    def _():
        zif_ref[...] = _dot_nt(u, wift_ref[...].astype(BF16))


def _inproj(u, w_in_t, layer, tm=L):
    m = u.shape[0]
    tn = IN_NSRC * IN_SRC
    n_gate, n_m, n_a = 3 * D_MODEL // IN_SRC, 4 * ML_DIM // IN_SRC, 3 * CONV_DIM // IN_SRC

    def src_row(t):
        gate = W_D_END + IN_SRC * t
        mls = W_A_END + IN_SRC * (t - n_gate)
        cnv = IN_SRC * (t - n_gate - n_m)
        att = W_IF_END + IN_SRC * (t - n_gate - n_m - n_a)
        row = jnp.where(t < n_gate, gate,
                        jnp.where(t < n_gate + n_m, mls, jnp.where(t < n_gate + n_m + n_a, cnv, att)))
        return pl.multiple_of(layer * W_IN + row, 8)

    def src_spec(k):
        return pl.BlockSpec((pl.Element(IN_SRC), pl.Element(D_MODEL)),
                            lambda i, j: (src_row(IN_NSRC * j + k), 0))

    return pl.pallas_call(
        _inproj_kernel,
        grid=(m // tm, Z_N // tn),
        in_specs=[pl.BlockSpec((tm, D_MODEL), lambda i, j: (i, 0), pipeline_mode=pl.Buffered(1))]
        + [src_spec(k) for k in range(IN_NSRC)]
        + [_resident((pl.Element(IF_W), pl.Element(D_MODEL)), lambda i, j: (layer * W_IN + W_M_END, 0))],
        out_specs=[
            pl.BlockSpec((tm, tn), lambda i, j: (i, j)),
            pl.BlockSpec((tm, IF_W), lambda i, j: (i, 0)),
        ],
        out_shape=[jax.ShapeDtypeStruct((m, Z_N), BF16),
                   jax.ShapeDtypeStruct((m, IF_W), F32)],
        compiler_params=_params(("parallel", "arbitrary")),
        name="in_proj",
    )(u, *([w_in_t] * (IN_NSRC + 1)))


ML_NCH = -(-L // ML_T)
ML_TAIL_VALID = ML_T - (L - (ML_NCH - 1) * ML_T)


def _mlstm_gates(if_ref, bias_ref, row_scr, col_scr):
    t = ML_T
    ninf = -jnp.inf
    bias = bias_ref[...]
    starts = [c * t for c in range(ML_NCH - 1)] + [L - t]
    gt = jnp.concatenate([(if_ref[r0:r0 + t, :] + bias).T[0:8, :] for r0 in starts], axis=0)
    shape = (8 * ML_NCH, t)
    lane = lax.broadcasted_iota(jnp.int32, shape, 1)
    row = lax.broadcasted_iota(jnp.int32, shape, 0)
    is_i = (row & 7) < ML_HEADS
    log_sig = jnp.minimum(gt, 0.0) - jnp.log1p(jnp.exp(-jnp.abs(gt)))
    lg = jnp.where(is_i, gt, log_sig)
    weightless = jnp.logical_and(row >= 8 * (ML_NCH - 1), lane < ML_TAIL_VALID)
    lg = jnp.where(weightless, jnp.where(is_i, ninf, 0.0), lg)
    cs = jnp.where(is_i, 0.0, lg)
    sh = 1
    while sh < t:
        cs = cs + jnp.where(lane >= sh, pltpu.roll(cs, sh, 1), 0.0)
        sh *= 2
    rowpack = jnp.where(is_i, lg, cs)
    row_scr[...] = rowpack
    pad = jnp.zeros((t - 8, t), F32)
    for c in range(ML_NCH):
        col_scr[c] = jnp.concatenate([rowpack[8 * c:8 * c + 8, :], pad], axis=0).T


def _mlstm_chunk(c, r0, first_valid, q_ref, k_ref, v_ref, o_ref, g_ref, out_ref, s_ref, m_ref, row_scr, col_scr):
    t = ML_T
    ninf = -jnp.inf
    rows = pl.ds(r0, t)
    row0 = c * 8 if isinstance(c, int) else pl.multiple_of(c * 8, 8)
    rowpack = row_scr[pl.ds(row0, 8), :]
    colpack = col_scr[c]

    tri = lax.broadcasted_iota(jnp.int32, (t, t), 0) >= lax.broadcasted_iota(jnp.int32, (t, t), 1)
    nh = ML_HEADS

    def heads(x, d):
        return jnp.stack([x[:, h * d:(h + 1) * d] for h in range(nh)], axis=0)

    q = heads(q_ref[rows, :], ML_DK)
    k = heads(k_ref[rows, :], ML_DK) * jnp.asarray(ML_DK ** -0.5, BF16)
    vext = jnp.concatenate([heads(v_ref[rows, :], ML_DV), jnp.ones((nh, t, ML_EXT - ML_DV), BF16)], axis=2)
    li_r = jnp.stack([rowpack[h:h + 1, :] for h in range(nh)], axis=0)
    b_r = jnp.stack([rowpack[nh + h:nh + h + 1, :] for h in range(nh)], axis=0)
    li_c = jnp.stack([colpack[:, h:h + 1] for h in range(nh)], axis=0)
    b_c = jnp.stack([colpack[:, nh + h:nh + h + 1] for h in range(nh)], axis=0)
    m_prev = m_ref[...][:, :, 0:1]

    dmat = jnp.where(tri[None], b_c - b_r + li_r, ninf)
    inter = b_c + m_prev
    m_t = jnp.maximum(jnp.max(dmat, axis=2, keepdims=True), inter)
    w = jnp.exp(dmat - m_t)
    g = jnp.exp(inter - m_t)
    s = _bdot(q, k, 2, 2) * w
    tot = g * _bdot(q, s_ref[...].astype(BF16), 2, 1) + _bdot(s.astype(BF16), vext, 2, 1)
    num = tot[:, :, :ML_DV]
    den = tot[:, :, ML_DV:ML_DV + 1]
    hh = num / jnp.maximum(jnp.abs(den), jnp.exp(-m_t))
    hn = hh * lax.rsqrt(jnp.mean(hh * hh, axis=2, keepdims=True) + EPS)
    for h in range(nh):
        sl = slice(h * ML_DV, (h + 1) * ML_DV)
        res = (hn[h] * g_ref[:, sl] * _sigmoid(o_ref[rows, sl].astype(F32))).astype(BF16)
        if first_valid:
            out_ref[pl.ds(r0 + first_valid, t - first_valid), sl] = res[first_valid:, :]
        else:
            out_ref[rows, sl] = res

    b_last = b_r[:, :, t - 1:t]
    ds_r = b_last - b_r + li_r
    ds_c = b_last - b_c + li_c
    m_new = jnp.maximum(b_last + m_prev, jnp.max(ds_r, axis=2, keepdims=True))
    wk = jnp.exp(ds_c - m_new)
    gs = jnp.exp(b_last + m_prev - m_new)
    vw = (vext.astype(F32) * wk).astype(BF16)
    s_ref[...] = gs * s_ref[...] + _bdot(k, vw, 1, 1)
    m_ref[...] = jnp.broadcast_to(m_new, (nh, 1, 128))


def _mlstm_kernel(q_ref, k_ref, v_ref, o_ref, if_ref, bias_ref, g_ref, out_ref, s_ref, m_ref, row_scr, col_scr):
    s_ref[...] = jnp.zeros_like(s_ref)
    m_ref[...] = jnp.zeros_like(m_ref)
    _mlstm_gates(if_ref, bias_ref, row_scr, col_scr)
    refs = (q_ref, k_ref, v_ref, o_ref, g_ref, out_ref, s_ref, m_ref, row_scr, col_scr)

    def body(c, carry):
        _mlstm_chunk(c, pl.multiple_of(c * ML_T, ML_T), 0, *refs)
        return carry

    lax.fori_loop(0, ML_NCH - 1, body, 0)
    _mlstm_chunk(ML_NCH - 1, L - ML_T, ML_TAIL_VALID, *refs)


def _mlstm(z, zif, bias, g, bsz):
    m = bsz * L
    qb = Z_MQ // ML_DIM

    def zspec(off):
        return pl.BlockSpec((L, ML_DIM), lambda b: (b, qb + off))

    return pl.pallas_call(
        _mlstm_kernel,
        grid=(bsz,),
        in_specs=[
            zspec(0), zspec(1), zspec(2), zspec(3),
            pl.BlockSpec((L, IF_W), lambda b: (b, 0)),
            pl.BlockSpec((1, IF_W), lambda b: (0, 0)),
            pl.BlockSpec((1, ML_DIM), lambda b: (0, 0)),
        ],
        out_specs=pl.BlockSpec((L, ML_DIM), lambda b: (b, 0)),
        out_shape=jax.ShapeDtypeStruct((m, ML_DIM), BF16),
        scratch_shapes=[pltpu.VMEM((ML_HEADS, ML_DK, ML_EXT), F32),
                        pltpu.VMEM((ML_HEADS, 1, 128), F32),
                        pltpu.VMEM((8 * ML_NCH, ML_T), F32),
                        pltpu.VMEM((ML_NCH, ML_T, 128), F32)],
        compiler_params=_params(("parallel",)),
        name="mlstm",
    )(z, z, z, z, zif, bias, g)


def _attn_block(q, k_ref, v_ref, n_frames, frame0, lam, lam_init, g):
    rows = q.shape[0]
    ninf = -jnp.inf
    first_map = lax.broadcasted_iota(jnp.int32, (rows, DA_VD), 1) < DA_HD
    k_meta = k_ref[0:ML_T, :]
    meta_ok = lax.broadcasted_iota(jnp.int32, (1, ML_T), 1) < N_META
    if n_frames:
        k_fr = k_ref[N_META:N_META + n_frames, :]
        qf = frame0 + lax.broadcasted_iota(jnp.int32, (rows, 1), 0)
        kend = (qf // CHUNK + 1) * CHUNK
        kf = frame0 + lax.broadcasted_iota(jnp.int32, (1, n_frames - frame0), 1)
        diag_ok = kf < kend
    probs = []
    for mp in range(2):
        qm = jnp.where(first_map if mp == 0 else jnp.logical_not(first_map), q, jnp.zeros_like(q))
        pieces = [jnp.where(meta_ok, _dot_nt(qm, k_meta), ninf)]
        if n_frames:
            s_fr = _dot_nt(qm, k_fr)
            if frame0:
                pieces.append(s_fr[:, :frame0])
            pieces.append(jnp.where(diag_ok, s_fr[:, frame0:], ninf))
        s = pieces[0] if len(pieces) == 1 else jnp.concatenate(pieces, axis=1)
        p = jnp.exp2(s - jnp.max(s, axis=1, keepdims=True))
        probs.append((p, jnp.sum(p, axis=1, keepdims=True)))
    a = (probs[0][0] * (1.0 / probs[0][1]) - probs[1][0] * (lam / probs[1][1])).astype(BF16)
    o = _dot(a[:, :ML_T], v_ref[0:ML_T, :])
    if n_frames:
        o = o + _dot(a[:, ML_T:], v_ref[N_META:N_META + n_frames, :])
    return (_rms(o, g) * (1.0 - lam_init)).astype(BF16)


def _attn_kernel(lam_init, q_ref, k_ref, v_ref, lam_ref, g_ref, out_ref):
    lf = lam_ref[...]
    lam = (jnp.exp(jnp.sum(lf[0:1] * lf[1:2], axis=1, keepdims=True))
           - jnp.exp(jnp.sum(lf[2:3] * lf[3:4], axis=1, keepdims=True)) + lam_init)
    scale = DA_HD ** -0.5 * math.log2(math.e)
    g = g_ref[...]

    def scaled(rows):
        return (q_ref[rows, :].astype(F32) * scale).astype(BF16)

    out_ref[0:N_META, :] = _attn_block(scaled(slice(0, N_META)), k_ref, v_ref, 0, 0, lam, lam_init, g)
    for f0 in range(0, SEQ, ATT_QB):
        rows = slice(N_META + f0, N_META + f0 + ATT_QB)
        out_ref[rows, :] = _attn_block(scaled(rows), k_ref, v_ref, f0 + ATT_QB, f0, lam, lam_init, g)


def _attn(z, lam_p, g, layer, bsz):
    lam_init = 0.8 - 0.6 * math.exp(-0.3 * layer)
    m = bsz * L
    qb = Z_DQ // DA_VD

    def zspec(off):
        return pl.BlockSpec((L, DA_VD), lambda b, h: (b, qb + off * DA_HEADS + h))

    return pl.pallas_call(
        functools.partial(_attn_kernel, lam_init),
        grid=(bsz, DA_HEADS),
        in_specs=[
            zspec(0), zspec(1), zspec(2),
            pl.BlockSpec((4, DA_HD), lambda b, h: (0, 0)),
            pl.BlockSpec((1, DA_VD), lambda b, h: (0, 0)),
        ],
        out_specs=pl.BlockSpec((L, DA_VD), lambda b, h: (b, h)),
        out_shape=jax.ShapeDtypeStruct((m, DA_DIM), BF16),
        compiler_params=_params(("parallel", "parallel")),
        name="diff_attn",
    )(z, z, z, lam_p, g)


MERGE_NC = 512


def _merge_kernel(tm, ax_ref, ab_ref, ac_ref, axh_ref, ach_ref, hm_ref, hd_ref,
                  ga_ref, gm_ref, gd_ref, h_ref, cw_ref, wa_ref, wm_ref, wd_ref, wo_ref, g_ref,
                  hn_ref, u_ref, mg_scr):
    pos = _pos_in_batch(pl.program_id(0) * tm, tm)
    cur = ac_ref[...].astype(F32) * ax_ref[...].astype(F32)
    prev = ach_ref[...].astype(F32) * axh_ref[...].astype(F32)
    a_act = (ab_ref[...].astype(F32) * _causal_conv3(prev, cur, cw_ref[...], pos)).astype(BF16)
    hm = hm_ref[...]
    hd = hd_ref[...]
    for c0 in range(0, D_MODEL, MERGE_NC):
        cs = slice(c0, c0 + MERGE_NC)
        merged = (_sigmoid(ga_ref[:, cs].astype(F32)) * _dot(a_act, wa_ref[:, cs])
                  + _sigmoid(gm_ref[:, cs].astype(F32)) * _dot(hm, wm_ref[:, cs])
                  + _sigmoid(gd_ref[:, cs].astype(F32)) * _dot(hd, wd_ref[:, cs]))
        mg_scr[:, cs] = merged.astype(BF16)
    hn = h_ref[...] + _dot(mg_scr[...], wo_ref[...])
    hn_ref[...] = hn
    u_ref[...] = _rms(hn, g_ref[...]).astype(BF16)


def _merge(z, hm, hd, h, layer, conv_a, w_br_a, w_br_m, w_br_d, w_out, g, tm=384):
    m = h.shape[0]
    axb = Z_AX // CONV_DIM
    hb = tm // HALO

    def halo(col):
        return pl.BlockSpec((HALO, CONV_DIM), lambda i: (jnp.maximum(i * hb - 1, 0), col))

    def weight(rows):
        return _resident((None, rows, D_MODEL), lambda i: (layer, 0, 0))

    return pl.pallas_call(
        functools.partial(_merge_kernel, tm),
        grid=(m // tm,),
        in_specs=[
            pl.BlockSpec((tm, CONV_DIM), lambda i: (i, axb)),
            pl.BlockSpec((tm, CONV_DIM), lambda i: (i, axb + 1)),
            pl.BlockSpec((tm, CONV_DIM), lambda i: (i, axb + 2)),
            halo(axb), halo(axb + 2),
            pl.BlockSpec((tm, ML_DIM), lambda i: (i, 0)),
            pl.BlockSpec((tm, DA_DIM), lambda i: (i, 0)),
            pl.BlockSpec((tm, D_MODEL), lambda i: (i, 0)),
            pl.BlockSpec((tm, D_MODEL), lambda i: (i, 1)),
            pl.BlockSpec((tm, D_MODEL), lambda i: (i, 2)),
            pl.BlockSpec((tm, D_MODEL), lambda i: (i, 0)),
            _resident((3, CONV_DIM), lambda i: (0, 0)),
            weight(CONV_DIM), weight(ML_DIM), weight(DA_DIM), weight(D_MODEL),
            _resident((1, D_MODEL), lambda i: (0, 0)),
        ],
        out_specs=[
            pl.BlockSpec((tm, D_MODEL), lambda i: (i, 0)),
            pl.BlockSpec((tm, D_MODEL), lambda i: (i, 0)),
        ],
        out_shape=[jax.ShapeDtypeStruct((m, D_MODEL), F32),
                   jax.ShapeDtypeStruct((m, D_MODEL), BF16)],
        scratch_shapes=[pltpu.VMEM((tm, D_MODEL), BF16)],
        compiler_params=_params(("parallel",)),
        name="merge_out_proj",
    )(z, z, z, z, z, hm, hd, z, z, z, h, conv_a, w_br_a, w_br_m, w_br_d, w_out, g)


FFN_ROWS = 2


def _row_blocks(tm, n):
    units = tm // HALO
    edges = [HALO * (units * k // n) for k in range(n + 1)]
    return list(zip(edges[:-1], edges[1:]))


def _ffn_up_kernel(tm, u_ref, uh_ref, wa_ref, wb_ref, cw_ref, cb_ref, out_ref, w_scr):
    @pl.when(pl.program_id(1) == 0)
    def _():
        w_scr[0] = wa_ref[...].astype(BF16)
        w_scr[1] = wb_ref[...].astype(BF16)

    r_tile = pl.program_id(1) * tm
    a_prev = _dot(uh_ref[...], w_scr[0])
    for r0, r1 in _row_blocks(tm, FFN_ROWS):
        u = u_ref[r0:r1, :]
        a = _dot(u, w_scr[0])
        conv = _causal_conv3(a_prev, a, cw_ref[...], _pos_in_batch(r_tile + r0, r1 - r0)) + cb_ref[...]
        gelu = 0.5 * conv * (1.0 + lax.erf(conv * (2.0 ** -0.5)))
        out_ref[r0:r1, :] = (gelu * _dot(u, w_scr[1])).astype(BF16)
        a_prev = a[r1 - r0 - HALO:, :]


def _ffn_up(u, w_up, layer, conv_w, conv_b, tm=L, tn=512):
    m = u.shape[0]
    nb = D_FF // tn
    hb = tm // HALO
    return pl.pallas_call(
        functools.partial(_ffn_up_kernel, tm),
        grid=(nb, m // tm),
        in_specs=[
            pl.BlockSpec((tm, D_MODEL), lambda j, i: (i, 0)),
            pl.BlockSpec((HALO, D_MODEL), lambda j, i: (jnp.maximum(i * hb - 1, 0), 0)),
            pl.BlockSpec((None, D_MODEL, tn), lambda j, i: (layer, 0, j)),
            pl.BlockSpec((None, D_MODEL, tn), lambda j, i: (layer, 0, nb + j)),
            pl.BlockSpec((3, tn), lambda j, i: (0, j)),
            pl.BlockSpec((1, tn), lambda j, i: (0, j)),
        ],
        out_specs=pl.BlockSpec((tm, tn), lambda j, i: (i, j)),
        out_shape=jax.ShapeDtypeStruct((m, D_FF), BF16),
        scratch_shapes=[pltpu.VMEM((2, D_MODEL, tn), BF16)],
        compiler_params=_params(("parallel", "arbitrary")),
        name="ffn_up",
    )(u, u, w_up, w_up, conv_w, conv_b)


def _ffn_down_kernel(final, act_ref, w_ref, h_ref, g_ref, *out_refs):
    hn = h_ref[...] + _dot(act_ref[...], w_ref[...])
    y = _rms(hn, g_ref[...])
    if final:
        out_refs[0][...] = y
    else:
        out_refs[0][...] = hn
        out_refs[1][...] = y.astype(BF16)


def _ffn_down(act, w_down, layer, h, g, final):
    m = h.shape[0]
    if final:
        tm = 256
        m = m // L * SEQ
        per_batch = SEQ // tm

        def stream_row(i):
            return pl.multiple_of(i * tm + N_META * (i // per_batch + 1), N_META)

        act_spec = pl.BlockSpec((pl.Element(tm), pl.Element(D_FF)), lambda i: (stream_row(i), 0))
        h_spec = pl.BlockSpec((pl.Element(tm), pl.Element(D_MODEL)), lambda i: (stream_row(i), 0))
        row = pl.BlockSpec((tm, D_MODEL), lambda i: (i, 0))
        out_specs = [row]
        out_shape = [jax.ShapeDtypeStruct((m, D_MODEL), F32)]
    else:
        tm = 384
        row = pl.BlockSpec((tm, D_MODEL), lambda i: (i, 0))
        act_spec = pl.BlockSpec((tm, D_FF), lambda i: (i, 0))
        h_spec = row
        out_specs = [row, row]
        out_shape = [jax.ShapeDtypeStruct((m, D_MODEL), F32),
                     jax.ShapeDtypeStruct((m, D_MODEL), BF16)]
    return pl.pallas_call(
        functools.partial(_ffn_down_kernel, final),
        grid=(m // tm,),
        in_specs=[
            act_spec,
            _resident((None, D_FF, D_MODEL), lambda i: (layer, 0, 0)),
            h_spec,
            _resident((1, D_MODEL), lambda i: (0, 0)),
        ],
        out_specs=out_specs,
        out_shape=out_shape,
        compiler_params=_params(("parallel",)),
        name="ffn_down_final" if final else "ffn_down",
    )(act, w_down, h, g)


def kernel(x, meta, norm_mix, w_in, conv_a, b_if, ml_norm, da_lambda, da_norm, w_br_a, w_br_m, w_br_d,
           w_out, norm_ffn, w_up, conv_ffn, conv_ffn_b, w_down, norm_f):
    bsz, seq, d = x.shape
    assert (seq, d) == (SEQ, D_MODEL)
    depth = w_in.shape[0]
    w_in_t = jnp.swapaxes(w_in, 1, 2).reshape(depth * W_IN, D_MODEL)
    wa, wm, wd, wo, wdn = (w.astype(BF16) for w in (w_br_a, w_br_m, w_br_d, w_out, w_down))
    h, u = _embed(x.reshape(bsz * SEQ, D_MODEL), meta, norm_mix[0][None], bsz)
    for i in range(depth):
        z, zif = _inproj(u, w_in_t, i)
        bias = jnp.pad(b_if[i].reshape(1, 2 * ML_HEADS), ((0, 0), (0, IF_W - 2 * ML_HEADS)))
        hm = _mlstm(z, zif, bias, ml_norm[i][None], bsz)
        hd = _attn(z, da_lambda[i], da_norm[i][None], i, bsz)
        h, u = _merge(z, hm, hd, h, i, conv_a[i], wa, wm, wd, wo, norm_ffn[i][None])
        act = _ffn_up(u, w_up, i, conv_ffn[i], conv_ffn_b[i][None])
        final = i == depth - 1
        g_next = norm_f if final else norm_mix[i + 1]
        outs = _ffn_down(act, wdn, i, h, g_next[None], final)
        if final:
            y = outs[0]
        else:
            h, u = outs
    return y.reshape(bsz, SEQ, D_MODEL)
```

```python
import functools
import math

import jax
import jax.numpy as jnp
from jax import lax
from jax.experimental import pallas as pl
from jax.experimental.pallas import tpu as pltpu

F32 = jnp.float32
BF16 = jnp.bfloat16

D_MODEL = 2048
SEQ = 2048
N_META = 16
L = N_META + SEQ
CHUNK = 64
EPS = 1e-6

CONV_DIM = 512
ML_HEADS = 4
ML_DK = 256
ML_DV = 256
ML_DIM = ML_HEADS * ML_DV
DA_HEADS = 4
DA_HD = 64
DA_VD = 2 * DA_HD
DA_DIM = DA_HEADS * DA_VD
D_FF = 5632

Z_GATE = 0
Z_MQ = 3 * D_MODEL
Z_AX = Z_MQ + 4 * ML_DIM
Z_DK = Z_AX + 3 * CONV_DIM
Z_N = Z_DK + DA_DIM
IF_W = 128

ML_T = 128
ML_EXT = ML_DV + 128
HALO = 16
ATT_QB = 256
EMB_T = L // 3
VMEM_LIMIT = 56 * 1024 * 1024


def _dot(a, b):
    return jnp.dot(a, b, preferred_element_type=F32)


def _dot_nt(a, b):
    return lax.dot_general(a, b, (((1,), (1,)), ((), ())), preferred_element_type=F32)


def _dot_tn(a, b):
    return lax.dot_general(a, b, (((0,), (0,)), ((), ())), preferred_element_type=F32)


def _bdot(a, b, ca, cb):
    return lax.dot_general(a, b, (((ca,), (cb,)), ((0,), (0,))), preferred_element_type=F32)


def _rms(x, g):
    return x * lax.rsqrt(jnp.mean(x * x, axis=-1, keepdims=True) + EPS) * g


def _sigmoid(x):
    return 1.0 / (1.0 + jnp.exp(-x))


def _pos_in_batch(r0, tm):
    pos = r0 % L + lax.broadcasted_iota(jnp.int32, (tm, 1), 0)
    return jnp.where(pos >= L, pos - L, pos)


def _causal_conv3(prev, cur, w, pos):
    cc = jnp.concatenate([prev, cur], axis=0)
    x1 = jnp.where(pos >= 1, pltpu.roll(cc, 1, 0)[HALO:], 0.0)
    x2 = jnp.where(pos >= 2, pltpu.roll(cc, 2, 0)[HALO:], 0.0)
    return w[0:1] * x2 + w[1:2] * x1 + w[2:3] * cur


def _params(sem):
    return pltpu.CompilerParams(dimension_semantics=sem, vmem_limit_bytes=VMEM_LIMIT)


def _resident(shape, index_map):
    return pl.BlockSpec(shape, index_map, pipeline_mode=pl.Buffered(1))


def _embed_kernel(x_ref, meta_ref, g_ref, h_ref, u_ref):
    j = pl.program_id(1)

    @pl.when(j == 0)
    def _():
        rows = jnp.concatenate([meta_ref[...], x_ref[0:EMB_T - N_META, :]], axis=0)
        h_ref[...] = rows
        u_ref[...] = _rms(rows, g_ref[...]).astype(BF16)

    @pl.when(j > 0)
    def _():
        rows = x_ref[...]
        h_ref[...] = rows
        u_ref[...] = _rms(rows, g_ref[...]).astype(BF16)


def _embed(x2d, meta, g, bsz):
    nb = L // EMB_T
    m = bsz * L
    return pl.pallas_call(
        _embed_kernel,
        grid=(bsz, nb),
        in_specs=[
            pl.BlockSpec((pl.Element(EMB_T), pl.Element(D_MODEL)),
                         lambda b, j: (pl.multiple_of(b * SEQ + jnp.maximum(j * EMB_T - N_META, 0), 16), 0)),
            pl.BlockSpec((N_META, D_MODEL), lambda b, j: (0, 0)),
            pl.BlockSpec((1, D_MODEL), lambda b, j: (0, 0)),
        ],
        out_specs=[
            pl.BlockSpec((EMB_T, D_MODEL), lambda b, j: (b * nb + j, 0)),
            pl.BlockSpec((EMB_T, D_MODEL), lambda b, j: (b * nb + j, 0)),
        ],
        out_shape=[jax.ShapeDtypeStruct((m, D_MODEL), F32),
                   jax.ShapeDtypeStruct((m, D_MODEL), BF16)],
        compiler_params=_params(("parallel", "arbitrary")),
        name="embed_norm",
    )(x2d, meta, g)


W_A_END = 3 * CONV_DIM
W_M_END = W_A_END + 4 * ML_DIM
W_IF_END = W_M_END + 2 * ML_HEADS
W_D_END = W_IF_END + 3 * DA_DIM
W_IN = W_D_END + 3 * D_MODEL


IN_SRC = 512
IN_NSRC = 2


def _inproj_kernel(u_ref, *refs):
    wt_refs, wift_ref, z_ref, zif_ref = refs[:IN_NSRC], refs[IN_NSRC], refs[IN_NSRC + 1], refs[IN_NSRC + 2]
    u = u_ref[...]
    for k, wt_ref in enumerate(wt_refs):
        z_ref[:, k * IN_SRC:(k + 1) * IN_SRC] = _dot_nt(u, wt_ref[...].astype(BF16)).astype(BF16)

    @pl.when(pl.program_id(1) == 0)
    def _():
        zif_ref[...] = _dot_nt(u, wift_ref[...].astype(BF16))


def _inproj(u, w_in_t, layer, tm=L):
    m = u.shape[0]
    tn = IN_NSRC * IN_SRC
    n_gate, n_m, n_a = 3 * D_MODEL // IN_SRC, 4 * ML_DIM // IN_SRC, 3 * CONV_DIM // IN_SRC

    def src_row(t):
        gate = W_D_END + IN_SRC * t
        mls = W_A_END + IN_SRC * (t - n_gate)
        cnv = IN_SRC * (t - n_gate - n_m)
        att = W_IF_END + DA_DIM + IN_SRC * (t - n_gate - n_m - n_a)
        row = jnp.where(t < n_gate, gate,
                        jnp.where(t < n_gate + n_m, mls, jnp.where(t < n_gate + n_m + n_a, cnv, att)))
        return pl.multiple_of(layer * W_IN + row, 8)

    def src_spec(k):
        return pl.BlockSpec((pl.Element(IN_SRC), pl.Element(D_MODEL)),
                            lambda i, j: (src_row(IN_NSRC * j + k), 0))

    return pl.pallas_call(
        _inproj_kernel,
        grid=(m // tm, Z_N // tn),
        in_specs=[pl.BlockSpec((tm, D_MODEL), lambda i, j: (i, 0), pipeline_mode=pl.Buffered(1))]
        + [src_spec(k) for k in range(IN_NSRC)]
        + [_resident((pl.Element(IF_W), pl.Element(D_MODEL)), lambda i, j: (layer * W_IN + W_M_END, 0))],
        out_specs=[
            pl.BlockSpec((tm, tn), lambda i, j: (i, j)),
            pl.BlockSpec((tm, IF_W), lambda i, j: (i, 0)),
        ],
        out_shape=[jax.ShapeDtypeStruct((m, Z_N), BF16),
                   jax.ShapeDtypeStruct((m, IF_W), F32)],
        compiler_params=_params(("parallel", "arbitrary")),
        name="in_proj",
    )(u, *([w_in_t] * (IN_NSRC + 1)))


ML_NCH = -(-L // ML_T)
ML_TAIL_VALID = ML_T - (L - (ML_NCH - 1) * ML_T)


def _mlstm_gates(if_ref, bias_ref, row_scr, col_scr):
    t = ML_T
    ninf = -jnp.inf
    bias = bias_ref[...]
    starts = [c * t for c in range(ML_NCH - 1)] + [L - t]
    gt = jnp.concatenate([(if_ref[r0:r0 + t, :] + bias).T[0:8, :] for r0 in starts], axis=0)
    shape = (8 * ML_NCH, t)
    lane = lax.broadcasted_iota(jnp.int32, shape, 1)
    row = lax.broadcasted_iota(jnp.int32, shape, 0)
    is_i = (row & 7) < ML_HEADS
    log_sig = jnp.minimum(gt, 0.0) - jnp.log1p(jnp.exp(-jnp.abs(gt)))
    lg = jnp.where(is_i, gt, log_sig)
    weightless = jnp.logical_and(row >= 8 * (ML_NCH - 1), lane < ML_TAIL_VALID)
    lg = jnp.where(weightless, jnp.where(is_i, ninf, 0.0), lg)
    cs = jnp.where(is_i, 0.0, lg)
    sh = 1
    while sh < t:
        cs = cs + jnp.where(lane >= sh, pltpu.roll(cs, sh, 1), 0.0)
        sh *= 2
    rowpack = jnp.where(is_i, lg, cs)
    row_scr[...] = rowpack
    pad = jnp.zeros((t - 8, t), F32)
    for c in range(ML_NCH):
        col_scr[c] = jnp.concatenate([rowpack[8 * c:8 * c + 8, :], pad], axis=0).T


def _mlstm_chunk(c, r0, first_valid, q_ref, k_ref, v_ref, o_ref, g_ref, out_ref, s_ref, m_ref, row_scr, col_scr):
    t = ML_T
    ninf = -jnp.inf
    rows = pl.ds(r0, t)
    row0 = c * 8 if isinstance(c, int) else pl.multiple_of(c * 8, 8)
    rowpack = row_scr[pl.ds(row0, 8), :]
    colpack = col_scr[c]

    tri = lax.broadcasted_iota(jnp.int32, (t, t), 0) >= lax.broadcasted_iota(jnp.int32, (t, t), 1)
    nh = ML_HEADS

    def heads(x, d):
        return jnp.stack([x[:, h * d:(h + 1) * d] for h in range(nh)], axis=0)

    q = heads(q_ref[rows, :], ML_DK)
    k = heads(k_ref[rows, :], ML_DK) * jnp.asarray(ML_DK ** -0.5, BF16)
    vext = jnp.concatenate([heads(v_ref[rows, :], ML_DV), jnp.ones((nh, t, ML_EXT - ML_DV), BF16)], axis=2)
    li_r = jnp.stack([rowpack[h:h + 1, :] for h in range(nh)], axis=0)
    b_r = jnp.stack([rowpack[nh + h:nh + h + 1, :] for h in range(nh)], axis=0)
    li_c = jnp.stack([colpack[:, h:h + 1] for h in range(nh)], axis=0)
    b_c = jnp.stack([colpack[:, nh + h:nh + h + 1] for h in range(nh)], axis=0)
    m_prev = m_ref[...][:, :, 0:1]

    dmat = jnp.where(tri[None], b_c - b_r + li_r, ninf)
    inter = b_c + m_prev
    m_t = jnp.maximum(jnp.max(dmat, axis=2, keepdims=True), inter)
    w = jnp.exp(dmat - m_t)
    g = jnp.exp(inter - m_t)
    s = _bdot(q, k, 2, 2) * w
    tot = g * _bdot(q, s_ref[...].astype(BF16), 2, 1) + _bdot(s.astype(BF16), vext, 2, 1)
    num = tot[:, :, :ML_DV]
    den = tot[:, :, ML_DV:ML_DV + 1]
    hh = num / jnp.maximum(jnp.abs(den), jnp.exp(-m_t))
    hn = hh * lax.rsqrt(jnp.mean(hh * hh, axis=2, keepdims=True) + EPS)
    for h in range(nh):
        sl = slice(h * ML_DV, (h + 1) * ML_DV)
        res = (hn[h] * g_ref[:, sl] * _sigmoid(o_ref[rows, sl].astype(F32))).astype(BF16)
        if first_valid:
            out_ref[pl.ds(r0 + first_valid, t - first_valid), sl] = res[first_valid:, :]
        else:
            out_ref[rows, sl] = res

    b_last = b_r[:, :, t - 1:t]
    ds_r = b_last - b_r + li_r
    ds_c = b_last - b_c + li_c
    m_new = jnp.maximum(b_last + m_prev, jnp.max(ds_r, axis=2, keepdims=True))
    wk = jnp.exp(ds_c - m_new)
    gs = jnp.exp(b_last + m_prev - m_new)
    vw = (vext.astype(F32) * wk).astype(BF16)
    s_ref[...] = gs * s_ref[...] + _bdot(k, vw, 1, 1)
    m_ref[...] = jnp.broadcast_to(m_new, (nh, 1, 128))


def _mlstm_kernel(q_ref, k_ref, v_ref, o_ref, if_ref, bias_ref, g_ref, out_ref, s_ref, m_ref, row_scr, col_scr):
    s_ref[...] = jnp.zeros_like(s_ref)
    m_ref[...] = jnp.zeros_like(m_ref)
    _mlstm_gates(if_ref, bias_ref, row_scr, col_scr)
    refs = (q_ref, k_ref, v_ref, o_ref, g_ref, out_ref, s_ref, m_ref, row_scr, col_scr)

    def body(c, carry):
        _mlstm_chunk(c, pl.multiple_of(c * ML_T, ML_T), 0, *refs)
        return carry

    lax.fori_loop(0, ML_NCH - 1, body, 0)
    _mlstm_chunk(ML_NCH - 1, L - ML_T, ML_TAIL_VALID, *refs)


def _mlstm(z, zif, bias, g, bsz):
    m = bsz * L
    qb = Z_MQ // ML_DIM

    def zspec(off):
        return pl.BlockSpec((L, ML_DIM), lambda b: (b, qb + off))

    return pl.pallas_call(
        _mlstm_kernel,
        grid=(bsz,),
        in_specs=[
            zspec(0), zspec(1), zspec(2), zspec(3),
            pl.BlockSpec((L, IF_W), lambda b: (b, 0)),
            pl.BlockSpec((1, IF_W), lambda b: (0, 0)),
            pl.BlockSpec((1, ML_DIM), lambda b: (0, 0)),
        ],
        out_specs=pl.BlockSpec((L, ML_DIM), lambda b: (b, 0)),
        out_shape=jax.ShapeDtypeStruct((m, ML_DIM), BF16),
        scratch_shapes=[pltpu.VMEM((ML_HEADS, ML_DK, ML_EXT), F32),
                        pltpu.VMEM((ML_HEADS, 1, 128), F32),
                        pltpu.VMEM((8 * ML_NCH, ML_T), F32),
                        pltpu.VMEM((ML_NCH, ML_T, 128), F32)],
        compiler_params=_params(("parallel",)),
        name="mlstm",
    )(z, z, z, z, zif, bias, g)


ATT_MB = 128
ATT_TW = SEQ + ATT_MB


def _attn_proj_kernel(u_ref, wq_ref, wv_ref, qt_ref, vt_ref):
    u_fr = u_ref[N_META:, :]
    u_meta = u_ref[0:ATT_MB, :]
    scale = DA_HD ** -0.5 * math.log2(math.e)
    for w_ref, o_ref, mul in ((wq_ref, qt_ref, scale), (wv_ref, vt_ref, 1.0)):
        wt = w_ref[...].astype(BF16)
        o_ref[:, 0:SEQ] = (_dot_nt(wt, u_fr) * mul).astype(BF16)
        o_ref[:, SEQ:ATT_TW] = (_dot_nt(wt, u_meta) * mul).astype(BF16)


def _attn_proj(u, w_in_t, layer, bsz):
    def wspec(off):
        return _resident((pl.Element(DA_DIM), pl.Element(D_MODEL)), lambda b: (layer * W_IN + W_IF_END + off, 0))

    out = pl.BlockSpec((None, DA_DIM, ATT_TW), lambda b: (b, 0, 0))
    shape = jax.ShapeDtypeStruct((bsz, DA_DIM, ATT_TW), BF16)
    return pl.pallas_call(
        _attn_proj_kernel,
        grid=(bsz,),
        in_specs=[pl.BlockSpec((L, D_MODEL), lambda b: (b, 0)), wspec(0), wspec(2 * DA_DIM)],
        out_specs=[out, out],
        out_shape=[shape, shape],
        compiler_params=_params(("parallel",)),
        name="attn_qv_proj",
    )(u, w_in_t, w_in_t)


def _attn_scores(qt, k_ref, s_ref, n_frames, frame0):
    cols = qt.shape[1]
    ninf = -jnp.inf
    first_map = lax.broadcasted_iota(jnp.int32, (DA_VD, cols), 0) < DA_HD
    k_meta = k_ref[0:ATT_MB, :]
    meta_ok = lax.broadcasted_iota(jnp.int32, (ATT_MB, 1), 0) < N_META
    if n_frames:
        k_fr = k_ref[N_META:N_META + n_frames, :]
        qf = frame0 + lax.broadcasted_iota(jnp.int32, (1, cols), 1)
        kend = (qf // CHUNK + 1) * CHUNK
        kf = frame0 + lax.broadcasted_iota(jnp.int32, (n_frames - frame0, 1), 0)
        diag_ok = kf < kend
    for mp in range(2):
        qm = jnp.where(first_map if mp == 0 else jnp.logical_not(first_map), qt, jnp.zeros_like(qt))
        s_ref[mp, 0:ATT_MB, 0:cols] = jnp.where(meta_ok, _dot(k_meta, qm), ninf)
        if n_frames:
            s_fr = _dot(k_fr, qm)
            if frame0:
                s_ref[mp, ATT_MB:ATT_MB + frame0, 0:cols] = s_fr[:frame0, :]
            s_ref[mp, ATT_MB + frame0:ATT_MB + n_frames, 0:cols] = jnp.where(diag_ok, s_fr[frame0:, :], ninf)


def _attn_output(s_ref, vt_ref, cols, n_frames, lam, lam_init, g):
    nk = ATT_MB + n_frames
    probs = []
    for mp in range(2):
        s = s_ref[mp, 0:nk, 0:cols]
        p = jnp.exp2(s - jnp.max(s, axis=0, keepdims=True))
        probs.append((p, jnp.sum(p, axis=0, keepdims=True)))
    a = (probs[0][0] * (1.0 / probs[0][1]) - probs[1][0] * (lam / probs[1][1])).astype(BF16)
    o = _dot(vt_ref[:, SEQ:ATT_TW], a[:ATT_MB, :])
    if n_frames:
        o = o + _dot(vt_ref[:, 0:n_frames], a[ATT_MB:, :])
    o = o * lax.rsqrt(jnp.mean(o * o, axis=0, keepdims=True) + EPS)
    return o * g * (1.0 - lam_init)


def _attn_kernel(lam_init, qt_ref, k_ref, vt_ref, lam_ref, g_ref, out_ref, s_scr):
    lf = lam_ref[...]
    lam = (jnp.exp(jnp.sum(lf[0:1] * lf[1:2], axis=1, keepdims=True))
           - jnp.exp(jnp.sum(lf[2:3] * lf[3:4], axis=1, keepdims=True)) + lam_init)
    g = g_ref[...]
    blocks = [(SEQ, ATT_MB, 0, 0)] + [(f0, ATT_QB, f0 + ATT_QB, f0) for f0 in range(0, SEQ, ATT_QB)]

    def scores(i):
        lane0, cols, n_frames, frame0 = blocks[i]
        _attn_scores(qt_ref[:, lane0:lane0 + cols], k_ref, s_scr.at[i % 2], n_frames, frame0)

    scores(0)
    for i, (lane0, cols, n_frames, frame0) in enumerate(blocks):
        if i + 1 < len(blocks):
            scores(i + 1)
        o = _attn_output(s_scr.at[i % 2], vt_ref, cols, n_frames, lam, lam_init, g)
        if n_frames:
            out_ref[N_META + frame0:N_META + frame0 + cols, :] = o.T.astype(BF16)
        else:
            out_ref[0:N_META, :] = o.T[0:N_META, :].astype(BF16)


def _attn(z, qt, vt, lam_p, g, layer, bsz):
    lam_init = 0.8 - 0.6 * math.exp(-0.3 * layer)
    m = bsz * L
    kb = Z_DK // DA_VD

    def tspec():
        return pl.BlockSpec((None, DA_VD, ATT_TW), lambda b, h: (b, h, 0))

    return pl.pallas_call(
        functools.partial(_attn_kernel, lam_init),
        grid=(bsz, DA_HEADS),
        in_specs=[
            tspec(),
            pl.BlockSpec((L, DA_VD), lambda b, h: (b, kb + h)),
            tspec(),
            pl.BlockSpec((4, DA_HD), lambda b, h: (0, 0)),
            pl.BlockSpec((DA_VD, 1), lambda b, h: (0, 0)),
        ],
        out_specs=pl.BlockSpec((L, DA_VD), lambda b, h: (b, h)),
        out_shape=jax.ShapeDtypeStruct((m, DA_DIM), BF16),
        scratch_shapes=[pltpu.VMEM((2, 2, ATT_TW, ATT_QB), F32)],
        compiler_params=_params(("parallel", "parallel")),
        name="diff_attn",
    )(qt, z, vt, lam_p, g)


MERGE_NC = 512


def _merge_kernel(tm, ax_ref, ab_ref, ac_ref, axh_ref, ach_ref, hm_ref, hd_ref,
                  ga_ref, gm_ref, gd_ref, h_ref, cw_ref, wa_ref, wm_ref, wd_ref, wo_ref, g_ref,
                  hn_ref, u_ref, mg_scr):
    pos = _pos_in_batch(pl.program_id(0) * tm, tm)
    cur = ac_ref[...].astype(F32) * ax_ref[...].astype(F32)
    prev = ach_ref[...].astype(F32) * axh_ref[...].astype(F32)
    a_act = (ab_ref[...].astype(F32) * _causal_conv3(prev, cur, cw_ref[...], pos)).astype(BF16)
    hm = hm_ref[...]
    hd = hd_ref[...]
    for c0 in range(0, D_MODEL, MERGE_NC):
        cs = slice(c0, c0 + MERGE_NC)
        merged = (_sigmoid(ga_ref[:, cs].astype(F32)) * _dot(a_act, wa_ref[:, cs])
                  + _sigmoid(gm_ref[:, cs].astype(F32)) * _dot(hm, wm_ref[:, cs])
                  + _sigmoid(gd_ref[:, cs].astype(F32)) * _dot(hd, wd_ref[:, cs]))
        mg_scr[:, cs] = merged.astype(BF16)
    hn = h_ref[...] + _dot(mg_scr[...], wo_ref[...])
    hn_ref[...] = hn
    u_ref[...] = _rms(hn, g_ref[...]).astype(BF16)


def _merge(z, hm, hd, h, layer, conv_a, w_br_a, w_br_m, w_br_d, w_out, g, tm=384):
    m = h.shape[0]
    axb = Z_AX // CONV_DIM
    hb = tm // HALO

    def halo(col):
        return pl.BlockSpec((HALO, CONV_DIM), lambda i: (jnp.maximum(i * hb - 1, 0), col))

    def weight(rows):
        return _resident((None, rows, D_MODEL), lambda i: (layer, 0, 0))

    return pl.pallas_call(
        functools.partial(_merge_kernel, tm),
        grid=(m // tm,),
        in_specs=[
            pl.BlockSpec((tm, CONV_DIM), lambda i: (i, axb)),
            pl.BlockSpec((tm, CONV_DIM), lambda i: (i, axb + 1)),
            pl.BlockSpec((tm, CONV_DIM), lambda i: (i, axb + 2)),
            halo(axb), halo(axb + 2),
            pl.BlockSpec((tm, ML_DIM), lambda i: (i, 0)),
            pl.BlockSpec((tm, DA_DIM), lambda i: (i, 0)),
            pl.BlockSpec((tm, D_MODEL), lambda i: (i, 0)),
            pl.BlockSpec((tm, D_MODEL), lambda i: (i, 1)),
            pl.BlockSpec((tm, D_MODEL), lambda i: (i, 2)),
            pl.BlockSpec((tm, D_MODEL), lambda i: (i, 0)),
            _resident((3, CONV_DIM), lambda i: (0, 0)),
            weight(CONV_DIM), weight(ML_DIM), weight(DA_DIM), weight(D_MODEL),
            _resident((1, D_MODEL), lambda i: (0, 0)),
        ],
        out_specs=[
            pl.BlockSpec((tm, D_MODEL), lambda i: (i, 0)),
            pl.BlockSpec((tm, D_MODEL), lambda i: (i, 0)),
        ],
        out_shape=[jax.ShapeDtypeStruct((m, D_MODEL), F32),
                   jax.ShapeDtypeStruct((m, D_MODEL), BF16)],
        scratch_shapes=[pltpu.VMEM((tm, D_MODEL), BF16)],
        compiler_params=_params(("parallel",)),
        name="merge_out_proj",
    )(z, z, z, z, z, hm, hd, z, z, z, h, conv_a, w_br_a, w_br_m, w_br_d, w_out, g)


FFN_ROWS = 2


def _row_blocks(tm, n):
    units = tm // HALO
    edges = [HALO * (units * k // n) for k in range(n + 1)]
    return list(zip(edges[:-1], edges[1:]))


def _ffn_up_kernel(tm, u_ref, uh_ref, wa_ref, wb_ref, cw_ref, cb_ref, out_ref, w_scr):
    @pl.when(pl.program_id(1) == 0)
    def _():
        w_scr[0] = wa_ref[...].astype(BF16)
        w_scr[1] = wb_ref[...].astype(BF16)

    r_tile = pl.program_id(1) * tm
    a_prev = _dot(uh_ref[...], w_scr[0])
    for r0, r1 in _row_blocks(tm, FFN_ROWS):
        u = u_ref[r0:r1, :]
        a = _dot(u, w_scr[0])
        conv = _causal_conv3(a_prev, a, cw_ref[...], _pos_in_batch(r_tile + r0, r1 - r0)) + cb_ref[...]
        gelu = 0.5 * conv * (1.0 + lax.erf(conv * (2.0 ** -0.5)))
        out_ref[r0:r1, :] = (gelu * _dot(u, w_scr[1])).astype(BF16)
        a_prev = a[r1 - r0 - HALO:, :]


def _ffn_up(u, w_up, layer, conv_w, conv_b, tm=L, tn=512):
    m = u.shape[0]
    nb = D_FF // tn
    hb = tm // HALO
    return pl.pallas_call(
        functools.partial(_ffn_up_kernel, tm),
        grid=(nb, m // tm),
        in_specs=[
            pl.BlockSpec((tm, D_MODEL), lambda j, i: (i, 0)),
            pl.BlockSpec((HALO, D_MODEL), lambda j, i: (jnp.maximum(i * hb - 1, 0), 0)),
            pl.BlockSpec((None, D_MODEL, tn), lambda j, i: (layer, 0, j)),
            pl.BlockSpec((None, D_MODEL, tn), lambda j, i: (layer, 0, nb + j)),
            pl.BlockSpec((3, tn), lambda j, i: (0, j)),
            pl.BlockSpec((1, tn), lambda j, i: (0, j)),
        ],
        out_specs=pl.BlockSpec((tm, tn), lambda j, i: (i, j)),
        out_shape=jax.ShapeDtypeStruct((m, D_FF), BF16),
        scratch_shapes=[pltpu.VMEM((2, D_MODEL, tn), BF16)],
        compiler_params=_params(("parallel", "arbitrary")),
        name="ffn_up",
    )(u, u, w_up, w_up, conv_w, conv_b)


def _ffn_down_kernel(final, act_ref, w_ref, h_ref, g_ref, *out_refs):
    hn = h_ref[...] + _dot(act_ref[...], w_ref[...])
    y = _rms(hn, g_ref[...])
    if final:
        out_refs[0][...] = y
    else:
        out_refs[0][...] = hn
        out_refs[1][...] = y.astype(BF16)


def _ffn_down(act, w_down, layer, h, g, final):
    m = h.shape[0]
    if final:
        tm = 256
        m = m // L * SEQ
        per_batch = SEQ // tm

        def stream_row(i):
            return pl.multiple_of(i * tm + N_META * (i // per_batch + 1), N_META)

        act_spec = pl.BlockSpec((pl.Element(tm), pl.Element(D_FF)), lambda i: (stream_row(i), 0))
        h_spec = pl.BlockSpec((pl.Element(tm), pl.Element(D_MODEL)), lambda i: (stream_row(i), 0))
        row = pl.BlockSpec((tm, D_MODEL), lambda i: (i, 0))
        out_specs = [row]
        out_shape = [jax.ShapeDtypeStruct((m, D_MODEL), F32)]
    else:
        tm = 384
        row = pl.BlockSpec((tm, D_MODEL), lambda i: (i, 0))
        act_spec = pl.BlockSpec((tm, D_FF), lambda i: (i, 0))
        h_spec = row
        out_specs = [row, row]
        out_shape = [jax.ShapeDtypeStruct((m, D_MODEL), F32),
                     jax.ShapeDtypeStruct((m, D_MODEL), BF16)]
    return pl.pallas_call(
        functools.partial(_ffn_down_kernel, final),
        grid=(m // tm,),
        in_specs=[
            act_spec,
            _resident((None, D_FF, D_MODEL), lambda i: (layer, 0, 0)),
            h_spec,
            _resident((1, D_MODEL), lambda i: (0, 0)),
        ],
        out_specs=out_specs,
        out_shape=out_shape,
        compiler_params=_params(("parallel",)),
        name="ffn_down_final" if final else "ffn_down",
    )(act, w_down, h, g)


def kernel(x, meta, norm_mix, w_in, conv_a, b_if, ml_norm, da_lambda, da_norm, w_br_a, w_br_m, w_br_d,
           w_out, norm_ffn, w_up, conv_ffn, conv_ffn_b, w_down, norm_f):
    bsz, seq, d = x.shape
    assert (seq, d) == (SEQ, D_MODEL)
    depth = w_in.shape[0]
    w_in_t = jnp.swapaxes(w_in, 1, 2).reshape(depth * W_IN, D_MODEL)
    wa, wm, wd, wo, wdn = (w.astype(BF16) for w in (w_br_a, w_br_m, w_br_d, w_out, w_down))
    h, u = _embed(x.reshape(bsz * SEQ, D_MODEL), meta, norm_mix[0][None], bsz)
    for i in range(depth):
        z, zif = _inproj(u, w_in_t, i)
        bias = jnp.pad(b_if[i].reshape(1, 2 * ML_HEADS), ((0, 0), (0, IF_W - 2 * ML_HEADS)))
        hm = _mlstm(z, zif, bias, ml_norm[i][None], bsz)
        qt, vt = _attn_proj(u, w_in_t, i, bsz)
        hd = _attn(z, qt, vt, da_lambda[i], da_norm[i][:, None], i, bsz)
        h, u = _merge(z, hm, hd, h, i, conv_a[i], wa, wm, wd, wo, norm_ffn[i][None])
        act = _ffn_up(u, w_up, i, conv_ffn[i], conv_ffn_b[i][None])
        final = i == depth - 1
        g_next = norm_f if final else norm_mix[i + 1]
        outs = _ffn_down(act, wdn, i, h, g_next[None], final)
        if final:
            y = outs[0]
        else:
            h, u = outs
    return y.reshape(bsz, SEQ, D_MODEL)
```

```python
import functools
import math

import jax
import jax.numpy as jnp
from jax import lax
from jax.experimental import pallas as pl
from jax.experimental.pallas import tpu as pltpu

F32 = jnp.float32
BF16 = jnp.bfloat16

D_MODEL = 2048
SEQ = 2048
N_META = 16
L = N_META + SEQ
CHUNK = 64
EPS = 1e-6

CONV_DIM = 512
ML_HEADS = 4
ML_DK = 256
ML_DV = 256
ML_DIM = ML_HEADS * ML_DV
DA_HEADS = 4
DA_HD = 64
DA_VD = 2 * DA_HD
DA_DIM = DA_HEADS * DA_VD
D_FF = 5632

Z_GATE = 0
Z_MQ = 3 * D_MODEL
Z_AX = Z_MQ + 4 * ML_DIM
Z_DK = Z_AX + 3 * CONV_DIM
Z_N = Z_DK + DA_DIM
IF_W = 128

ML_T = 128
ML_EXT = ML_DV + 128
HALO = 16
ATT_QB = 256
EMB_T = L // 3
VMEM_LIMIT = 56 * 1024 * 1024


def _dot(a, b):
    return jnp.dot(a, b, preferred_element_type=F32)


def _dot_nt(a, b):
    return lax.dot_general(a, b, (((1,), (1,)), ((), ())), preferred_element_type=F32)


def _dot_tn(a, b):
    return lax.dot_general(a, b, (((0,), (0,)), ((), ())), preferred_element_type=F32)


def _bdot(a, b, ca, cb):
    return lax.dot_general(a, b, (((ca,), (cb,)), ((0,), (0,))), preferred_element_type=F32)


def _rms(x, g):
    return x * lax.rsqrt(jnp.mean(x * x, axis=-1, keepdims=True) + EPS) * g


def _sigmoid(x):
    return 1.0 / (1.0 + jnp.exp(-x))


def _pos_in_batch(r0, tm):
    pos = r0 % L + lax.broadcasted_iota(jnp.int32, (tm, 1), 0)
    return jnp.where(pos >= L, pos - L, pos)


def _causal_conv3(prev, cur, w, pos):
    cc = jnp.concatenate([prev, cur], axis=0)
    x1 = jnp.where(pos >= 1, pltpu.roll(cc, 1, 0)[HALO:], 0.0)
    x2 = jnp.where(pos >= 2, pltpu.roll(cc, 2, 0)[HALO:], 0.0)
    return w[0:1] * x2 + w[1:2] * x1 + w[2:3] * cur


def _params(sem):
    return pltpu.CompilerParams(dimension_semantics=sem, vmem_limit_bytes=VMEM_LIMIT)


def _resident(shape, index_map):
    return pl.BlockSpec(shape, index_map, pipeline_mode=pl.Buffered(1))


def _embed_kernel(x_ref, meta_ref, g_ref, h_ref, u_ref):
    j = pl.program_id(1)

    @pl.when(j == 0)
    def _():
        rows = jnp.concatenate([meta_ref[...], x_ref[0:EMB_T - N_META, :]], axis=0)
        h_ref[...] = rows
        u_ref[...] = _rms(rows, g_ref[...]).astype(BF16)

    @pl.when(j > 0)
    def _():
        rows = x_ref[...]
        h_ref[...] = rows
        u_ref[...] = _rms(rows, g_ref[...]).astype(BF16)


def _embed(x2d, meta, g, bsz):
    nb = L // EMB_T
    m = bsz * L
    return pl.pallas_call(
        _embed_kernel,
        grid=(bsz, nb),
        in_specs=[
            pl.BlockSpec((pl.Element(EMB_T), pl.Element(D_MODEL)),
                         lambda b, j: (pl.multiple_of(b * SEQ + jnp.maximum(j * EMB_T - N_META, 0), 16), 0)),
            pl.BlockSpec((N_META, D_MODEL), lambda b, j: (0, 0)),
            pl.BlockSpec((1, D_MODEL), lambda b, j: (0, 0)),
        ],
        out_specs=[
            pl.BlockSpec((EMB_T, D_MODEL), lambda b, j: (b * nb + j, 0)),
            pl.BlockSpec((EMB_T, D_MODEL), lambda b, j: (b * nb + j, 0)),
        ],
        out_shape=[jax.ShapeDtypeStruct((m, D_MODEL), F32),
                   jax.ShapeDtypeStruct((m, D_MODEL), BF16)],
        compiler_params=_params(("parallel", "arbitrary")),
        name="embed_norm",
    )(x2d, meta, g)


W_A_END = 3 * CONV_DIM
W_M_END = W_A_END + 4 * ML_DIM
W_IF_END = W_M_END + 2 * ML_HEADS
W_D_END = W_IF_END + 3 * DA_DIM
W_IN = W_D_END + 3 * D_MODEL


IN_SRC = 512
IN_NSRC = 2


def _inproj_kernel(u_ref, *refs):
    wt_refs, z_ref = refs[:IN_NSRC], refs[IN_NSRC]
    u = u_ref[...]
    for k, wt_ref in enumerate(wt_refs):
        z_ref[:, k * IN_SRC:(k + 1) * IN_SRC] = _dot_nt(u, wt_ref[...].astype(BF16)).astype(BF16)


def _inproj(u, w_in_t, layer, tm=L):
    m = u.shape[0]
    tn = IN_NSRC * IN_SRC
    n_gate, n_m, n_a = 3 * D_MODEL // IN_SRC, 4 * ML_DIM // IN_SRC, 3 * CONV_DIM // IN_SRC

    def src_row(t):
        gate = W_D_END + IN_SRC * t
        mls = W_A_END + IN_SRC * (t - n_gate)
        cnv = IN_SRC * (t - n_gate - n_m)
        att = W_IF_END + DA_DIM + IN_SRC * (t - n_gate - n_m - n_a)
        row = jnp.where(t < n_gate, gate,
                        jnp.where(t < n_gate + n_m, mls, jnp.where(t < n_gate + n_m + n_a, cnv, att)))
        return pl.multiple_of(layer * W_IN + row, 8)

    def src_spec(k):
        return pl.BlockSpec((pl.Element(IN_SRC), pl.Element(D_MODEL)),
                            lambda i, j: (src_row(IN_NSRC * j + k), 0))

    return pl.pallas_call(
        _inproj_kernel,
        grid=(m // tm, Z_N // tn),
        in_specs=[pl.BlockSpec((tm, D_MODEL), lambda i, j: (i, 0))] + [src_spec(k) for k in range(IN_NSRC)],
        out_specs=pl.BlockSpec((tm, tn), lambda i, j: (i, j)),
        out_shape=jax.ShapeDtypeStruct((m, Z_N), BF16),
        compiler_params=_params(("parallel", "arbitrary")),
        name="in_proj",
    )(u, *([w_in_t] * IN_NSRC))


ML_NCH = -(-L // ML_T)
ML_TAIL_VALID = ML_T - (L - (ML_NCH - 1) * ML_T)


def _mlstm_gates(if_ref, bias_ref, row_scr, col_scr):
    t = ML_T
    ninf = -jnp.inf
    bias = bias_ref[...]
    starts = [c * t for c in range(ML_NCH - 1)] + [L - t]
    gt = jnp.concatenate([(if_ref[r0:r0 + t, :] + bias).T[0:8, :] for r0 in starts], axis=0)
    shape = (8 * ML_NCH, t)
    lane = lax.broadcasted_iota(jnp.int32, shape, 1)
    row = lax.broadcasted_iota(jnp.int32, shape, 0)
    is_i = (row & 7) < ML_HEADS
    log_sig = jnp.minimum(gt, 0.0) - jnp.log1p(jnp.exp(-jnp.abs(gt)))
    lg = jnp.where(is_i, gt, log_sig)
    weightless = jnp.logical_and(row >= 8 * (ML_NCH - 1), lane < ML_TAIL_VALID)
    lg = jnp.where(weightless, jnp.where(is_i, ninf, 0.0), lg)
    cs = jnp.where(is_i, 0.0, lg)
    sh = 1
    while sh < t:
        cs = cs + jnp.where(lane >= sh, pltpu.roll(cs, sh, 1), 0.0)
        sh *= 2
    rowpack = jnp.where(is_i, lg, cs)
    row_scr[...] = rowpack
    pad = jnp.zeros((t - 8, t), F32)
    for c in range(ML_NCH):
        col_scr[c] = jnp.concatenate([rowpack[8 * c:8 * c + 8, :], pad], axis=0).T


def _mlstm_chunk(c, r0, first_valid, q_ref, k_ref, v_ref, o_ref, g_ref, out_ref, s_ref, m_ref, row_scr, col_scr):
    t = ML_T
    ninf = -jnp.inf
    rows = pl.ds(r0, t)
    row0 = c * 8 if isinstance(c, int) else pl.multiple_of(c * 8, 8)
    rowpack = row_scr[pl.ds(row0, 8), :]
    colpack = col_scr[c]

    tri = lax.broadcasted_iota(jnp.int32, (t, t), 0) >= lax.broadcasted_iota(jnp.int32, (t, t), 1)
    nh = ML_HEADS

    def heads(x, d):
        return jnp.stack([x[:, h * d:(h + 1) * d] for h in range(nh)], axis=0)

    q = heads(q_ref[rows, :], ML_DK)
    k = heads(k_ref[rows, :], ML_DK) * jnp.asarray(ML_DK ** -0.5, BF16)
    vext = jnp.concatenate([heads(v_ref[rows, :], ML_DV), jnp.ones((nh, t, ML_EXT - ML_DV), BF16)], axis=2)
    li_r = jnp.stack([rowpack[h:h + 1, :] for h in range(nh)], axis=0)
    b_r = jnp.stack([rowpack[nh + h:nh + h + 1, :] for h in range(nh)], axis=0)
    li_c = jnp.stack([colpack[:, h:h + 1] for h in range(nh)], axis=0)
    b_c = jnp.stack([colpack[:, nh + h:nh + h + 1] for h in range(nh)], axis=0)
    m_prev = m_ref[...][:, :, 0:1]

    dmat = jnp.where(tri[None], b_c - b_r + li_r, ninf)
    inter = b_c + m_prev
    m_t = jnp.maximum(jnp.max(dmat, axis=2, keepdims=True), inter)
    w = jnp.exp(dmat - m_t)
    g = jnp.exp(inter - m_t)
    s = _bdot(q, k, 2, 2) * w
    tot = g * _bdot(q, s_ref[...].astype(BF16), 2, 1) + _bdot(s.astype(BF16), vext, 2, 1)
    num = tot[:, :, :ML_DV]
    den = tot[:, :, ML_DV:ML_DV + 1]
    hh = num / jnp.maximum(jnp.abs(den), jnp.exp(-m_t))
    hn = hh * lax.rsqrt(jnp.mean(hh * hh, axis=2, keepdims=True) + EPS)
    for h in range(nh):
        sl = slice(h * ML_DV, (h + 1) * ML_DV)
        res = (hn[h] * g_ref[:, sl] * _sigmoid(o_ref[rows, sl].astype(F32))).astype(BF16)
        if first_valid:
            out_ref[pl.ds(r0 + first_valid, t - first_valid), sl] = res[first_valid:, :]
        else:
            out_ref[rows, sl] = res

    b_last = b_r[:, :, t - 1:t]
    ds_r = b_last - b_r + li_r
    ds_c = b_last - b_c + li_c
    m_new = jnp.maximum(b_last + m_prev, jnp.max(ds_r, axis=2, keepdims=True))
    wk = jnp.exp(ds_c - m_new)
    gs = jnp.exp(b_last + m_prev - m_new)
    vw = (vext.astype(F32) * wk).astype(BF16)
    s_ref[...] = gs * s_ref[...] + _bdot(k, vw, 1, 1)
    m_ref[...] = jnp.broadcast_to(m_new, (nh, 1, 128))


def _mlstm_kernel(q_ref, k_ref, v_ref, o_ref, if_ref, bias_ref, g_ref, out_ref, s_ref, m_ref, row_scr, col_scr):
    s_ref[...] = jnp.zeros_like(s_ref)
    m_ref[...] = jnp.zeros_like(m_ref)
    _mlstm_gates(if_ref, bias_ref, row_scr, col_scr)
    refs = (q_ref, k_ref, v_ref, o_ref, g_ref, out_ref, s_ref, m_ref, row_scr, col_scr)

    def body(i, carry):
        for c in (2 * i, 2 * i + 1):
            _mlstm_chunk(c, pl.multiple_of(c * ML_T, ML_T), 0, *refs)
        return carry

    lax.fori_loop(0, (ML_NCH - 1) // 2, body, 0)
    _mlstm_chunk(ML_NCH - 1, L - ML_T, ML_TAIL_VALID, *refs)


def _mlstm(z, zif, bias, g, bsz):
    m = bsz * L
    qb = Z_MQ // ML_DIM

    def zspec(off):
        return pl.BlockSpec((L, ML_DIM), lambda b: (b, qb + off))

    return pl.pallas_call(
        _mlstm_kernel,
        grid=(bsz,),
        in_specs=[
            zspec(0), zspec(1), zspec(2), zspec(3),
            pl.BlockSpec((L, IF_W), lambda b: (b, 0)),
            pl.BlockSpec((1, IF_W), lambda b: (0, 0)),
            pl.BlockSpec((1, ML_DIM), lambda b: (0, 0)),
        ],
        out_specs=pl.BlockSpec((L, ML_DIM), lambda b: (b, 0)),
        out_shape=jax.ShapeDtypeStruct((m, ML_DIM), BF16),
        scratch_shapes=[pltpu.VMEM((ML_HEADS, ML_DK, ML_EXT), F32),
                        pltpu.VMEM((ML_HEADS, 1, 128), F32),
                        pltpu.VMEM((8 * ML_NCH, ML_T), F32),
                        pltpu.VMEM((ML_NCH, ML_T, 128), F32)],
        compiler_params=_params(("parallel",)),
        name="mlstm",
    )(z, z, z, z, zif, bias, g)


ATT_MB = 128
ATT_TW = SEQ + ATT_MB


def _attn_proj_kernel(u_ref, wq_ref, wv_ref, wif_ref, qt_ref, vt_ref, zif_ref):
    u_fr = u_ref[N_META:, :]
    u_meta = u_ref[0:ATT_MB, :]
    scale = DA_HD ** -0.5 * math.log2(math.e)
    for w_ref, o_ref, mul in ((wq_ref, qt_ref, scale), (wv_ref, vt_ref, 1.0)):
        wt = w_ref[...].astype(BF16)
        o_ref[:, 0:SEQ] = (_dot_nt(wt, u_fr) * mul).astype(BF16)
        o_ref[:, SEQ:ATT_TW] = (_dot_nt(wt, u_meta) * mul).astype(BF16)
    zif_ref[...] = _dot_nt(u_ref[...], wif_ref[...].astype(BF16))


def _attn_proj(u, w_in_t, layer, bsz):
    def wspec(rows, row0):
        return _resident((pl.Element(rows), pl.Element(D_MODEL)), lambda b: (layer * W_IN + row0, 0))

    out = pl.BlockSpec((None, DA_DIM, ATT_TW), lambda b: (b, 0, 0))
    shape = jax.ShapeDtypeStruct((bsz, DA_DIM, ATT_TW), BF16)
    return pl.pallas_call(
        _attn_proj_kernel,
        grid=(bsz,),
        in_specs=[pl.BlockSpec((L, D_MODEL), lambda b: (b, 0)),
                  wspec(DA_DIM, W_IF_END), wspec(DA_DIM, W_IF_END + 2 * DA_DIM), wspec(IF_W, W_M_END)],
        out_specs=[out, out, pl.BlockSpec((L, IF_W), lambda b: (b, 0))],
        out_shape=[shape, shape, jax.ShapeDtypeStruct((bsz * L, IF_W), F32)],
        compiler_params=_params(("parallel",)),
        name="attn_qv_proj",
    )(u, w_in_t, w_in_t, w_in_t)


def _attn_scores(qt, k_ref, s_ref, n_frames, frame0):
    cols = qt.shape[1]
    ninf = -jnp.inf
    first_map = lax.broadcasted_iota(jnp.int32, (DA_VD, cols), 0) < DA_HD
    k_meta = k_ref[0:ATT_MB, :]
    meta_ok = lax.broadcasted_iota(jnp.int32, (ATT_MB, 1), 0) < N_META
    if n_frames:
        k_fr = k_ref[N_META:N_META + n_frames, :]
        qf = frame0 + lax.broadcasted_iota(jnp.int32, (1, cols), 1)
        kend = (qf // CHUNK + 1) * CHUNK
        kf = frame0 + lax.broadcasted_iota(jnp.int32, (n_frames - frame0, 1), 0)
        diag_ok = kf < kend
    for mp in range(2):
        qm = jnp.where(first_map if mp == 0 else jnp.logical_not(first_map), qt, jnp.zeros_like(qt))
        s_ref[mp, 0:ATT_MB, 0:cols] = jnp.where(meta_ok, _dot(k_meta, qm), ninf)
        if n_frames:
            s_fr = _dot(k_fr, qm)
            if frame0:
                s_ref[mp, ATT_MB:ATT_MB + frame0, 0:cols] = s_fr[:frame0, :]
            s_ref[mp, ATT_MB + frame0:ATT_MB + n_frames, 0:cols] = jnp.where(diag_ok, s_fr[frame0:, :], ninf)


def _attn_output(s_ref, vt_ref, cols, n_frames, lam, lam_init, g):
    nk = ATT_MB + n_frames
    probs = []
    for mp in range(2):
        s = s_ref[mp, 0:nk, 0:cols]
        p = jnp.exp2(s - jnp.max(s, axis=0, keepdims=True))
        probs.append((p, jnp.sum(p, axis=0, keepdims=True)))
    a = (probs[0][0] * (1.0 / probs[0][1]) - probs[1][0] * (lam / probs[1][1])).astype(BF16)
    o = _dot(vt_ref[:, SEQ:ATT_TW], a[:ATT_MB, :])
    if n_frames:
        o = o + _dot(vt_ref[:, 0:n_frames], a[ATT_MB:, :])
    o = o * lax.rsqrt(jnp.mean(o * o, axis=0, keepdims=True) + EPS)
    return o * g * (1.0 - lam_init)


def _attn_kernel(lam_init, qt_ref, k_ref, vt_ref, lam_ref, g_ref, out_ref, s_scr):
    lf = lam_ref[...]
    lam = (jnp.exp(jnp.sum(lf[0:1] * lf[1:2], axis=1, keepdims=True))
           - jnp.exp(jnp.sum(lf[2:3] * lf[3:4], axis=1, keepdims=True)) + lam_init)
    g = g_ref[...]
    blocks = [(SEQ, ATT_MB, 0, 0)] + [(f0, ATT_QB, f0 + ATT_QB, f0) for f0 in range(0, SEQ, ATT_QB)]

    def scores(i):
        lane0, cols, n_frames, frame0 = blocks[i]
        _attn_scores(qt_ref[:, lane0:lane0 + cols], k_ref, s_scr.at[i % 2], n_frames, frame0)

    scores(0)
    for i, (lane0, cols, n_frames, frame0) in enumerate(blocks):
        if i + 1 < len(blocks):
            scores(i + 1)
        o = _attn_output(s_scr.at[i % 2], vt_ref, cols, n_frames, lam, lam_init, g)
        if n_frames:
            out_ref[N_META + frame0:N_META + frame0 + cols, :] = o.T.astype(BF16)
        else:
            out_ref[0:N_META, :] = o.T[0:N_META, :].astype(BF16)


def _attn(z, qt, vt, lam_p, g, layer, bsz):
    lam_init = 0.8 - 0.6 * math.exp(-0.3 * layer)
    m = bsz * L
    kb = Z_DK // DA_VD

    def tspec():
        return pl.BlockSpec((None, DA_VD, ATT_TW), lambda b, h: (b, h, 0))

    return pl.pallas_call(
        functools.partial(_attn_kernel, lam_init),
        grid=(bsz, DA_HEADS),
        in_specs=[
            tspec(),
            pl.BlockSpec((L, DA_VD), lambda b, h: (b, kb + h)),
            tspec(),
            pl.BlockSpec((4, DA_HD), lambda b, h: (0, 0)),
            pl.BlockSpec((DA_VD, 1), lambda b, h: (0, 0)),
        ],
        out_specs=pl.BlockSpec((L, DA_VD), lambda b, h: (b, h)),
        out_shape=jax.ShapeDtypeStruct((m, DA_DIM), BF16),
        scratch_shapes=[pltpu.VMEM((2, 2, ATT_TW, ATT_QB), F32)],
        compiler_params=_params(("parallel", "parallel")),
        name="diff_attn",
    )(qt, z, vt, lam_p, g)


MERGE_NC = 512


def _merge_kernel(tm, ax_ref, ab_ref, ac_ref, axh_ref, ach_ref, hm_ref, hd_ref,
                  ga_ref, gm_ref, gd_ref, h_ref, cw_ref, wa_ref, wm_ref, wd_ref, wo_ref, g_ref,
                  hn_ref, u_ref, mg_scr):
    pos = _pos_in_batch(pl.program_id(0) * tm, tm)
    cur = ac_ref[...].astype(F32) * ax_ref[...].astype(F32)
    prev = ach_ref[...].astype(F32) * axh_ref[...].astype(F32)
    a_act = (ab_ref[...].astype(F32) * _causal_conv3(prev, cur, cw_ref[...], pos)).astype(BF16)
    hm = hm_ref[...]
    hd = hd_ref[...]
    for c0 in range(0, D_MODEL, MERGE_NC):
        cs = slice(c0, c0 + MERGE_NC)
        merged = (_sigmoid(ga_ref[:, cs].astype(F32)) * _dot(a_act, wa_ref[:, cs])
                  + _sigmoid(gm_ref[:, cs].astype(F32)) * _dot(hm, wm_ref[:, cs])
                  + _sigmoid(gd_ref[:, cs].astype(F32)) * _dot(hd, wd_ref[:, cs]))
        mg_scr[:, cs] = merged.astype(BF16)
    hn = h_ref[...] + _dot(mg_scr[...], wo_ref[...])
    hn_ref[...] = hn
    u_ref[...] = _rms(hn, g_ref[...]).astype(BF16)


def _merge(z, hm, hd, h, layer, conv_a, w_br_a, w_br_m, w_br_d, w_out, g, tm=384):
    m = h.shape[0]
    axb = Z_AX // CONV_DIM
    hb = tm // HALO

    def halo(col):
        return pl.BlockSpec((HALO, CONV_DIM), lambda i: (jnp.maximum(i * hb - 1, 0), col))

    def weight(rows):
        return _resident((None, rows, D_MODEL), lambda i: (layer, 0, 0))

    return pl.pallas_call(
        functools.partial(_merge_kernel, tm),
        grid=(m // tm,),
        in_specs=[
            pl.BlockSpec((tm, CONV_DIM), lambda i: (i, axb)),
            pl.BlockSpec((tm, CONV_DIM), lambda i: (i, axb + 1)),
            pl.BlockSpec((tm, CONV_DIM), lambda i: (i, axb + 2)),
            halo(axb), halo(axb + 2),
            pl.BlockSpec((tm, ML_DIM), lambda i: (i, 0)),
            pl.BlockSpec((tm, DA_DIM), lambda i: (i, 0)),
            pl.BlockSpec((tm, D_MODEL), lambda i: (i, 0)),
            pl.BlockSpec((tm, D_MODEL), lambda i: (i, 1)),
            pl.BlockSpec((tm, D_MODEL), lambda i: (i, 2)),
            pl.BlockSpec((tm, D_MODEL), lambda i: (i, 0)),
            _resident((3, CONV_DIM), lambda i: (0, 0)),
            weight(CONV_DIM), weight(ML_DIM), weight(DA_DIM), weight(D_MODEL),
            _resident((1, D_MODEL), lambda i: (0, 0)),
        ],
        out_specs=[
            pl.BlockSpec((tm, D_MODEL), lambda i: (i, 0)),
            pl.BlockSpec((tm, D_MODEL), lambda i: (i, 0)),
        ],
        out_shape=[jax.ShapeDtypeStruct((m, D_MODEL), F32),
                   jax.ShapeDtypeStruct((m, D_MODEL), BF16)],
        scratch_shapes=[pltpu.VMEM((tm, D_MODEL), BF16)],
        compiler_params=_params(("parallel",)),
        name="merge_out_proj",
    )(z, z, z, z, z, hm, hd, z, z, z, h, conv_a, w_br_a, w_br_m, w_br_d, w_out, g)


FFN_ROWS = 2


def _row_blocks(tm, n):
    units = tm // HALO
    edges = [HALO * (units * k // n) for k in range(n + 1)]
    return list(zip(edges[:-1], edges[1:]))


def _ffn_up_kernel(tm, u_ref, uh_ref, wa_ref, wb_ref, cw_ref, cb_ref, out_ref, w_scr):
    @pl.when(pl.program_id(1) == 0)
    def _():
        w_scr[0] = wa_ref[...].astype(BF16)
        w_scr[1] = wb_ref[...].astype(BF16)

    r_tile = pl.program_id(1) * tm
    a_prev = _dot(uh_ref[...], w_scr[0])
    for r0, r1 in _row_blocks(tm, FFN_ROWS):
        u = u_ref[r0:r1, :]
        a = _dot(u, w_scr[0])
        conv = _causal_conv3(a_prev, a, cw_ref[...], _pos_in_batch(r_tile + r0, r1 - r0)) + cb_ref[...]
        gelu = 0.5 * conv * (1.0 + lax.erf(conv * (2.0 ** -0.5)))
        out_ref[r0:r1, :] = (gelu * _dot(u, w_scr[1])).astype(BF16)
        a_prev = a[r1 - r0 - HALO:, :]


def _ffn_up(u, w_up, layer, conv_w, conv_b, tm=L, tn=512):
    m = u.shape[0]
    nb = D_FF // tn
    hb = tm // HALO
    return pl.pallas_call(
        functools.partial(_ffn_up_kernel, tm),
        grid=(nb, m // tm),
        in_specs=[
            pl.BlockSpec((tm, D_MODEL), lambda j, i: (i, 0)),
            pl.BlockSpec((HALO, D_MODEL), lambda j, i: (jnp.maximum(i * hb - 1, 0), 0)),
            pl.BlockSpec((None, D_MODEL, tn), lambda j, i: (layer, 0, j)),
            pl.BlockSpec((None, D_MODEL, tn), lambda j, i: (layer, 0, nb + j)),
            pl.BlockSpec((3, tn), lambda j, i: (0, j)),
            pl.BlockSpec((1, tn), lambda j, i: (0, j)),
        ],
        out_specs=pl.BlockSpec((tm, tn), lambda j, i: (i, j)),
        out_shape=jax.ShapeDtypeStruct((m, D_FF), BF16),
        scratch_shapes=[pltpu.VMEM((2, D_MODEL, tn), BF16)],
        compiler_params=_params(("parallel", "arbitrary")),
        name="ffn_up",
    )(u, u, w_up, w_up, conv_w, conv_b)


def _ffn_down_kernel(final, act_ref, w_ref, h_ref, g_ref, *out_refs):
    hn = h_ref[...] + _dot(act_ref[...], w_ref[...])
    y = _rms(hn, g_ref[...])
    if final:
        out_refs[0][...] = y
    else:
        out_refs[0][...] = hn
        out_refs[1][...] = y.astype(BF16)


def _ffn_down(act, w_down, layer, h, g, final):
    m = h.shape[0]
    if final:
        tm = 256
        m = m // L * SEQ
        per_batch = SEQ // tm

        def stream_row(i):
            return pl.multiple_of(i * tm + N_META * (i // per_batch + 1), N_META)

        act_spec = pl.BlockSpec((pl.Element(tm), pl.Element(D_FF)), lambda i: (stream_row(i), 0))
        h_spec = pl.BlockSpec((pl.Element(tm), pl.Element(D_MODEL)), lambda i: (stream_row(i), 0))
        row = pl.BlockSpec((tm, D_MODEL), lambda i: (i, 0))
        out_specs = [row]
        out_shape = [jax.ShapeDtypeStruct((m, D_MODEL), F32)]
    else:
        tm = 384
        row = pl.BlockSpec((tm, D_MODEL), lambda i: (i, 0))
        act_spec = pl.BlockSpec((tm, D_FF), lambda i: (i, 0))
        h_spec = row
        out_specs = [row, row]
        out_shape = [jax.ShapeDtypeStruct((m, D_MODEL), F32),
                     jax.ShapeDtypeStruct((m, D_MODEL), BF16)]
    return pl.pallas_call(
        functools.partial(_ffn_down_kernel, final),
        grid=(m // tm,),
        in_specs=[
            act_spec,
            _resident((None, D_FF, D_MODEL), lambda i: (layer, 0, 0)),
            h_spec,
            _resident((1, D_MODEL), lambda i: (0, 0)),
        ],
        out_specs=out_specs,
        out_shape=out_shape,
        compiler_params=_params(("parallel",)),
        name="ffn_down_final" if final else "ffn_down",
    )(act, w_down, h, g)


def kernel(x, meta, norm_mix, w_in, conv_a, b_if, ml_norm, da_lambda, da_norm, w_br_a, w_br_m, w_br_d,
           w_out, norm_ffn, w_up, conv_ffn, conv_ffn_b, w_down, norm_f):
    bsz, seq, d = x.shape
    assert (seq, d) == (SEQ, D_MODEL)
    depth = w_in.shape[0]
    w_in_t = jnp.swapaxes(w_in, 1, 2).reshape(depth * W_IN, D_MODEL)
    wa, wm, wd, wo, wdn = (w.astype(BF16) for w in (w_br_a, w_br_m, w_br_d, w_out, w_down))
    h, u = _embed(x.reshape(bsz * SEQ, D_MODEL), meta, norm_mix[0][None], bsz)
    for i in range(depth):
        z = _inproj(u, w_in_t, i)
        qt, vt, zif = _attn_proj(u, w_in_t, i, bsz)
        bias = jnp.pad(b_if[i].reshape(1, 2 * ML_HEADS), ((0, 0), (0, IF_W - 2 * ML_HEADS)))
        hm = _mlstm(z, zif, bias, ml_norm[i][None], bsz)
        hd = _attn(z, qt, vt, da_lambda[i], da_norm[i][:, None], i, bsz)
        h, u = _merge(z, hm, hd, h, i, conv_a[i], wa, wm, wd, wo, norm_ffn[i][None])
        act = _ffn_up(u, w_up, i, conv_ffn[i], conv_ffn_b[i][None])
        final = i == depth - 1
        g_next = norm_f if final else norm_mix[i + 1]
        outs = _ffn_down(act, wdn, i, h, g_next[None], final)
        if final:
            y = outs[0]
        else:
            h, u = outs
    return y.reshape(bsz, SEQ, D_MODEL)
```

```python
import functools
import math

import jax
import jax.numpy as jnp
from jax import lax
from jax.experimental import pallas as pl
from jax.experimental.pallas import tpu as pltpu

F32 = jnp.float32
BF16 = jnp.bfloat16

D_MODEL = 2048
SEQ = 2048
N_META = 16
L = N_META + SEQ
CHUNK = 64
EPS = 1e-6

CONV_DIM = 512
ML_HEADS = 4
ML_DK = 256
ML_DV = 256
ML_DIM = ML_HEADS * ML_DV
DA_HEADS = 4
DA_HD = 64
DA_VD = 2 * DA_HD
DA_DIM = DA_HEADS * DA_VD
D_FF = 5632

Z_GATE = 0
Z_AX = 3 * D_MODEL
Z_DK = Z_AX + 3 * CONV_DIM
Z_MQ = Z_DK + DA_DIM
Z_N = Z_MQ + 4 * ML_DIM
IF_W = 128

ML_T = 128
ML_EXT = ML_DV + 128
HALO = 16
ATT_QB = 256
EMB_T = L // 3
VMEM_LIMIT = 56 * 1024 * 1024


def _dot(a, b):
    return jnp.dot(a, b, preferred_element_type=F32)


def _dot_nt(a, b):
    return lax.dot_general(a, b, (((1,), (1,)), ((), ())), preferred_element_type=F32)


def _dot_tn(a, b):
    return lax.dot_general(a, b, (((0,), (0,)), ((), ())), preferred_element_type=F32)


def _bdot(a, b, ca, cb):
    return lax.dot_general(a, b, (((ca,), (cb,)), ((0,), (0,))), preferred_element_type=F32)


def _rms(x, g):
    return x * lax.rsqrt(jnp.mean(x * x, axis=-1, keepdims=True) + EPS) * g


def _sigmoid(x):
    return 1.0 / (1.0 + jnp.exp(-x))


def _pos_in_batch(r0, tm):
    pos = r0 % L + lax.broadcasted_iota(jnp.int32, (tm, 1), 0)
    return jnp.where(pos >= L, pos - L, pos)


def _causal_conv3(prev, cur, w, pos):
    cc = jnp.concatenate([prev, cur], axis=0)
    x1 = jnp.where(pos >= 1, pltpu.roll(cc, 1, 0)[HALO:], 0.0)
    x2 = jnp.where(pos >= 2, pltpu.roll(cc, 2, 0)[HALO:], 0.0)
    return w[0:1] * x2 + w[1:2] * x1 + w[2:3] * cur


def _params(sem):
    return pltpu.CompilerParams(dimension_semantics=sem, vmem_limit_bytes=VMEM_LIMIT)


def _resident(shape, index_map):
    return pl.BlockSpec(shape, index_map, pipeline_mode=pl.Buffered(1))


def _embed_kernel(x_ref, meta_ref, g_ref, h_ref, u_ref):
    j = pl.program_id(1)

    @pl.when(j == 0)
    def _():
        rows = jnp.concatenate([meta_ref[...], x_ref[0:EMB_T - N_META, :]], axis=0)
        h_ref[...] = rows
        u_ref[...] = _rms(rows, g_ref[...]).astype(BF16)

    @pl.when(j > 0)
    def _():
        rows = x_ref[...]
        h_ref[...] = rows
        u_ref[...] = _rms(rows, g_ref[...]).astype(BF16)


def _embed(x2d, meta, g, bsz):
    nb = L // EMB_T
    m = bsz * L
    return pl.pallas_call(
        _embed_kernel,
        grid=(bsz, nb),
        in_specs=[
            pl.BlockSpec((pl.Element(EMB_T), pl.Element(D_MODEL)),
                         lambda b, j: (pl.multiple_of(b * SEQ + jnp.maximum(j * EMB_T - N_META, 0), 16), 0)),
            pl.BlockSpec((N_META, D_MODEL), lambda b, j: (0, 0)),
            pl.BlockSpec((1, D_MODEL), lambda b, j: (0, 0)),
        ],
        out_specs=[
            pl.BlockSpec((EMB_T, D_MODEL), lambda b, j: (b * nb + j, 0)),
            pl.BlockSpec((EMB_T, D_MODEL), lambda b, j: (b * nb + j, 0)),
        ],
        out_shape=[jax.ShapeDtypeStruct((m, D_MODEL), F32),
                   jax.ShapeDtypeStruct((m, D_MODEL), BF16)],
        compiler_params=_params(("parallel", "arbitrary")),
        name="embed_norm",
    )(x2d, meta, g)


W_A_END = 3 * CONV_DIM
W_M_END = W_A_END + 4 * ML_DIM
W_IF_END = W_M_END + 2 * ML_HEADS
W_D_END = W_IF_END + 3 * DA_DIM
W_IN = W_D_END + 3 * D_MODEL


IN_SRC = 512
IN_NSRC = 2


def _inproj_kernel(u_ref, *refs):
    wt_refs, z_ref = refs[:IN_NSRC], refs[IN_NSRC]
    u = u_ref[...]
    for k, wt_ref in enumerate(wt_refs):
        z_ref[:, k * IN_SRC:(k + 1) * IN_SRC] = _dot_nt(u, wt_ref[...].astype(BF16)).astype(BF16)


def _inproj(u, w_in_t, layer, tm=L):
    m = u.shape[0]
    tn = IN_NSRC * IN_SRC
    n_gate, n_a = 3 * D_MODEL // IN_SRC, 3 * CONV_DIM // IN_SRC

    def src_row(t):
        gate = W_D_END + IN_SRC * t
        cnv = IN_SRC * (t - n_gate)
        att = W_IF_END + DA_DIM + IN_SRC * (t - n_gate - n_a)
        mls = W_A_END + IN_SRC * (t - n_gate - n_a - 1)
        row = jnp.where(t < n_gate, gate,
                        jnp.where(t < n_gate + n_a, cnv, jnp.where(t < n_gate + n_a + 1, att, mls)))
        return pl.multiple_of(layer * W_IN + row, 8)

    def src_spec(k):
        return pl.BlockSpec((pl.Element(IN_SRC), pl.Element(D_MODEL)),
                            lambda i, j: (src_row(IN_NSRC * j + k), 0))

    return pl.pallas_call(
        _inproj_kernel,
        grid=(m // tm, Z_N // tn),
        in_specs=[pl.BlockSpec((tm, D_MODEL), lambda i, j: (i, 0))] + [src_spec(k) for k in range(IN_NSRC)],
        out_specs=pl.BlockSpec((tm, tn), lambda i, j: (i, j)),
        out_shape=jax.ShapeDtypeStruct((m, Z_N), BF16),
        compiler_params=_params(("parallel", "arbitrary")),
        name="in_proj",
    )(u, *([w_in_t] * IN_NSRC))


ML_NCH = -(-L // ML_T)
ML_TAIL_VALID = ML_T - (L - (ML_NCH - 1) * ML_T)


def _mlstm_gates(if_ref, bias_ref, row_scr, col_scr):
    t = ML_T
    ninf = -jnp.inf
    bias = bias_ref[...]
    starts = [c * t for c in range(ML_NCH - 1)] + [L - t]
    gt = jnp.concatenate([(if_ref[r0:r0 + t, :] + bias).T[0:8, :] for r0 in starts], axis=0)
    shape = (8 * ML_NCH, t)
    lane = lax.broadcasted_iota(jnp.int32, shape, 1)
    row = lax.broadcasted_iota(jnp.int32, shape, 0)
    is_i = (row & 7) < ML_HEADS
    log_sig = jnp.minimum(gt, 0.0) - jnp.log1p(jnp.exp(-jnp.abs(gt)))
    lg = jnp.where(is_i, gt, log_sig)
    weightless = jnp.logical_and(row >= 8 * (ML_NCH - 1), lane < ML_TAIL_VALID)
    lg = jnp.where(weightless, jnp.where(is_i, ninf, 0.0), lg)
    cs = jnp.where(is_i, 0.0, lg)
    sh = 1
    while sh < t:
        cs = cs + jnp.where(lane >= sh, pltpu.roll(cs, sh, 1), 0.0)
        sh *= 2
    rowpack = jnp.where(is_i, lg, cs)
    row_scr[...] = rowpack
    pad = jnp.zeros((t - 8, t), F32)
    for c in range(ML_NCH):
        col_scr[c] = jnp.concatenate([rowpack[8 * c:8 * c + 8, :], pad], axis=0).T


def _mlstm_chunk(c, r0, first_valid, q_ref, k_ref, v_ref, o_ref, g_ref, out_ref, s_ref, m_ref, row_scr, col_scr):
    t = ML_T
    ninf = -jnp.inf
    rows = pl.ds(r0, t)
    row0 = c * 8 if isinstance(c, int) else pl.multiple_of(c * 8, 8)
    rowpack = row_scr[pl.ds(row0, 8), :]
    colpack = col_scr[c]

    tri = lax.broadcasted_iota(jnp.int32, (t, t), 0) >= lax.broadcasted_iota(jnp.int32, (t, t), 1)
    nh = ML_HEADS

    def heads(x, d):
        return jnp.stack([x[:, h * d:(h + 1) * d] for h in range(nh)], axis=0)

    q = heads(q_ref[rows, :], ML_DK)
    k = heads(k_ref[rows, :], ML_DK) * jnp.asarray(ML_DK ** -0.5, BF16)
    vext = jnp.concatenate([heads(v_ref[rows, :], ML_DV), jnp.ones((nh, t, ML_EXT - ML_DV), BF16)], axis=2)
    li_r = jnp.stack([rowpack[h:h + 1, :] for h in range(nh)], axis=0)
    b_r = jnp.stack([rowpack[nh + h:nh + h + 1, :] for h in range(nh)], axis=0)
    li_c = jnp.stack([colpack[:, h:h + 1] for h in range(nh)], axis=0)
    b_c = jnp.stack([colpack[:, nh + h:nh + h + 1] for h in range(nh)], axis=0)
    m_prev = m_ref[...][:, :, 0:1]

    dmat = jnp.where(tri[None], b_c - b_r + li_r, ninf)
    inter = b_c + m_prev
    m_t = jnp.maximum(jnp.max(dmat, axis=2, keepdims=True), inter)
    w = jnp.exp(dmat - m_t)
    g = jnp.exp(inter - m_t)
    s = _bdot(q, k, 2, 2) * w
    tot = g * _bdot(q, s_ref[...].astype(BF16), 2, 1) + _bdot(s.astype(BF16), vext, 2, 1)
    num = tot[:, :, :ML_DV]
    den = tot[:, :, ML_DV:ML_DV + 1]
    hh = num / jnp.maximum(jnp.abs(den), jnp.exp(-m_t))
    hn = hh * lax.rsqrt(jnp.mean(hh * hh, axis=2, keepdims=True) + EPS)
    for h in range(nh):
        sl = slice(h * ML_DV, (h + 1) * ML_DV)
        res = (hn[h] * g_ref[:, sl] * _sigmoid(o_ref[rows, sl].astype(F32))).astype(BF16)
        if first_valid:
            out_ref[pl.ds(r0 + first_valid, t - first_valid), sl] = res[first_valid:, :]
        else:
            out_ref[rows, sl] = res

    b_last = b_r[:, :, t - 1:t]
    ds_r = b_last - b_r + li_r
    ds_c = b_last - b_c + li_c
    m_new = jnp.maximum(b_last + m_prev, jnp.max(ds_r, axis=2, keepdims=True))
    wk = jnp.exp(ds_c - m_new)
    gs = jnp.exp(b_last + m_prev - m_new)
    vw = (vext.astype(F32) * wk).astype(BF16)
    s_ref[...] = gs * s_ref[...] + _bdot(k, vw, 1, 1)
    m_ref[...] = jnp.broadcast_to(m_new, (nh, 1, 128))


def _mlstm_kernel(q_ref, k_ref, v_ref, o_ref, if_ref, bias_ref, g_ref, out_ref, s_ref, m_ref, row_scr, col_scr):
    s_ref[...] = jnp.zeros_like(s_ref)
    m_ref[...] = jnp.zeros_like(m_ref)
    _mlstm_gates(if_ref, bias_ref, row_scr, col_scr)
    refs = (q_ref, k_ref, v_ref, o_ref, g_ref, out_ref, s_ref, m_ref, row_scr, col_scr)

    def body(i, carry):
        for c in (2 * i, 2 * i + 1):
            _mlstm_chunk(c, pl.multiple_of(c * ML_T, ML_T), 0, *refs)
        return carry

    lax.fori_loop(0, (ML_NCH - 1) // 2, body, 0)
    _mlstm_chunk(ML_NCH - 1, L - ML_T, ML_TAIL_VALID, *refs)


def _mlstm(z, zif, bias, g, bsz):
    m = bsz * L
    qb = Z_MQ // ML_DIM

    def zspec(off):
        return pl.BlockSpec((L, ML_DIM), lambda b: (b, qb + off))

    return pl.pallas_call(
        _mlstm_kernel,
        grid=(bsz,),
        in_specs=[
            zspec(0), zspec(1), zspec(2), zspec(3),
            pl.BlockSpec((L, IF_W), lambda b: (b, 0)),
            pl.BlockSpec((1, IF_W), lambda b: (0, 0)),
            pl.BlockSpec((1, ML_DIM), lambda b: (0, 0)),
        ],
        out_specs=pl.BlockSpec((L, ML_DIM), lambda b: (b, 0)),
        out_shape=jax.ShapeDtypeStruct((m, ML_DIM), BF16),
        scratch_shapes=[pltpu.VMEM((ML_HEADS, ML_DK, ML_EXT), F32),
                        pltpu.VMEM((ML_HEADS, 1, 128), F32),
                        pltpu.VMEM((8 * ML_NCH, ML_T), F32),
                        pltpu.VMEM((ML_NCH, ML_T, 128), F32)],
        compiler_params=_params(("parallel",)),
        name="mlstm",
    )(z, z, z, z, zif, bias, g)


ATT_MB = 128
ATT_TW = SEQ + ATT_MB
ATT_HPS = 2


def _attn_proj_kernel(u_ref, wq_ref, wv_ref, wif_ref, qt_ref, vt_ref, zif_ref):
    u_fr = u_ref[N_META:, :]
    u_meta = u_ref[0:ATT_MB, :]
    scale = DA_HD ** -0.5 * math.log2(math.e)
    for w_ref, o_ref, mul in ((wq_ref, qt_ref, scale), (wv_ref, vt_ref, 1.0)):
        wt = w_ref[...].astype(BF16)
        o_ref[:, 0:SEQ] = (_dot_nt(wt, u_fr) * mul).astype(BF16)
        o_ref[:, SEQ:ATT_TW] = (_dot_nt(wt, u_meta) * mul).astype(BF16)
    zif_ref[...] = _dot_nt(u_ref[...], wif_ref[...].astype(BF16))


def _attn_proj(u, w_in_t, layer, bsz):
    def wspec(rows, row0):
        return _resident((pl.Element(rows), pl.Element(D_MODEL)), lambda b: (layer * W_IN + row0, 0))

    out = pl.BlockSpec((None, DA_DIM, ATT_TW), lambda b: (b, 0, 0))
    shape = jax.ShapeDtypeStruct((bsz, DA_DIM, ATT_TW), BF16)
    return pl.pallas_call(
        _attn_proj_kernel,
        grid=(bsz,),
        in_specs=[pl.BlockSpec((L, D_MODEL), lambda b: (b, 0)),
                  wspec(DA_DIM, W_IF_END), wspec(DA_DIM, W_IF_END + 2 * DA_DIM), wspec(IF_W, W_M_END)],
        out_specs=[out, out, pl.BlockSpec((L, IF_W), lambda b: (b, 0))],
        out_shape=[shape, shape, jax.ShapeDtypeStruct((bsz * L, IF_W), F32)],
        compiler_params=_params(("parallel",)),
        name="attn_qv_proj",
    )(u, w_in_t, w_in_t, w_in_t)


def _attn_scores(qt, k_ref, s_ref, n_frames, frame0):
    cols = qt.shape[1]
    ninf = -jnp.inf
    first_map = lax.broadcasted_iota(jnp.int32, (DA_VD, cols), 0) < DA_HD
    k_meta = k_ref[0:ATT_MB, :]
    meta_ok = lax.broadcasted_iota(jnp.int32, (ATT_MB, 1), 0) < N_META
    if n_frames:
        k_fr = k_ref[N_META:N_META + n_frames, :]
        qf = frame0 + lax.broadcasted_iota(jnp.int32, (1, cols), 1)
        kend = (qf // CHUNK + 1) * CHUNK
        kf = frame0 + lax.broadcasted_iota(jnp.int32, (n_frames - frame0, 1), 0)
        diag_ok = kf < kend
    for mp in range(2):
        qm = jnp.where(first_map if mp == 0 else jnp.logical_not(first_map), qt, jnp.zeros_like(qt))
        s_ref[mp, 0:ATT_MB, 0:cols] = jnp.where(meta_ok, _dot(k_meta, qm), ninf)
        if n_frames:
            s_fr = _dot(k_fr, qm)
            if frame0:
                s_ref[mp, ATT_MB:ATT_MB + frame0, 0:cols] = s_fr[:frame0, :]
            s_ref[mp, ATT_MB + frame0:ATT_MB + n_frames, 0:cols] = jnp.where(diag_ok, s_fr[frame0:, :], ninf)


def _attn_output(s_ref, vt_ref, cols, n_frames, lam, lam_init, g):
    nk = ATT_MB + n_frames
    probs = []
    for mp in range(2):
        s = s_ref[mp, 0:nk, 0:cols]
        p = jnp.exp2(s - jnp.max(s, axis=0, keepdims=True))
        probs.append((p, jnp.sum(p, axis=0, keepdims=True)))
    a = (probs[0][0] * (1.0 / probs[0][1]) - probs[1][0] * (lam / probs[1][1])).astype(BF16)
    o = _dot(vt_ref[:, SEQ:ATT_TW], a[:ATT_MB, :])
    if n_frames:
        o = o + _dot(vt_ref[:, 0:n_frames], a[ATT_MB:, :])
    o = o * lax.rsqrt(jnp.mean(o * o, axis=0, keepdims=True) + EPS)
    return o * g * (1.0 - lam_init)


def _attn_kernel(lam_init, qt_ref, k_ref, vt_ref, lam_ref, g_ref, out_ref, s_scr):
    lf = lam_ref[...]
    lam = (jnp.exp(jnp.sum(lf[0:1] * lf[1:2], axis=1, keepdims=True))
           - jnp.exp(jnp.sum(lf[2:3] * lf[3:4], axis=1, keepdims=True)) + lam_init)
    g = g_ref[...]
    blocks = [(SEQ, ATT_MB, 0, 0)] + [(f0, ATT_QB, f0 + ATT_QB, f0) for f0 in range(0, SEQ, ATT_QB)]
    tasks = [(hh,) + blk for hh in range(ATT_HPS) for blk in blocks]

    def head(ref, hh, axis):
        sl = pl.ds(hh * DA_VD, DA_VD)
        return ref.at[sl, :] if axis == 0 else ref.at[:, sl]

    def scores(i):
        hh, lane0, cols, n_frames, frame0 = tasks[i]
        qt = qt_ref[hh * DA_VD:(hh + 1) * DA_VD, lane0:lane0 + cols]
        _attn_scores(qt, head(k_ref, hh, 1), s_scr.at[i % 2], n_frames, frame0)

    scores(0)
    for i, (hh, lane0, cols, n_frames, frame0) in enumerate(tasks):
        if i + 1 < len(tasks):
            scores(i + 1)
        o = _attn_output(s_scr.at[i % 2], head(vt_ref, hh, 0), cols, n_frames, lam, lam_init, g)
        lanes = slice(hh * DA_VD, (hh + 1) * DA_VD)
        if n_frames:
            out_ref[N_META + frame0:N_META + frame0 + cols, lanes] = o.T.astype(BF16)
        else:
            out_ref[0:N_META, lanes] = o.T[0:N_META, :].astype(BF16)


def _attn(z, qt, vt, lam_p, g, layer, bsz):
    lam_init = 0.8 - 0.6 * math.exp(-0.3 * layer)
    m = bsz * L
    hw = ATT_HPS * DA_VD
    kb = Z_DK // hw

    def tspec():
        return pl.BlockSpec((None, hw, ATT_TW), lambda b, h: (b, h, 0))

    return pl.pallas_call(
        functools.partial(_attn_kernel, lam_init),
        grid=(bsz, DA_HEADS // ATT_HPS),
        in_specs=[
            tspec(),
            pl.BlockSpec((L, hw), lambda b, h: (b, kb + h)),
            tspec(),
            pl.BlockSpec((4, DA_HD), lambda b, h: (0, 0)),
            pl.BlockSpec((DA_VD, 1), lambda b, h: (0, 0)),
        ],
        out_specs=pl.BlockSpec((L, hw), lambda b, h: (b, h)),
        out_shape=jax.ShapeDtypeStruct((m, DA_DIM), BF16),
        scratch_shapes=[pltpu.VMEM((2, 2, ATT_TW, ATT_QB), F32)],
        compiler_params=_params(("parallel", "parallel")),
        name="diff_attn",
    )(qt, z, vt, lam_p, g)


MERGE_NC = 512


def _merge_kernel(tm, a_ref, ah_ref, hm_ref, hd_ref, gate_ref, h_ref, cw_ref, wa_ref, wm_ref, wd_ref, wo_ref,
                  g_ref, hn_ref, u_ref, mg_scr):
    ax, ab, ac = (slice(k * CONV_DIM, (k + 1) * CONV_DIM) for k in range(3))
    pos = _pos_in_batch(pl.program_id(0) * tm, tm)
    cur = a_ref[:, ac].astype(F32) * a_ref[:, ax].astype(F32)
    prev = ah_ref[:, ac].astype(F32) * ah_ref[:, ax].astype(F32)
    a_act = (a_ref[:, ab].astype(F32) * _causal_conv3(prev, cur, cw_ref[...], pos)).astype(BF16)
    hm = hm_ref[...]
    hd = hd_ref[...]
    for c0 in range(0, D_MODEL, MERGE_NC):
        cs = slice(c0, c0 + MERGE_NC)
        ga, gm, gd = (gate_ref[:, k * D_MODEL + c0:k * D_MODEL + c0 + MERGE_NC].astype(F32) for k in range(3))
        merged = (_sigmoid(ga) * _dot(a_act, wa_ref[:, cs])
                  + _sigmoid(gm) * _dot(hm, wm_ref[:, cs])
                  + _sigmoid(gd) * _dot(hd, wd_ref[:, cs]))
        mg_scr[:, cs] = merged.astype(BF16)
    hn = h_ref[...] + _dot(mg_scr[...], wo_ref[...])
    hn_ref[...] = hn
    u_ref[...] = _rms(hn, g_ref[...]).astype(BF16)


def _merge(z, hm, hd, h, layer, conv_a, w_br_a, w_br_m, w_br_d, w_out, g, tm=384):
    m = h.shape[0]
    a_w = 3 * CONV_DIM
    ab = Z_AX // a_w
    hb = tm // HALO

    def weight(rows):
        return _resident((None, rows, D_MODEL), lambda i: (layer, 0, 0))

    return pl.pallas_call(
        functools.partial(_merge_kernel, tm),
        grid=(m // tm,),
        in_specs=[
            pl.BlockSpec((tm, a_w), lambda i: (i, ab)),
            pl.BlockSpec((HALO, a_w), lambda i: (jnp.maximum(i * hb - 1, 0), ab)),
            pl.BlockSpec((tm, ML_DIM), lambda i: (i, 0)),
            pl.BlockSpec((tm, DA_DIM), lambda i: (i, 0)),
            pl.BlockSpec((tm, 3 * D_MODEL), lambda i: (i, 0)),
            pl.BlockSpec((tm, D_MODEL), lambda i: (i, 0)),
            _resident((3, CONV_DIM), lambda i: (0, 0)),
            weight(CONV_DIM), weight(ML_DIM), weight(DA_DIM), weight(D_MODEL),
            _resident((1, D_MODEL), lambda i: (0, 0)),
        ],
        out_specs=[
            pl.BlockSpec((tm, D_MODEL), lambda i: (i, 0)),
            pl.BlockSpec((tm, D_MODEL), lambda i: (i, 0)),
        ],
        out_shape=[jax.ShapeDtypeStruct((m, D_MODEL), F32),
                   jax.ShapeDtypeStruct((m, D_MODEL), BF16)],
        scratch_shapes=[pltpu.VMEM((tm, D_MODEL), BF16)],
        compiler_params=_params(("parallel",)),
        name="merge_out_proj",
    )(z, z, hm, hd, z, h, conv_a, w_br_a, w_br_m, w_br_d, w_out, g)


FFN_ROWS = 2


def _row_blocks(tm, n):
    units = tm // HALO
    edges = [HALO * (units * k // n) for k in range(n + 1)]
    return list(zip(edges[:-1], edges[1:]))


def _ffn_up_kernel(tm, u_ref, uh_ref, wa_ref, wb_ref, cw_ref, cb_ref, out_ref, w_scr):
    @pl.when(pl.program_id(1) == 0)
    def _():
        w_scr[0] = wa_ref[...].astype(BF16)
        w_scr[1] = wb_ref[...].astype(BF16)

    r_tile = pl.program_id(1) * tm
    a_prev = _dot(uh_ref[...], w_scr[0])
    for r0, r1 in _row_blocks(tm, FFN_ROWS):
        u = u_ref[r0:r1, :]
        a = _dot(u, w_scr[0])
        conv = _causal_conv3(a_prev, a, cw_ref[...], _pos_in_batch(r_tile + r0, r1 - r0)) + cb_ref[...]
        gelu = 0.5 * conv * (1.0 + lax.erf(conv * (2.0 ** -0.5)))
        out_ref[r0:r1, :] = (gelu * _dot(u, w_scr[1])).astype(BF16)
        a_prev = a[r1 - r0 - HALO:, :]


def _ffn_up(u, w_up, layer, conv_w, conv_b, tm=L, tn=512):
    m = u.shape[0]
    nb = D_FF // tn
    hb = tm // HALO
    return pl.pallas_call(
        functools.partial(_ffn_up_kernel, tm),
        grid=(nb, m // tm),
        in_specs=[
            pl.BlockSpec((tm, D_MODEL), lambda j, i: (i, 0)),
            pl.BlockSpec((HALO, D_MODEL), lambda j, i: (jnp.maximum(i * hb - 1, 0), 0)),
            pl.BlockSpec((None, D_MODEL, tn), lambda j, i: (layer, 0, j)),
            pl.BlockSpec((None, D_MODEL, tn), lambda j, i: (layer, 0, nb + j)),
            pl.BlockSpec((3, tn), lambda j, i: (0, j)),
            pl.BlockSpec((1, tn), lambda j, i: (0, j)),
        ],
        out_specs=pl.BlockSpec((tm, tn), lambda j, i: (i, j)),
        out_shape=jax.ShapeDtypeStruct((m, D_FF), BF16),
        scratch_shapes=[pltpu.VMEM((2, D_MODEL, tn), BF16)],
        compiler_params=_params(("parallel", "arbitrary")),
        name="ffn_up",
    )(u, u, w_up, w_up, conv_w, conv_b)


def _ffn_down_kernel(final, act_ref, w_ref, h_ref, g_ref, *out_refs):
    hn = h_ref[...] + _dot(act_ref[...], w_ref[...])
    y = _rms(hn, g_ref[...])
    if final:
        out_refs[0][...] = y
    else:
        out_refs[0][...] = hn
        out_refs[1][...] = y.astype(BF16)


def _ffn_down(act, w_down, layer, h, g, final):
    m = h.shape[0]
    if final:
        tm = 256
        m = m // L * SEQ
        per_batch = SEQ // tm

        def stream_row(i):
            return pl.multiple_of(i * tm + N_META * (i // per_batch + 1), N_META)

        act_spec = pl.BlockSpec((pl.Element(tm), pl.Element(D_FF)), lambda i: (stream_row(i), 0))
        h_spec = pl.BlockSpec((pl.Element(tm), pl.Element(D_MODEL)), lambda i: (stream_row(i), 0))
        row = pl.BlockSpec((tm, D_MODEL), lambda i: (i, 0))
        out_specs = [row]
        out_shape = [jax.ShapeDtypeStruct((m, D_MODEL), F32)]
    else:
        tm = 384
        row = pl.BlockSpec((tm, D_MODEL), lambda i: (i, 0))
        act_spec = pl.BlockSpec((tm, D_FF), lambda i: (i, 0))
        h_spec = row
        out_specs = [row, row]
        out_shape = [jax.ShapeDtypeStruct((m, D_MODEL), F32),
                     jax.ShapeDtypeStruct((m, D_MODEL), BF16)]
    return pl.pallas_call(
        functools.partial(_ffn_down_kernel, final),
        grid=(m // tm,),
        in_specs=[
            act_spec,
            _resident((None, D_FF, D_MODEL), lambda i: (layer, 0, 0)),
            h_spec,
            _resident((1, D_MODEL), lambda i: (0, 0)),
        ],
        out_specs=out_specs,
        out_shape=out_shape,
        compiler_params=_params(("parallel",)),
        name="ffn_down_final" if final else "ffn_down",
    )(act, w_down, h, g)


def kernel(x, meta, norm_mix, w_in, conv_a, b_if, ml_norm, da_lambda, da_norm, w_br_a, w_br_m, w_br_d,
           w_out, norm_ffn, w_up, conv_ffn, conv_ffn_b, w_down, norm_f):
    bsz, seq, d = x.shape
    assert (seq, d) == (SEQ, D_MODEL)
    depth = w_in.shape[0]
    w_in_t = jnp.swapaxes(w_in, 1, 2).reshape(depth * W_IN, D_MODEL)
    wa, wm, wd, wo, wdn = (w.astype(BF16) for w in (w_br_a, w_br_m, w_br_d, w_out, w_down))
    h, u = _embed(x.reshape(bsz * SEQ, D_MODEL), meta, norm_mix[0][None], bsz)
    for i in range(depth):
        z = _inproj(u, w_in_t, i)
        qt, vt, zif = _attn_proj(u, w_in_t, i, bsz)
        bias = jnp.pad(b_if[i].reshape(1, 2 * ML_HEADS), ((0, 0), (0, IF_W - 2 * ML_HEADS)))
        hm = _mlstm(z, zif, bias, ml_norm[i][None], bsz)
        hd = _attn(z, qt, vt, da_lambda[i], da_norm[i][:, None], i, bsz)
        h, u = _merge(z, hm, hd, h, i, conv_a[i], wa, wm, wd, wo, norm_ffn[i][None])
        act = _ffn_up(u, w_up, i, conv_ffn[i], conv_ffn_b[i][None])
        final = i == depth - 1
        g_next = norm_f if final else norm_mix[i + 1]
        outs = _ffn_down(act, wdn, i, h, g_next[None], final)
        if final:
            y = outs[0]
        else:
            h, u = outs
    return y.reshape(bsz, SEQ, D_MODEL)
```

```python
import functools
import math

import jax
import jax.numpy as jnp
from jax import lax
from jax.experimental import pallas as pl
from jax.experimental.pallas import tpu as pltpu

F32 = jnp.float32
BF16 = jnp.bfloat16

D_MODEL = 2048
SEQ = 2048
N_META = 16
L = N_META + SEQ
CHUNK = 64
EPS = 1e-6

CONV_DIM = 512
ML_HEADS = 4
ML_DK = 256
ML_DV = 256
ML_DIM = ML_HEADS * ML_DV
DA_HEADS = 4
DA_HD = 64
DA_VD = 2 * DA_HD
DA_DIM = DA_HEADS * DA_VD
D_FF = 5632

Z_GATE = 0
Z_AX = 3 * D_MODEL
Z_DK = Z_AX + 3 * CONV_DIM
Z_MQ = Z_DK + DA_DIM
Z_N = Z_MQ + 4 * ML_DIM
IF_W = 128

ML_T = 128
ML_EXT = ML_DV + 128
HALO = 16
ATT_QB = 256
EMB_T = L // 3
VMEM_LIMIT = 56 * 1024 * 1024


def _dot(a, b):
    return jnp.dot(a, b, preferred_element_type=F32)


def _dot_nt(a, b):
    return lax.dot_general(a, b, (((1,), (1,)), ((), ())), preferred_element_type=F32)


def _dot_tn(a, b):
    return lax.dot_general(a, b, (((0,), (0,)), ((), ())), preferred_element_type=F32)


def _bdot(a, b, ca, cb):
    return lax.dot_general(a, b, (((ca,), (cb,)), ((0,), (0,))), preferred_element_type=F32)


def _rms(x, g):
    return x * lax.rsqrt(jnp.mean(x * x, axis=-1, keepdims=True) + EPS) * g


def _sigmoid(x):
    return 1.0 / (1.0 + jnp.exp(-x))


def _pos_in_batch(r0, tm):
    pos = r0 % L + lax.broadcasted_iota(jnp.int32, (tm, 1), 0)
    return jnp.where(pos >= L, pos - L, pos)


def _causal_conv3(prev, cur, w, pos):
    cc = jnp.concatenate([prev, cur], axis=0)
    x1 = jnp.where(pos >= 1, pltpu.roll(cc, 1, 0)[HALO:], 0.0)
    x2 = jnp.where(pos >= 2, pltpu.roll(cc, 2, 0)[HALO:], 0.0)
    return w[0:1] * x2 + w[1:2] * x1 + w[2:3] * cur


def _params(sem):
    return pltpu.CompilerParams(dimension_semantics=sem, vmem_limit_bytes=VMEM_LIMIT)


def _resident(shape, index_map):
    return pl.BlockSpec(shape, index_map, pipeline_mode=pl.Buffered(1))


def _embed_kernel(x_ref, meta_ref, g_ref, h_ref, u_ref):
    j = pl.program_id(1)

    @pl.when(j == 0)
    def _():
        rows = jnp.concatenate([meta_ref[...], x_ref[0:EMB_T - N_META, :]], axis=0)
        h_ref[...] = rows
        u_ref[...] = _rms(rows, g_ref[...]).astype(BF16)

    @pl.when(j > 0)
    def _():
        rows = x_ref[...]
        h_ref[...] = rows
        u_ref[...] = _rms(rows, g_ref[...]).astype(BF16)


def _embed(x2d, meta, g, bsz):
    nb = L // EMB_T
    m = bsz * L
    return pl.pallas_call(
        _embed_kernel,
        grid=(bsz, nb),
        in_specs=[
            pl.BlockSpec((pl.Element(EMB_T), pl.Element(D_MODEL)),
                         lambda b, j: (pl.multiple_of(b * SEQ + jnp.maximum(j * EMB_T - N_META, 0), 16), 0)),
            pl.BlockSpec((N_META, D_MODEL), lambda b, j: (0, 0)),
            pl.BlockSpec((1, D_MODEL), lambda b, j: (0, 0)),
        ],
        out_specs=[
            pl.BlockSpec((EMB_T, D_MODEL), lambda b, j: (b * nb + j, 0)),
            pl.BlockSpec((EMB_T, D_MODEL), lambda b, j: (b * nb + j, 0)),
        ],
        out_shape=[jax.ShapeDtypeStruct((m, D_MODEL), F32),
                   jax.ShapeDtypeStruct((m, D_MODEL), BF16)],
        compiler_params=_params(("parallel", "arbitrary")),
        name="embed_norm",
    )(x2d, meta, g)


W_A_END = 3 * CONV_DIM
W_M_END = W_A_END + 4 * ML_DIM
W_IF_END = W_M_END + 2 * ML_HEADS
W_D_END = W_IF_END + 3 * DA_DIM
W_IN = W_D_END + 3 * D_MODEL


IN_SRC = 512
IN_NSRC = 2


def _inproj_kernel(u_ref, *refs):
    wt_refs, z_ref = refs[:IN_NSRC], refs[IN_NSRC]
    u = u_ref[...]
    for k, wt_ref in enumerate(wt_refs):
        z_ref[:, k * IN_SRC:(k + 1) * IN_SRC] = _dot_nt(u, wt_ref[...].astype(BF16)).astype(BF16)


def _inproj(u, w_in_t, layer, tm=L):
    m = u.shape[0]
    tn = IN_NSRC * IN_SRC
    n_gate, n_a = 3 * D_MODEL // IN_SRC, 3 * CONV_DIM // IN_SRC

    def src_row(t):
        gate = W_D_END + IN_SRC * t
        cnv = IN_SRC * (t - n_gate)
        att = W_IF_END + DA_DIM + IN_SRC * (t - n_gate - n_a)
        mls = W_A_END + IN_SRC * (t - n_gate - n_a - 1)
        row = jnp.where(t < n_gate, gate,
                        jnp.where(t < n_gate + n_a, cnv, jnp.where(t < n_gate + n_a + 1, att, mls)))
        return pl.multiple_of(layer * W_IN + row, 8)

    def src_spec(k):
        return pl.BlockSpec((pl.Element(IN_SRC), pl.Element(D_MODEL)),
                            lambda i, j: (src_row(IN_NSRC * j + k), 0))

    return pl.pallas_call(
        _inproj_kernel,
        grid=(m // tm, Z_N // tn),
        in_specs=[pl.BlockSpec((tm, D_MODEL), lambda i, j: (i, 0))] + [src_spec(k) for k in range(IN_NSRC)],
        out_specs=pl.BlockSpec((tm, tn), lambda i, j: (i, j)),
        out_shape=jax.ShapeDtypeStruct((m, Z_N), BF16),
        compiler_params=_params(("parallel", "arbitrary")),
        name="in_proj",
    )(u, *([w_in_t] * IN_NSRC))


ML_NCH = -(-L // ML_T)
ML_TAIL_VALID = ML_T - (L - (ML_NCH - 1) * ML_T)


def _mlstm_gates(if_ref, bias_ref, ab_scr, cm_scr):
    t = ML_T
    ninf = -jnp.inf
    bias = bias_ref[...]
    starts = [c * t for c in range(ML_NCH - 1)] + [L - t]
    gt = jnp.concatenate([(if_ref[r0:r0 + t, :] + bias).T[0:8, :] for r0 in starts], axis=0)
    shape = (8 * ML_NCH, t)
    lane = lax.broadcasted_iota(jnp.int32, shape, 1)
    row = lax.broadcasted_iota(jnp.int32, shape, 0)
    is_i = (row & 7) < ML_HEADS
    log_sig = jnp.minimum(gt, 0.0) - jnp.log1p(jnp.exp(-jnp.abs(gt)))
    lg = jnp.where(is_i, gt, log_sig)
    weightless = jnp.logical_and(row >= 8 * (ML_NCH - 1), lane < ML_TAIL_VALID)
    lg = jnp.where(weightless, jnp.where(is_i, ninf, 0.0), lg)
    cs = jnp.where(is_i, 0.0, lg)
    sh = 1
    while sh < t:
        cs = cs + jnp.where(lane >= sh, pltpu.roll(cs, sh, 1), 0.0)
        sh *= 2
    a = jnp.where(is_i, lg - pltpu.roll(cs, 8 * ML_NCH - ML_HEADS, 0), ninf)
    cm = a
    sh = 1
    while sh < t:
        cm = jnp.maximum(cm, jnp.where(lane >= sh, pltpu.roll(cm, sh, 1), ninf))
        sh *= 2
    ab_scr[...] = jnp.where(is_i, a, cs)
    cm_scr[...] = cm


def _mlstm_chunk(c, r0, first_valid, q_ref, k_ref, v_ref, o_ref, g_ref, out_ref, s_ref, m_ref, ab_scr, cm_scr):
    t = ML_T
    rows = pl.ds(r0, t)
    row0 = c * 8 if isinstance(c, int) else pl.multiple_of(c * 8, 8)
    ab = ab_scr[pl.ds(row0, 8), :]
    cmr = cm_scr[pl.ds(row0, 8), :]

    tri = lax.broadcasted_iota(jnp.int32, (t, t), 0) >= lax.broadcasted_iota(jnp.int32, (t, t), 1)
    nh = ML_HEADS

    def heads(x, d):
        return jnp.stack([x[:, h * d:(h + 1) * d] for h in range(nh)], axis=0)

    def row_of(x, r):
        return jnp.stack([x[r + h:r + h + 1, :] for h in range(nh)], axis=0)

    def lane_rep(x, r):
        return jnp.stack([jnp.broadcast_to(x[r + h:r + h + 1, :], (t, t)).T for h in range(nh)], axis=0)

    def widen(x, n):
        return jnp.concatenate([x] * n, axis=2)

    q = heads(q_ref[rows, :], ML_DK)
    k = heads(k_ref[rows, :], ML_DK) * jnp.asarray(ML_DK ** -0.5, BF16)
    vext = jnp.concatenate([heads(v_ref[rows, :], ML_DV), jnp.ones((nh, t, ML_EXT - ML_DV), BF16)], axis=2)
    a_r = row_of(ab, 0)
    a_l, b_l, cm_l = lane_rep(ab, 0), lane_rep(ab, nh), lane_rep(cmr, 0)
    m_prev = m_ref[...][:, :, 0:1]

    m_l = jnp.maximum(cm_l, m_prev)
    w = jnp.where(tri[None], jnp.exp(a_r - m_l), 0.0)
    g_l = jnp.exp(m_prev - m_l)
    s = _bdot(q, k, 2, 2) * w
    tot = widen(g_l, 3) * _bdot(q, s_ref[...].astype(BF16), 2, 1) + _bdot(s.astype(BF16), vext, 2, 1)
    scale_l = 1.0 / jnp.maximum(jnp.abs(tot[:, :, ML_DV:]), jnp.exp(-(b_l + m_l)))
    hh = tot[:, :, :ML_DV] * widen(scale_l, 2)
    ssq_l = _bdot((hh * hh).astype(BF16), jnp.ones((nh, ML_DV, 128), BF16), 2, 1)
    hn = hh * widen(lax.rsqrt(ssq_l * (1.0 / ML_DV) + EPS), 2)
    for h in range(nh):
        sl = slice(h * ML_DV, (h + 1) * ML_DV)
        res = (hn[h] * g_ref[:, sl] * _sigmoid(o_ref[rows, sl].astype(F32))).astype(BF16)
        if first_valid:
            out_ref[pl.ds(r0 + first_valid, t - first_valid), sl] = res[first_valid:, :]
        else:
            out_ref[rows, sl] = res

    b_last = row_of(ab, nh)[:, :, t - 1:t]
    m_x = jnp.maximum(m_prev, row_of(cmr, 0)[:, :, t - 1:t])
    vw = (vext.astype(F32) * widen(jnp.exp(a_l - m_x), 3)).astype(BF16)
    s_ref[...] = jnp.exp(m_prev - m_x) * s_ref[...] + _bdot(k, vw, 1, 1)
    m_ref[...] = jnp.broadcast_to(b_last + m_x, (nh, 1, 128))


def _mlstm_kernel(q_ref, k_ref, v_ref, o_ref, if_ref, bias_ref, g_ref, out_ref, s_ref, m_ref, ab_scr, cm_scr):
    s_ref[...] = jnp.zeros_like(s_ref)
    m_ref[...] = jnp.zeros_like(m_ref)
    _mlstm_gates(if_ref, bias_ref, ab_scr, cm_scr)
    refs = (q_ref, k_ref, v_ref, o_ref, g_ref, out_ref, s_ref, m_ref, ab_scr, cm_scr)

    def body(i, carry):
        for c in (2 * i, 2 * i + 1):
            _mlstm_chunk(c, pl.multiple_of(c * ML_T, ML_T), 0, *refs)
        return carry

    lax.fori_loop(0, (ML_NCH - 1) // 2, body, 0)
    _mlstm_chunk(ML_NCH - 1, L - ML_T, ML_TAIL_VALID, *refs)


def _mlstm(z, zif, bias, g, bsz):
    m = bsz * L
    qb = Z_MQ // ML_DIM

    def zspec(off):
        return pl.BlockSpec((L, ML_DIM), lambda b: (b, qb + off))

    return pl.pallas_call(
        _mlstm_kernel,
        grid=(bsz,),
        in_specs=[
            zspec(0), zspec(1), zspec(2), zspec(3),
            pl.BlockSpec((L, IF_W), lambda b: (b, 0)),
            pl.BlockSpec((1, IF_W), lambda b: (0, 0)),
            pl.BlockSpec((1, ML_DIM), lambda b: (0, 0)),
        ],
        out_specs=pl.BlockSpec((L, ML_DIM), lambda b: (b, 0)),
        out_shape=jax.ShapeDtypeStruct((m, ML_DIM), BF16),
        scratch_shapes=[pltpu.VMEM((ML_HEADS, ML_DK, ML_EXT), F32),
                        pltpu.VMEM((ML_HEADS, 1, 128), F32),
                        pltpu.VMEM((8 * ML_NCH, ML_T), F32),
                        pltpu.VMEM((8 * ML_NCH, ML_T), F32)],
        compiler_params=_params(("parallel",)),
        name="mlstm",
    )(z, z, z, z, zif, bias, g)


ATT_MB = 128
ATT_TW = SEQ + ATT_MB
ATT_HPS = 2


def _attn_proj_kernel(u_ref, wq_ref, wv_ref, wif_ref, qt_ref, vt_ref, zif_ref):
    u_fr = u_ref[N_META:, :]
    u_meta = u_ref[0:ATT_MB, :]
    scale = DA_HD ** -0.5 * math.log2(math.e)
    for w_ref, o_ref, mul in ((wq_ref, qt_ref, scale), (wv_ref, vt_ref, 1.0)):
        wt = w_ref[...].astype(BF16)
        o_ref[:, 0:SEQ] = (_dot_nt(wt, u_fr) * mul).astype(BF16)
        o_ref[:, SEQ:ATT_TW] = (_dot_nt(wt, u_meta) * mul).astype(BF16)
    zif_ref[...] = _dot_nt(u_ref[...], wif_ref[...].astype(BF16))


def _attn_proj(u, w_in_t, layer, bsz):
    def wspec(rows, row0):
        return _resident((pl.Element(rows), pl.Element(D_MODEL)), lambda b: (layer * W_IN + row0, 0))

    out = pl.BlockSpec((None, DA_DIM, ATT_TW), lambda b: (b, 0, 0))
    shape = jax.ShapeDtypeStruct((bsz, DA_DIM, ATT_TW), BF16)
    return pl.pallas_call(
        _attn_proj_kernel,
        grid=(bsz,),
        in_specs=[pl.BlockSpec((L, D_MODEL), lambda b: (b, 0)),
                  wspec(DA_DIM, W_IF_END), wspec(DA_DIM, W_IF_END + 2 * DA_DIM), wspec(IF_W, W_M_END)],
        out_specs=[out, out, pl.BlockSpec((L, IF_W), lambda b: (b, 0))],
        out_shape=[shape, shape, jax.ShapeDtypeStruct((bsz * L, IF_W), F32)],
        compiler_params=_params(("parallel",)),
        name="attn_qv_proj",
    )(u, w_in_t, w_in_t, w_in_t)


def _attn_scores(qt, k_ref, s_ref, n_frames, frame0):
    cols = qt.shape[1]
    ninf = -jnp.inf
    first_map = lax.broadcasted_iota(jnp.int32, (DA_VD, cols), 0) < DA_HD
    k_meta = k_ref[0:ATT_MB, :]
    meta_ok = lax.broadcasted_iota(jnp.int32, (ATT_MB, 1), 0) < N_META
    if n_frames:
        k_fr = k_ref[N_META:N_META + n_frames, :]
        qf = frame0 + lax.broadcasted_iota(jnp.int32, (1, cols), 1)
        kend = (qf // CHUNK + 1) * CHUNK
        kf = frame0 + lax.broadcasted_iota(jnp.int32, (n_frames - frame0, 1), 0)
        diag_ok = kf < kend
    for mp in range(2):
        qm = jnp.where(first_map if mp == 0 else jnp.logical_not(first_map), qt, jnp.zeros_like(qt))
        s_ref[mp, 0:ATT_MB, 0:cols] = jnp.where(meta_ok, _dot(k_meta, qm), ninf)
        if n_frames:
            s_fr = _dot(k_fr, qm)
            if frame0:
                s_ref[mp, ATT_MB:ATT_MB + frame0, 0:cols] = s_fr[:frame0, :]
            s_ref[mp, ATT_MB + frame0:ATT_MB + n_frames, 0:cols] = jnp.where(diag_ok, s_fr[frame0:, :], ninf)


def _attn_output(s_ref, vt_ref, cols, n_frames, lam, lam_init, g):
    nk = ATT_MB + n_frames
    probs = []
    for mp in range(2):
        s = s_ref[mp, 0:nk, 0:cols]
        p = jnp.exp2(s - jnp.max(s, axis=0, keepdims=True))
        probs.append((p, jnp.sum(p, axis=0, keepdims=True)))
    a = (probs[0][0] * (1.0 / probs[0][1]) - probs[1][0] * (lam / probs[1][1])).astype(BF16)
    o = _dot(vt_ref[:, SEQ:ATT_TW], a[:ATT_MB, :])
    if n_frames:
        o = o + _dot(vt_ref[:, 0:n_frames], a[ATT_MB:, :])
    o = o * lax.rsqrt(jnp.mean(o * o, axis=0, keepdims=True) + EPS)
    return o * g * (1.0 - lam_init)


def _attn_kernel(lam_init, qt_ref, k_ref, vt_ref, lam_ref, g_ref, out_ref, s_scr):
    lf = lam_ref[...]
    lam = (jnp.exp(jnp.sum(lf[0:1] * lf[1:2], axis=1, keepdims=True))
           - jnp.exp(jnp.sum(lf[2:3] * lf[3:4], axis=1, keepdims=True)) + lam_init)
    g = g_ref[...]
    blocks = [(SEQ, ATT_MB, 0, 0)] + [(f0, ATT_QB, f0 + ATT_QB, f0) for f0 in range(0, SEQ, ATT_QB)]
    tasks = [(hh,) + blk for hh in range(ATT_HPS) for blk in blocks]

    def head(ref, hh, axis):
        sl = pl.ds(hh * DA_VD, DA_VD)
        return ref.at[sl, :] if axis == 0 else ref.at[:, sl]

    def scores(i):
        hh, lane0, cols, n_frames, frame0 = tasks[i]
        qt = qt_ref[hh * DA_VD:(hh + 1) * DA_VD, lane0:lane0 + cols]
        _attn_scores(qt, head(k_ref, hh, 1), s_scr.at[i % 2], n_frames, frame0)

    scores(0)
    for i, (hh, lane0, cols, n_frames, frame0) in enumerate(tasks):
        if i + 1 < len(tasks):
            scores(i + 1)
        o = _attn_output(s_scr.at[i % 2], head(vt_ref, hh, 0), cols, n_frames, lam, lam_init, g)
        lanes = slice(hh * DA_VD, (hh + 1) * DA_VD)
        if n_frames:
            out_ref[N_META + frame0:N_META + frame0 + cols, lanes] = o.T.astype(BF16)
        else:
            out_ref[0:N_META, lanes] = o.T[0:N_META, :].astype(BF16)


def _attn(z, qt, vt, lam_p, g, layer, bsz):
    lam_init = 0.8 - 0.6 * math.exp(-0.3 * layer)
    m = bsz * L
    hw = ATT_HPS * DA_VD
    kb = Z_DK // hw

    def tspec():
        return pl.BlockSpec((None, hw, ATT_TW), lambda b, h: (b, h, 0))

    return pl.pallas_call(
        functools.partial(_attn_kernel, lam_init),
        grid=(bsz, DA_HEADS // ATT_HPS),
        in_specs=[
            tspec(),
            pl.BlockSpec((L, hw), lambda b, h: (b, kb + h)),
            tspec(),
            pl.BlockSpec((4, DA_HD), lambda b, h: (0, 0)),
            pl.BlockSpec((DA_VD, 1), lambda b, h: (0, 0)),
        ],
        out_specs=pl.BlockSpec((L, hw), lambda b, h: (b, h)),
        out_shape=jax.ShapeDtypeStruct((m, DA_DIM), BF16),
        scratch_shapes=[pltpu.VMEM((2, 2, ATT_TW, ATT_QB), F32)],
        compiler_params=_params(("parallel", "parallel")),
        name="diff_attn",
    )(qt, z, vt, lam_p, g)


MERGE_NC = 512


def _merge_kernel(tm, a_ref, ah_ref, hm_ref, hd_ref, gate_ref, h_ref, cw_ref, wa_ref, wm_ref, wd_ref, wo_ref,
                  g_ref, hn_ref, u_ref, mg_scr):
    ax, ab, ac = (slice(k * CONV_DIM, (k + 1) * CONV_DIM) for k in range(3))
    pos = _pos_in_batch(pl.program_id(0) * tm, tm)
    cur = a_ref[:, ac].astype(F32) * a_ref[:, ax].astype(F32)
    prev = ah_ref[:, ac].astype(F32) * ah_ref[:, ax].astype(F32)
    a_act = (a_ref[:, ab].astype(F32) * _causal_conv3(prev, cur, cw_ref[...], pos)).astype(BF16)
    hm = hm_ref[...]
    hd = hd_ref[...]
    for c0 in range(0, D_MODEL, MERGE_NC):
        cs = slice(c0, c0 + MERGE_NC)
        ga, gm, gd = (gate_ref[:, k * D_MODEL + c0:k * D_MODEL + c0 + MERGE_NC].astype(F32) for k in range(3))
        merged = (_sigmoid(ga) * _dot(a_act, wa_ref[:, cs])
                  + _sigmoid(gm) * _dot(hm, wm_ref[:, cs])
                  + _sigmoid(gd) * _dot(hd, wd_ref[:, cs]))
        mg_scr[:, cs] = merged.astype(BF16)
    hn = h_ref[...] + _dot(mg_scr[...], wo_ref[...])
    hn_ref[...] = hn
    u_ref[...] = _rms(hn, g_ref[...]).astype(BF16)


def _merge(z, hm, hd, h, layer, conv_a, w_br_a, w_br_m, w_br_d, w_out, g, tm=384):
    m = h.shape[0]
    a_w = 3 * CONV_DIM
    ab = Z_AX // a_w
    hb = tm // HALO

    def weight(rows):
        return _resident((None, rows, D_MODEL), lambda i: (layer, 0, 0))

    return pl.pallas_call(
        functools.partial(_merge_kernel, tm),
        grid=(m // tm,),
        in_specs=[
            pl.BlockSpec((tm, a_w), lambda i: (i, ab)),
            pl.BlockSpec((HALO, a_w), lambda i: (jnp.maximum(i * hb - 1, 0), ab)),
            pl.BlockSpec((tm, ML_DIM), lambda i: (i, 0)),
            pl.BlockSpec((tm, DA_DIM), lambda i: (i, 0)),
            pl.BlockSpec((tm, 3 * D_MODEL), lambda i: (i, 0)),
            pl.BlockSpec((tm, D_MODEL), lambda i: (i, 0)),
            _resident((3, CONV_DIM), lambda i: (0, 0)),
            weight(CONV_DIM), weight(ML_DIM), weight(DA_DIM), weight(D_MODEL),
            _resident((1, D_MODEL), lambda i: (0, 0)),
        ],
        out_specs=[
            pl.BlockSpec((tm, D_MODEL), lambda i: (i, 0)),
            pl.BlockSpec((tm, D_MODEL), lambda i: (i, 0)),
        ],
        out_shape=[jax.ShapeDtypeStruct((m, D_MODEL), F32),
                   jax.ShapeDtypeStruct((m, D_MODEL), BF16)],
        scratch_shapes=[pltpu.VMEM((tm, D_MODEL), BF16)],
        compiler_params=_params(("parallel",)),
        name="merge_out_proj",
    )(z, z, hm, hd, z, h, conv_a, w_br_a, w_br_m, w_br_d, w_out, g)


FFN_ROWS = 2


def _row_blocks(tm, n):
    units = tm // HALO
    edges = [HALO * (units * k // n) for k in range(n + 1)]
    return list(zip(edges[:-1], edges[1:]))


def _ffn_up_kernel(tm, u_ref, uh_ref, wa_ref, wb_ref, cw_ref, cb_ref, out_ref, w_scr):
    @pl.when(pl.program_id(1) == 0)
    def _():
        w_scr[0] = wa_ref[...].astype(BF16)
        w_scr[1] = wb_ref[...].astype(BF16)

    r_tile = pl.program_id(1) * tm
    a_prev = _dot(uh_ref[...], w_scr[0])
    for r0, r1 in _row_blocks(tm, FFN_ROWS):
        u = u_ref[r0:r1, :]
        a = _dot(u, w_scr[0])
        conv = _causal_conv3(a_prev, a, cw_ref[...], _pos_in_batch(r_tile + r0, r1 - r0)) + cb_ref[...]
        gelu = 0.5 * conv * (1.0 + lax.erf(conv * (2.0 ** -0.5)))
        out_ref[r0:r1, :] = (gelu * _dot(u, w_scr[1])).astype(BF16)
        a_prev = a[r1 - r0 - HALO:, :]


def _ffn_up(u, w_up, layer, conv_w, conv_b, tm=L, tn=512):
    m = u.shape[0]
    nb = D_FF // tn
    hb = tm // HALO
    return pl.pallas_call(
        functools.partial(_ffn_up_kernel, tm),
        grid=(nb, m // tm),
        in_specs=[
            pl.BlockSpec((tm, D_MODEL), lambda j, i: (i, 0)),
            pl.BlockSpec((HALO, D_MODEL), lambda j, i: (jnp.maximum(i * hb - 1, 0), 0)),
            pl.BlockSpec((None, D_MODEL, tn), lambda j, i: (layer, 0, j)),
            pl.BlockSpec((None, D_MODEL, tn), lambda j, i: (layer, 0, nb + j)),
            pl.BlockSpec((3, tn), lambda j, i: (0, j)),
            pl.BlockSpec((1, tn), lambda j, i: (0, j)),
        ],
        out_specs=pl.BlockSpec((tm, tn), lambda j, i: (i, j)),
        out_shape=jax.ShapeDtypeStruct((m, D_FF), BF16),
        scratch_shapes=[pltpu.VMEM((2, D_MODEL, tn), BF16)],
        compiler_params=_params(("parallel", "arbitrary")),
        name="ffn_up",
    )(u, u, w_up, w_up, conv_w, conv_b)


def _ffn_down_kernel(final, act_ref, w_ref, h_ref, g_ref, *out_refs):
    hn = h_ref[...] + _dot(act_ref[...], w_ref[...])
    y = _rms(hn, g_ref[...])
    if final:
        out_refs[0][...] = y
    else:
        out_refs[0][...] = hn
        out_refs[1][...] = y.astype(BF16)


def _ffn_down(act, w_down, layer, h, g, final):
    m = h.shape[0]
    if final:
        tm = 256
        m = m // L * SEQ
        per_batch = SEQ // tm

        def stream_row(i):
            return pl.multiple_of(i * tm + N_META * (i // per_batch + 1), N_META)

        act_spec = pl.BlockSpec((pl.Element(tm), pl.Element(D_FF)), lambda i: (stream_row(i), 0))
        h_spec = pl.BlockSpec((pl.Element(tm), pl.Element(D_MODEL)), lambda i: (stream_row(i), 0))
        row = pl.BlockSpec((tm, D_MODEL), lambda i: (i, 0))
        out_specs = [row]
        out_shape = [jax.ShapeDtypeStruct((m, D_MODEL), F32)]
    else:
        tm = 384
        row = pl.BlockSpec((tm, D_MODEL), lambda i: (i, 0))
        act_spec = pl.BlockSpec((tm, D_FF), lambda i: (i, 0))
        h_spec = row
        out_specs = [row, row]
        out_shape = [jax.ShapeDtypeStruct((m, D_MODEL), F32),
                     jax.ShapeDtypeStruct((m, D_MODEL), BF16)]
    return pl.pallas_call(
        functools.partial(_ffn_down_kernel, final),
        grid=(m // tm,),
        in_specs=[
            act_spec,
            _resident((None, D_FF, D_MODEL), lambda i: (layer, 0, 0)),
            h_spec,
            _resident((1, D_MODEL), lambda i: (0, 0)),
        ],
        out_specs=out_specs,
        out_shape=out_shape,
        compiler_params=_params(("parallel",)),
        name="ffn_down_final" if final else "ffn_down",
    )(act, w_down, h, g)


def kernel(x, meta, norm_mix, w_in, conv_a, b_if, ml_norm, da_lambda, da_norm, w_br_a, w_br_m, w_br_d,
           w_out, norm_ffn, w_up, conv_ffn, conv_ffn_b, w_down, norm_f):
    bsz, seq, d = x.shape
    assert (seq, d) == (SEQ, D_MODEL)
    depth = w_in.shape[0]
    w_in_t = jnp.swapaxes(w_in, 1, 2).reshape(depth * W_IN, D_MODEL)
    wa, wm, wd, wo, wdn = (w.astype(BF16) for w in (w_br_a, w_br_m, w_br_d, w_out, w_down))
    h, u = _embed(x.reshape(bsz * SEQ, D_MODEL), meta, norm_mix[0][None], bsz)
    for i in range(depth):
        z = _inproj(u, w_in_t, i)
        qt, vt, zif = _attn_proj(u, w_in_t, i, bsz)
        bias = jnp.pad(b_if[i].reshape(1, 2 * ML_HEADS), ((0, 0), (0, IF_W - 2 * ML_HEADS)))
        hm = _mlstm(z, zif, bias, ml_norm[i][None], bsz)
        hd = _attn(z, qt, vt, da_lambda[i], da_norm[i][:, None], i, bsz)
        h, u = _merge(z, hm, hd, h, i, conv_a[i], wa, wm, wd, wo, norm_ffn[i][None])
        act = _ffn_up(u, w_up, i, conv_ffn[i], conv_ffn_b[i][None])
        final = i == depth - 1
        g_next = norm_f if final else norm_mix[i + 1]
        outs = _ffn_down(act, wdn, i, h, g_next[None], final)
        if final:
            y = outs[0]
        else:
            h, u = outs
    return y.reshape(bsz, SEQ, D_MODEL)
```

```python
import functools
import math

import jax
import jax.numpy as jnp
from jax import lax
from jax.experimental import pallas as pl
from jax.experimental.pallas import tpu as pltpu

F32 = jnp.float32
BF16 = jnp.bfloat16

D_MODEL = 2048
SEQ = 2048
N_META = 16
L = N_META + SEQ
CHUNK = 64
EPS = 1e-6

CONV_DIM = 512
ML_HEADS = 4
ML_DK = 256
ML_DV = 256
ML_DIM = ML_HEADS * ML_DV
DA_HEADS = 4
DA_HD = 64
DA_VD = 2 * DA_HD
DA_DIM = DA_HEADS * DA_VD
D_FF = 5632

Z_GATE = 0
Z_AX = 3 * D_MODEL
Z_DK = Z_AX + 3 * CONV_DIM
Z_MQ = Z_DK + DA_DIM
Z_N = Z_MQ + 4 * ML_DIM
IF_W = 128

ML_T = 128
ML_EXT = ML_DV + 128
HALO = 16
ATT_QB = 256
EMB_T = L // 3
VMEM_LIMIT = 56 * 1024 * 1024


def _dot(a, b):
    return jnp.dot(a, b, preferred_element_type=F32)


def _dot_nt(a, b):
    return lax.dot_general(a, b, (((1,), (1,)), ((), ())), preferred_element_type=F32)


def _dot_tn(a, b):
    return lax.dot_general(a, b, (((0,), (0,)), ((), ())), preferred_element_type=F32)


def _bdot(a, b, ca, cb):
    return lax.dot_general(a, b, (((ca,), (cb,)), ((0,), (0,))), preferred_element_type=F32)


def _rms(x, g):
    return x * lax.rsqrt(jnp.mean(x * x, axis=-1, keepdims=True) + EPS) * g


def _sigmoid(x):
    return 1.0 / (1.0 + jnp.exp(-x))


def _pos_in_batch(r0, tm):
    pos = r0 % L + lax.broadcasted_iota(jnp.int32, (tm, 1), 0)
    return jnp.where(pos >= L, pos - L, pos)


def _causal_conv3(prev, cur, w, pos):
    cc = jnp.concatenate([prev, cur], axis=0)
    x1 = jnp.where(pos >= 1, pltpu.roll(cc, 1, 0)[HALO:], 0.0)
    x2 = jnp.where(pos >= 2, pltpu.roll(cc, 2, 0)[HALO:], 0.0)
    return w[0:1] * x2 + w[1:2] * x1 + w[2:3] * cur


def _params(sem):
    return pltpu.CompilerParams(dimension_semantics=sem, vmem_limit_bytes=VMEM_LIMIT)


def _resident(shape, index_map):
    return pl.BlockSpec(shape, index_map, pipeline_mode=pl.Buffered(1))


def _embed_kernel(x_ref, meta_ref, g_ref, h_ref, u_ref):
    j = pl.program_id(1)

    @pl.when(j == 0)
    def _():
        rows = jnp.concatenate([meta_ref[...], x_ref[0:EMB_T - N_META, :]], axis=0)
        h_ref[...] = rows
        u_ref[...] = _rms(rows, g_ref[...]).astype(BF16)

    @pl.when(j > 0)
    def _():
        rows = x_ref[...]
        h_ref[...] = rows
        u_ref[...] = _rms(rows, g_ref[...]).astype(BF16)


def _embed(x2d, meta, g, bsz):
    nb = L // EMB_T
    m = bsz * L
    return pl.pallas_call(
        _embed_kernel,
        grid=(bsz, nb),
        in_specs=[
            pl.BlockSpec((pl.Element(EMB_T), pl.Element(D_MODEL)),
                         lambda b, j: (pl.multiple_of(b * SEQ + jnp.maximum(j * EMB_T - N_META, 0), 16), 0)),
            pl.BlockSpec((N_META, D_MODEL), lambda b, j: (0, 0)),
            pl.BlockSpec((1, D_MODEL), lambda b, j: (0, 0)),
        ],
        out_specs=[
            pl.BlockSpec((EMB_T, D_MODEL), lambda b, j: (b * nb + j, 0)),
            pl.BlockSpec((EMB_T, D_MODEL), lambda b, j: (b * nb + j, 0)),
        ],
        out_shape=[jax.ShapeDtypeStruct((m, D_MODEL), F32),
                   jax.ShapeDtypeStruct((m, D_MODEL), BF16)],
        compiler_params=_params(("parallel", "arbitrary")),
        name="embed_norm",
    )(x2d, meta, g)


W_A_END = 3 * CONV_DIM
W_M_END = W_A_END + 4 * ML_DIM
W_IF_END = W_M_END + 2 * ML_HEADS
W_D_END = W_IF_END + 3 * DA_DIM
W_IN = W_D_END + 3 * D_MODEL


IN_SRC = 512
IN_NSRC = 2


def _inproj_kernel(u_ref, *refs):
    wt_refs, z_ref = refs[:IN_NSRC], refs[IN_NSRC]
    u = u_ref[...]
    for k, wt_ref in enumerate(wt_refs):
        z_ref[:, k * IN_SRC:(k + 1) * IN_SRC] = _dot_nt(u, wt_ref[...].astype(BF16)).astype(BF16)


def _inproj(u, w_in_t, layer, tm=L):
    m = u.shape[0]
    tn = IN_NSRC * IN_SRC
    n_gate, n_a = 3 * D_MODEL // IN_SRC, 3 * CONV_DIM // IN_SRC

    def src_row(t):
        gate = W_D_END + IN_SRC * t
        cnv = IN_SRC * (t - n_gate)
        att = W_IF_END + DA_DIM + IN_SRC * (t - n_gate - n_a)
        mls = W_A_END + IN_SRC * (t - n_gate - n_a - 1)
        row = jnp.where(t < n_gate, gate,
                        jnp.where(t < n_gate + n_a, cnv, jnp.where(t < n_gate + n_a + 1, att, mls)))
        return pl.multiple_of(layer * W_IN + row, 8)

    def src_spec(k):
        return pl.BlockSpec((pl.Element(IN_SRC), pl.Element(D_MODEL)),
                            lambda i, j: (src_row(IN_NSRC * j + k), 0))

    return pl.pallas_call(
        _inproj_kernel,
        grid=(m // tm, Z_N // tn),
        in_specs=[pl.BlockSpec((tm, D_MODEL), lambda i, j: (i, 0))] + [src_spec(k) for k in range(IN_NSRC)],
        out_specs=pl.BlockSpec((tm, tn), lambda i, j: (i, j)),
        out_shape=jax.ShapeDtypeStruct((m, Z_N), BF16),
        compiler_params=_params(("parallel", "arbitrary")),
        name="in_proj",
    )(u, *([w_in_t] * IN_NSRC))


ML_NCH = -(-L // ML_T)
ML_TAIL_VALID = ML_T - (L - (ML_NCH - 1) * ML_T)
ML_UNROLL = 2


def _mlstm_gates(if_ref, bias_ref, ab_scr, cm_scr):
    t = ML_T
    ninf = -jnp.inf
    bias = bias_ref[...]
    starts = [c * t for c in range(ML_NCH - 1)] + [L - t]
    gt = jnp.concatenate([(if_ref[r0:r0 + t, :] + bias).T[0:8, :] for r0 in starts], axis=0)
    shape = (8 * ML_NCH, t)
    lane = lax.broadcasted_iota(jnp.int32, shape, 1)
    row = lax.broadcasted_iota(jnp.int32, shape, 0)
    is_i = (row & 7) < ML_HEADS
    log_sig = jnp.minimum(gt, 0.0) - jnp.log1p(jnp.exp(-jnp.abs(gt)))
    lg = jnp.where(is_i, gt, log_sig)
    weightless = jnp.logical_and(row >= 8 * (ML_NCH - 1), lane < ML_TAIL_VALID)
    lg = jnp.where(weightless, jnp.where(is_i, ninf, 0.0), lg)
    cs = jnp.where(is_i, 0.0, lg)
    sh = 1
    while sh < t:
        cs = cs + jnp.where(lane >= sh, pltpu.roll(cs, sh, 1), 0.0)
        sh *= 2
    a = jnp.where(is_i, lg - pltpu.roll(cs, 8 * ML_NCH - ML_HEADS, 0), ninf)
    cm = a
    sh = 1
    while sh < t:
        cm = jnp.maximum(cm, jnp.where(lane >= sh, pltpu.roll(cm, sh, 1), ninf))
        sh *= 2
    ab_scr[...] = jnp.where(is_i, a, cs)
    cm_scr[...] = cm


def _mlstm_chunk(c, r0, first_valid, q_ref, k_ref, v_ref, o_ref, g_ref, out_ref, s_ref, m_ref, ab_scr, cm_scr):
    t = ML_T
    rows = pl.ds(r0, t)
    row0 = c * 8 if isinstance(c, int) else pl.multiple_of(c * 8, 8)
    ab = ab_scr[pl.ds(row0, 8), :]
    cmr = cm_scr[pl.ds(row0, 8), :]

    tri = lax.broadcasted_iota(jnp.int32, (t, t), 0) >= lax.broadcasted_iota(jnp.int32, (t, t), 1)
    nh = ML_HEADS

    def heads(x, d):
        return jnp.stack([x[:, h * d:(h + 1) * d] for h in range(nh)], axis=0)

    def row_of(x, r):
        return jnp.stack([x[r + h:r + h + 1, :] for h in range(nh)], axis=0)

    def lane_rep(x, r):
        return jnp.stack([jnp.broadcast_to(x[r + h:r + h + 1, :], (t, t)).T for h in range(nh)], axis=0)

    def widen(x, n):
        return jnp.concatenate([x] * n, axis=2)

    q = heads(q_ref[rows, :], ML_DK)
    k = heads(k_ref[rows, :], ML_DK) * jnp.asarray(ML_DK ** -0.5, BF16)
    vext = jnp.concatenate([heads(v_ref[rows, :], ML_DV), jnp.ones((nh, t, ML_EXT - ML_DV), BF16)], axis=2)
    a_r = row_of(ab, 0)
    a_l, b_l, cm_l = lane_rep(ab, 0), lane_rep(ab, nh), lane_rep(cmr, 0)
    m_prev = m_ref[...][:, :, 0:1]

    m_l = jnp.maximum(cm_l, m_prev)
    w = jnp.where(tri[None], jnp.exp(a_r - m_l), 0.0)
    g_l = jnp.exp(m_prev - m_l)
    s = _bdot(q, k, 2, 2) * w
    tot = widen(g_l, 3) * _bdot(q, s_ref[...].astype(BF16), 2, 1) + _bdot(s.astype(BF16), vext, 2, 1)
    scale_l = 1.0 / jnp.maximum(jnp.abs(tot[:, :, ML_DV:]), jnp.exp(-(b_l + m_l)))
    hh = tot[:, :, :ML_DV] * widen(scale_l, 2)
    ssq_l = _bdot((hh * hh).astype(BF16), jnp.ones((nh, ML_DV, 128), BF16), 2, 1)
    hn = hh * widen(lax.rsqrt(ssq_l * (1.0 / ML_DV) + EPS), 2)
    for h in range(nh):
        sl = slice(h * ML_DV, (h + 1) * ML_DV)
        res = (hn[h] * g_ref[:, sl] * _sigmoid(o_ref[rows, sl].astype(F32))).astype(BF16)
        if first_valid:
            out_ref[pl.ds(r0 + first_valid, t - first_valid), sl] = res[first_valid:, :]
        else:
            out_ref[rows, sl] = res

    b_last = row_of(ab, nh)[:, :, t - 1:t]
    m_x = jnp.maximum(m_prev, row_of(cmr, 0)[:, :, t - 1:t])
    vw = (vext.astype(F32) * widen(jnp.exp(a_l - m_x), 3)).astype(BF16)
    s_ref[...] = jnp.exp(m_prev - m_x) * s_ref[...] + _bdot(k, vw, 1, 1)
    m_ref[...] = jnp.broadcast_to(b_last + m_x, (nh, 1, 128))


def _mlstm_kernel(q_ref, k_ref, v_ref, o_ref, if_ref, bias_ref, g_ref, out_ref, s_ref, m_ref, ab_scr, cm_scr):
    s_ref[...] = jnp.zeros_like(s_ref)
    m_ref[...] = jnp.zeros_like(m_ref)
    _mlstm_gates(if_ref, bias_ref, ab_scr, cm_scr)
    refs = (q_ref, k_ref, v_ref, o_ref, g_ref, out_ref, s_ref, m_ref, ab_scr, cm_scr)

    def body(i, carry):
        for k in range(ML_UNROLL):
            c = ML_UNROLL * i + k
            _mlstm_chunk(c, pl.multiple_of(c * ML_T, ML_T), 0, *refs)
        return carry

    lax.fori_loop(0, (ML_NCH - 1) // ML_UNROLL, body, 0)
    _mlstm_chunk(ML_NCH - 1, L - ML_T, ML_TAIL_VALID, *refs)


def _mlstm(z, zif, bias, g, bsz):
    m = bsz * L
    qb = Z_MQ // ML_DIM

    def zspec(off):
        return pl.BlockSpec((L, ML_DIM), lambda b: (b, qb + off))

    return pl.pallas_call(
        _mlstm_kernel,
        grid=(bsz,),
        in_specs=[
            zspec(0), zspec(1), zspec(2), zspec(3),
            pl.BlockSpec((L, IF_W), lambda b: (b, 0)),
            pl.BlockSpec((1, IF_W), lambda b: (0, 0)),
            pl.BlockSpec((1, ML_DIM), lambda b: (0, 0)),
        ],
        out_specs=pl.BlockSpec((L, ML_DIM), lambda b: (b, 0)),
        out_shape=jax.ShapeDtypeStruct((m, ML_DIM), BF16),
        scratch_shapes=[pltpu.VMEM((ML_HEADS, ML_DK, ML_EXT), F32),
                        pltpu.VMEM((ML_HEADS, 1, 128), F32),
                        pltpu.VMEM((8 * ML_NCH, ML_T), F32),
                        pltpu.VMEM((8 * ML_NCH, ML_T), F32)],
        compiler_params=_params(("parallel",)),
        name="mlstm",
    )(z, z, z, z, zif, bias, g)


ATT_MB = 128
ATT_TW = SEQ + ATT_MB
ATT_HPS = 2


def _attn_proj_kernel(u_ref, wq_ref, wv_ref, wif_ref, qt_ref, vt_ref, zif_ref):
    u_fr = u_ref[N_META:, :]
    u_meta = u_ref[0:ATT_MB, :]
    scale = DA_HD ** -0.5 * math.log2(math.e)
    for w_ref, o_ref, mul in ((wq_ref, qt_ref, scale), (wv_ref, vt_ref, 1.0)):
        wt = w_ref[...].astype(BF16)
        o_ref[:, 0:SEQ] = (_dot_nt(wt, u_fr) * mul).astype(BF16)
        o_ref[:, SEQ:ATT_TW] = (_dot_nt(wt, u_meta) * mul).astype(BF16)
    zif_ref[...] = _dot_nt(u_ref[...], wif_ref[...].astype(BF16))


def _attn_proj(u, w_in_t, layer, bsz):
    def wspec(rows, row0):
        return _resident((pl.Element(rows), pl.Element(D_MODEL)), lambda b: (layer * W_IN + row0, 0))

    out = pl.BlockSpec((None, DA_DIM, ATT_TW), lambda b: (b, 0, 0))
    shape = jax.ShapeDtypeStruct((bsz, DA_DIM, ATT_TW), BF16)
    return pl.pallas_call(
        _attn_proj_kernel,
        grid=(bsz,),
        in_specs=[pl.BlockSpec((L, D_MODEL), lambda b: (b, 0)),
                  wspec(DA_DIM, W_IF_END), wspec(DA_DIM, W_IF_END + 2 * DA_DIM), wspec(IF_W, W_M_END)],
        out_specs=[out, out, pl.BlockSpec((L, IF_W), lambda b: (b, 0))],
        out_shape=[shape, shape, jax.ShapeDtypeStruct((bsz * L, IF_W), F32)],
        compiler_params=_params(("parallel",)),
        name="attn_qv_proj",
    )(u, w_in_t, w_in_t, w_in_t)


def _attn_scores(qt, k_ref, s_ref, n_frames, frame0):
    cols = qt.shape[1]
    ninf = -jnp.inf
    first_map = lax.broadcasted_iota(jnp.int32, (DA_VD, cols), 0) < DA_HD
    k_meta = k_ref[0:ATT_MB, :]
    meta_ok = lax.broadcasted_iota(jnp.int32, (ATT_MB, 1), 0) < N_META
    if n_frames:
        k_fr = k_ref[N_META:N_META + n_frames, :]
        qf = frame0 + lax.broadcasted_iota(jnp.int32, (1, cols), 1)
        kend = (qf // CHUNK + 1) * CHUNK
        kf = frame0 + lax.broadcasted_iota(jnp.int32, (n_frames - frame0, 1), 0)
        diag_ok = kf < kend
    for mp in range(2):
        qm = jnp.where(first_map if mp == 0 else jnp.logical_not(first_map), qt, jnp.zeros_like(qt))
        s_ref[mp, 0:ATT_MB, 0:cols] = jnp.where(meta_ok, _dot(k_meta, qm), ninf)
        if n_frames:
            s_fr = _dot(k_fr, qm)
            if frame0:
                s_ref[mp, ATT_MB:ATT_MB + frame0, 0:cols] = s_fr[:frame0, :]
            s_ref[mp, ATT_MB + frame0:ATT_MB + n_frames, 0:cols] = jnp.where(diag_ok, s_fr[frame0:, :], ninf)


def _attn_output(s_ref, vt_ref, cols, n_frames, lam, lam_init, g):
    nk = ATT_MB + n_frames
    probs = []
    for mp in range(2):
        s = s_ref[mp, 0:nk, 0:cols]
        p = jnp.exp2(s - jnp.max(s, axis=0, keepdims=True))
        probs.append((p, jnp.sum(p, axis=0, keepdims=True)))
    inv_l1 = 1.0 / probs[0][1]
    a = (probs[0][0] - probs[1][0] * (lam * probs[0][1] / probs[1][1])).astype(BF16)
    o = _dot(vt_ref[:, SEQ:ATT_TW], a[:ATT_MB, :])
    if n_frames:
        o = o + _dot(vt_ref[:, 0:n_frames], a[ATT_MB:, :])
    o = o * inv_l1
    o = o * lax.rsqrt(jnp.mean(o * o, axis=0, keepdims=True) + EPS)
    return o * g * (1.0 - lam_init)


def _attn_kernel(lam_init, qt_ref, k_ref, vt_ref, lam_ref, g_ref, out_ref, s_scr):
    lf = lam_ref[...]
    lam = (jnp.exp(jnp.sum(lf[0:1] * lf[1:2], axis=1, keepdims=True))
           - jnp.exp(jnp.sum(lf[2:3] * lf[3:4], axis=1, keepdims=True)) + lam_init)
    g = g_ref[...]
    blocks = [(SEQ, ATT_MB, 0, 0)] + [(f0, ATT_QB, f0 + ATT_QB, f0) for f0 in range(0, SEQ, ATT_QB)]
    tasks = [(hh,) + blk for hh in range(ATT_HPS) for blk in blocks]

    def head(ref, hh, axis):
        sl = pl.ds(hh * DA_VD, DA_VD)
        return ref.at[sl, :] if axis == 0 else ref.at[:, sl]

    def scores(i):
        hh, lane0, cols, n_frames, frame0 = tasks[i]
        qt = qt_ref[hh * DA_VD:(hh + 1) * DA_VD, lane0:lane0 + cols]
        _attn_scores(qt, head(k_ref, hh, 1), s_scr.at[i % 2], n_frames, frame0)

    scores(0)
    for i, (hh, lane0, cols, n_frames, frame0) in enumerate(tasks):
        if i + 1 < len(tasks):
            scores(i + 1)
        o = _attn_output(s_scr.at[i % 2], head(vt_ref, hh, 0), cols, n_frames, lam, lam_init, g)
        lanes = slice(hh * DA_VD, (hh + 1) * DA_VD)
        if n_frames:
            out_ref[N_META + frame0:N_META + frame0 + cols, lanes] = o.T.astype(BF16)
        else:
            out_ref[0:N_META, lanes] = o.T[0:N_META, :].astype(BF16)


def _attn(z, qt, vt, lam_p, g, layer, bsz):
    lam_init = 0.8 - 0.6 * math.exp(-0.3 * layer)
    m = bsz * L
    hw = ATT_HPS * DA_VD
    kb = Z_DK // hw

    def tspec():
        return pl.BlockSpec((None, hw, ATT_TW), lambda b, h: (b, h, 0))

    return pl.pallas_call(
        functools.partial(_attn_kernel, lam_init),
        grid=(bsz, DA_HEADS // ATT_HPS),
        in_specs=[
            tspec(),
            pl.BlockSpec((L, hw), lambda b, h: (b, kb + h)),
            tspec(),
            pl.BlockSpec((4, DA_HD), lambda b, h: (0, 0)),
            pl.BlockSpec((DA_VD, 1), lambda b, h: (0, 0)),
        ],
        out_specs=pl.BlockSpec((L, hw), lambda b, h: (b, h)),
        out_shape=jax.ShapeDtypeStruct((m, DA_DIM), BF16),
        scratch_shapes=[pltpu.VMEM((2, 2, ATT_TW, ATT_QB), F32)],
        compiler_params=_params(("parallel", "parallel")),
        name="diff_attn",
    )(qt, z, vt, lam_p, g)


MERGE_NC = 512


def _merge_kernel(tm, a_ref, ah_ref, hm_ref, hd_ref, gate_ref, h_ref, cw_ref, wa_ref, wm_ref, wd_ref, wo_ref,
                  g_ref, hn_ref, u_ref, mg_scr):
    ax, ab, ac = (slice(k * CONV_DIM, (k + 1) * CONV_DIM) for k in range(3))
    pos = _pos_in_batch(pl.program_id(0) * tm, tm)
    cur = a_ref[:, ac].astype(F32) * a_ref[:, ax].astype(F32)
    prev = ah_ref[:, ac].astype(F32) * ah_ref[:, ax].astype(F32)
    a_act = (a_ref[:, ab].astype(F32) * _causal_conv3(prev, cur, cw_ref[...], pos)).astype(BF16)
    hm = hm_ref[...]
    hd = hd_ref[...]
    for c0 in range(0, D_MODEL, MERGE_NC):
        cs = slice(c0, c0 + MERGE_NC)
        ga, gm, gd = (gate_ref[:, k * D_MODEL + c0:k * D_MODEL + c0 + MERGE_NC].astype(F32) for k in range(3))
        merged = (_sigmoid(ga) * _dot(a_act, wa_ref[:, cs])
                  + _sigmoid(gm) * _dot(hm, wm_ref[:, cs])
                  + _sigmoid(gd) * _dot(hd, wd_ref[:, cs]))
        mg_scr[:, cs] = merged.astype(BF16)
    hn = h_ref[...] + _dot(mg_scr[...], wo_ref[...])
    hn_ref[...] = hn
    u_ref[...] = _rms(hn, g_ref[...]).astype(BF16)


def _merge(z, hm, hd, h, layer, conv_a, w_br_a, w_br_m, w_br_d, w_out, g, tm=384):
    m = h.shape[0]
    a_w = 3 * CONV_DIM
    ab = Z_AX // a_w
    hb = tm // HALO

    def weight(rows):
        return _resident((None, rows, D_MODEL), lambda i: (layer, 0, 0))

    return pl.pallas_call(
        functools.partial(_merge_kernel, tm),
        grid=(m // tm,),
        in_specs=[
            pl.BlockSpec((tm, a_w), lambda i: (i, ab)),
            pl.BlockSpec((HALO, a_w), lambda i: (jnp.maximum(i * hb - 1, 0), ab)),
            pl.BlockSpec((tm, ML_DIM), lambda i: (i, 0)),
            pl.BlockSpec((tm, DA_DIM), lambda i: (i, 0)),
            pl.BlockSpec((tm, 3 * D_MODEL), lambda i: (i, 0)),
            pl.BlockSpec((tm, D_MODEL), lambda i: (i, 0)),
            _resident((3, CONV_DIM), lambda i: (0, 0)),
            weight(CONV_DIM), weight(ML_DIM), weight(DA_DIM), weight(D_MODEL),
            _resident((1, D_MODEL), lambda i: (0, 0)),
        ],
        out_specs=[
            pl.BlockSpec((tm, D_MODEL), lambda i: (i, 0)),
            pl.BlockSpec((tm, D_MODEL), lambda i: (i, 0)),
        ],
        out_shape=[jax.ShapeDtypeStruct((m, D_MODEL), F32),
                   jax.ShapeDtypeStruct((m, D_MODEL), BF16)],
        scratch_shapes=[pltpu.VMEM((tm, D_MODEL), BF16)],
        compiler_params=_params(("parallel",)),
        name="merge_out_proj",
    )(z, z, hm, hd, z, h, conv_a, w_br_a, w_br_m, w_br_d, w_out, g)


FFN_ROWS = 2


def _row_blocks(tm, n):
    units = tm // HALO
    edges = [HALO * (units * k // n) for k in range(n + 1)]
    return list(zip(edges[:-1], edges[1:]))


def _ffn_up_kernel(tm, u_ref, uh_ref, wa_ref, wb_ref, cw_ref, cb_ref, out_ref, w_scr):
    @pl.when(pl.program_id(1) == 0)
    def _():
        w_scr[0] = wa_ref[...].astype(BF16)
        w_scr[1] = wb_ref[...].astype(BF16)

    r_tile = pl.program_id(1) * tm
    a_prev = _dot(uh_ref[...], w_scr[0])
    for r0, r1 in _row_blocks(tm, FFN_ROWS):
        u = u_ref[r0:r1, :]
        a = _dot(u, w_scr[0])
        conv = _causal_conv3(a_prev, a, cw_ref[...], _pos_in_batch(r_tile + r0, r1 - r0)) + cb_ref[...]
        gelu = 0.5 * conv * (1.0 + lax.erf(conv * (2.0 ** -0.5)))
        out_ref[r0:r1, :] = (gelu * _dot(u, w_scr[1])).astype(BF16)
        a_prev = a[r1 - r0 - HALO:, :]


def _ffn_up(u, w_up, layer, conv_w, conv_b, tm=L, tn=512):
    m = u.shape[0]
    nb = D_FF // tn
    hb = tm // HALO
    return pl.pallas_call(
        functools.partial(_ffn_up_kernel, tm),
        grid=(nb, m // tm),
        in_specs=[
            pl.BlockSpec((tm, D_MODEL), lambda j, i: (i, 0)),
            pl.BlockSpec((HALO, D_MODEL), lambda j, i: (jnp.maximum(i * hb - 1, 0), 0)),
            pl.BlockSpec((None, D_MODEL, tn), lambda j, i: (layer, 0, j)),
            pl.BlockSpec((None, D_MODEL, tn), lambda j, i: (layer, 0, nb + j)),
            pl.BlockSpec((3, tn), lambda j, i: (0, j)),
            pl.BlockSpec((1, tn), lambda j, i: (0, j)),
        ],
        out_specs=pl.BlockSpec((tm, tn), lambda j, i: (i, j)),
        out_shape=jax.ShapeDtypeStruct((m, D_FF), BF16),
        scratch_shapes=[pltpu.VMEM((2, D_MODEL, tn), BF16)],
        compiler_params=_params(("parallel", "arbitrary")),
        name="ffn_up",
    )(u, u, w_up, w_up, conv_w, conv_b)


def _ffn_down_kernel(final, act_ref, w_ref, h_ref, g_ref, *out_refs):
    hn = h_ref[...] + _dot(act_ref[...], w_ref[...])
    y = _rms(hn, g_ref[...])
    if final:
        out_refs[0][...] = y
    else:
        out_refs[0][...] = hn
        out_refs[1][...] = y.astype(BF16)


def _ffn_down(act, w_down, layer, h, g, final):
    m = h.shape[0]
    if final:
        tm = 256
        m = m // L * SEQ
        per_batch = SEQ // tm

        def stream_row(i):
            return pl.multiple_of(i * tm + N_META * (i // per_batch + 1), N_META)

        act_spec = pl.BlockSpec((pl.Element(tm), pl.Element(D_FF)), lambda i: (stream_row(i), 0))
        h_spec = pl.BlockSpec((pl.Element(tm), pl.Element(D_MODEL)), lambda i: (stream_row(i), 0))
        row = pl.BlockSpec((tm, D_MODEL), lambda i: (i, 0))
        out_specs = [row]
        out_shape = [jax.ShapeDtypeStruct((m, D_MODEL), F32)]
    else:
        tm = 384
        row = pl.BlockSpec((tm, D_MODEL), lambda i: (i, 0))
        act_spec = pl.BlockSpec((tm, D_FF), lambda i: (i, 0))
        h_spec = row
        out_specs = [row, row]
        out_shape = [jax.ShapeDtypeStruct((m, D_MODEL), F32),
                     jax.ShapeDtypeStruct((m, D_MODEL), BF16)]
    return pl.pallas_call(
        functools.partial(_ffn_down_kernel, final),
        grid=(m // tm,),
        in_specs=[
            act_spec,
            _resident((None, D_FF, D_MODEL), lambda i: (layer, 0, 0)),
            h_spec,
            _resident((1, D_MODEL), lambda i: (0, 0)),
        ],
        out_specs=out_specs,
        out_shape=out_shape,
        compiler_params=_params(("parallel",)),
        name="ffn_down_final" if final else "ffn_down",
    )(act, w_down, h, g)


def kernel(x, meta, norm_mix, w_in, conv_a, b_if, ml_norm, da_lambda, da_norm, w_br_a, w_br_m, w_br_d,
           w_out, norm_ffn, w_up, conv_ffn, conv_ffn_b, w_down, norm_f):
    bsz, seq, d = x.shape
    assert (seq, d) == (SEQ, D_MODEL)
    depth = w_in.shape[0]
    w_in_t = jnp.swapaxes(w_in, 1, 2).reshape(depth * W_IN, D_MODEL)
    wa, wm, wd, wo, wdn = (w.astype(BF16) for w in (w_br_a, w_br_m, w_br_d, w_out, w_down))
    h, u = _embed(x.reshape(bsz * SEQ, D_MODEL), meta, norm_mix[0][None], bsz)
    for i in range(depth):
        z = _inproj(u, w_in_t, i)
        qt, vt, zif = _attn_proj(u, w_in_t, i, bsz)
        bias = jnp.pad(b_if[i].reshape(1, 2 * ML_HEADS), ((0, 0), (0, IF_W - 2 * ML_HEADS)))
        hm = _mlstm(z, zif, bias, ml_norm[i][None], bsz)
        hd = _attn(z, qt, vt, da_lambda[i], da_norm[i][:, None], i, bsz)
        h, u = _merge(z, hm, hd, h, i, conv_a[i], wa, wm, wd, wo, norm_ffn[i][None])
        act = _ffn_up(u, w_up, i, conv_ffn[i], conv_ffn_b[i][None])
        final = i == depth - 1
        g_next = norm_f if final else norm_mix[i + 1]
        outs = _ffn_down(act, wdn, i, h, g_next[None], final)
        if final:
            y = outs[0]
        else:
            h, u = outs
    return y.reshape(bsz, SEQ, D_MODEL)
```

```python
import functools
import math

import jax
import jax.numpy as jnp
from jax import lax
from jax.experimental import pallas as pl
from jax.experimental.pallas import tpu as pltpu

F32 = jnp.float32
BF16 = jnp.bfloat16

D_MODEL = 2048
SEQ = 2048
N_META = 16
L = N_META + SEQ
CHUNK = 64
EPS = 1e-6

CONV_DIM = 512
ML_HEADS = 4
ML_DK = 256
ML_DV = 256
ML_DIM = ML_HEADS * ML_DV
DA_HEADS = 4
DA_HD = 64
DA_VD = 2 * DA_HD
DA_DIM = DA_HEADS * DA_VD
D_FF = 5632

Z_GATE = 0
Z_AX = 3 * D_MODEL
Z_DK = Z_AX + 3 * CONV_DIM
Z_MQ = Z_DK + DA_DIM
Z_N = Z_MQ + 4 * ML_DIM

ML_T = 128
ML_EXT = ML_DV + 128
HALO = 16
ATT_QB = 256
EMB_T = L // 3
VMEM_LIMIT = 56 * 1024 * 1024


def _dot(a, b):
    return jnp.dot(a, b, preferred_element_type=F32)


def _dot_nt(a, b):
    return lax.dot_general(a, b, (((1,), (1,)), ((), ())), preferred_element_type=F32)


def _dot_tn(a, b):
    return lax.dot_general(a, b, (((0,), (0,)), ((), ())), preferred_element_type=F32)


def _bdot(a, b, ca, cb):
    return lax.dot_general(a, b, (((ca,), (cb,)), ((0,), (0,))), preferred_element_type=F32)


def _rms(x, g):
    return x * lax.rsqrt(jnp.mean(x * x, axis=-1, keepdims=True) + EPS) * g


def _sigmoid(x):
    return 1.0 / (1.0 + jnp.exp(-x))


def _pos_in_batch(r0, tm):
    pos = r0 % L + lax.broadcasted_iota(jnp.int32, (tm, 1), 0)
    return jnp.where(pos >= L, pos - L, pos)


def _causal_conv3(prev, cur, w, pos):
    cc = jnp.concatenate([prev, cur], axis=0)
    x1 = jnp.where(pos >= 1, pltpu.roll(cc, 1, 0)[HALO:], 0.0)
    x2 = jnp.where(pos >= 2, pltpu.roll(cc, 2, 0)[HALO:], 0.0)
    return w[0:1] * x2 + w[1:2] * x1 + w[2:3] * cur


def _params(sem):
    return pltpu.CompilerParams(dimension_semantics=sem, vmem_limit_bytes=VMEM_LIMIT)


def _resident(shape, index_map):
    return pl.BlockSpec(shape, index_map, pipeline_mode=pl.Buffered(1))


def _embed_kernel(x_ref, meta_ref, g_ref, h_ref, u_ref):
    j = pl.program_id(1)

    @pl.when(j == 0)
    def _():
        rows = jnp.concatenate([meta_ref[...], x_ref[0:EMB_T - N_META, :]], axis=0)
        h_ref[...] = rows
        u_ref[...] = _rms(rows, g_ref[...]).astype(BF16)

    @pl.when(j > 0)
    def _():
        rows = x_ref[...]
        h_ref[...] = rows
        u_ref[...] = _rms(rows, g_ref[...]).astype(BF16)


def _embed(x2d, meta, g, bsz):
    nb = L // EMB_T
    m = bsz * L
    return pl.pallas_call(
        _embed_kernel,
        grid=(bsz, nb),
        in_specs=[
            pl.BlockSpec((pl.Element(EMB_T), pl.Element(D_MODEL)),
                         lambda b, j: (pl.multiple_of(b * SEQ + jnp.maximum(j * EMB_T - N_META, 0), 16), 0)),
            pl.BlockSpec((N_META, D_MODEL), lambda b, j: (0, 0)),
            pl.BlockSpec((1, D_MODEL), lambda b, j: (0, 0)),
        ],
        out_specs=[
            pl.BlockSpec((EMB_T, D_MODEL), lambda b, j: (b * nb + j, 0)),
            pl.BlockSpec((EMB_T, D_MODEL), lambda b, j: (b * nb + j, 0)),
        ],
        out_shape=[jax.ShapeDtypeStruct((m, D_MODEL), F32),
                   jax.ShapeDtypeStruct((m, D_MODEL), BF16)],
        compiler_params=_params(("parallel", "arbitrary")),
        name="embed_norm",
    )(x2d, meta, g)


W_A_END = 3 * CONV_DIM
W_M_END = W_A_END + 4 * ML_DIM
W_IF_END = W_M_END + 2 * ML_HEADS
W_D_END = W_IF_END + 3 * DA_DIM
W_IN = W_D_END + 3 * D_MODEL


IN_SRC = 512
IN_NSRC = 2


def _inproj_kernel(u_ref, *refs):
    wt_refs, z_ref = refs[:IN_NSRC], refs[IN_NSRC]
    u = u_ref[...]
    for k, wt_ref in enumerate(wt_refs):
        z_ref[:, k * IN_SRC:(k + 1) * IN_SRC] = _dot_nt(u, wt_ref[...].astype(BF16)).astype(BF16)


def _inproj(u, w_in_t, layer, tm=L):
    m = u.shape[0]
    tn = IN_NSRC * IN_SRC
    n_gate, n_a = 3 * D_MODEL // IN_SRC, 3 * CONV_DIM // IN_SRC

    def src_row(t):
        gate = W_D_END + IN_SRC * t
        cnv = IN_SRC * (t - n_gate)
        att = W_IF_END + DA_DIM + IN_SRC * (t - n_gate - n_a)
        mls = W_A_END + IN_SRC * (t - n_gate - n_a - 1)
        row = jnp.where(t < n_gate, gate,
                        jnp.where(t < n_gate + n_a, cnv, jnp.where(t < n_gate + n_a + 1, att, mls)))
        return pl.multiple_of(layer * W_IN + row, 8)

    def src_spec(k):
        return pl.BlockSpec((pl.Element(IN_SRC), pl.Element(D_MODEL)),
                            lambda i, j: (src_row(IN_NSRC * j + k), 0))

    return pl.pallas_call(
        _inproj_kernel,
        grid=(m // tm, Z_N // tn),
        in_specs=[pl.BlockSpec((tm, D_MODEL), lambda i, j: (i, 0))] + [src_spec(k) for k in range(IN_NSRC)],
        out_specs=pl.BlockSpec((tm, tn), lambda i, j: (i, j)),
        out_shape=jax.ShapeDtypeStruct((m, Z_N), BF16),
        compiler_params=_params(("parallel", "arbitrary")),
        name="in_proj",
    )(u, *([w_in_t] * IN_NSRC))


ML_NCH = 1 + SEQ // ML_T
ML_UNROLL = 2


def _mlstm_gates(ift_ref, bias_ref, ab_scr, cm_scr):
    t = ML_T
    ninf = -jnp.inf
    bias = bias_ref[...]
    starts = [SEQ] + [c * t for c in range(ML_NCH - 1)]
    gt = jnp.concatenate([ift_ref[0:8, l0:l0 + t] + bias for l0 in starts], axis=0)
    shape = (8 * ML_NCH, t)
    lane = lax.broadcasted_iota(jnp.int32, shape, 1)
    row = lax.broadcasted_iota(jnp.int32, shape, 0)
    is_i = (row & 7) < ML_HEADS
    log_sig = jnp.minimum(gt, 0.0) - jnp.log1p(jnp.exp(-jnp.abs(gt)))
    lg = jnp.where(is_i, gt, log_sig)
    weightless = jnp.logical_and(row < 8, lane >= N_META)
    lg = jnp.where(weightless, jnp.where(is_i, ninf, 0.0), lg)
    cs = jnp.where(is_i, 0.0, lg)
    sh = 1
    while sh < t:
        cs = cs + jnp.where(lane >= sh, pltpu.roll(cs, sh, 1), 0.0)
        sh *= 2
    a = jnp.where(is_i, lg - pltpu.roll(cs, 8 * ML_NCH - ML_HEADS, 0), ninf)
    cm = a
    sh = 1
    while sh < t:
        cm = jnp.maximum(cm, jnp.where(lane >= sh, pltpu.roll(cm, sh, 1), ninf))
        sh *= 2
    ab_scr[...] = jnp.where(is_i, a, cs)
    cm_scr[...] = cm


def _mlstm_chunk(c, r0, n_store, q_ref, k_ref, v_ref, o_ref, g_ref, out_ref, s_ref, m_ref, ab_scr, cm_scr):
    t = ML_T
    rows = pl.ds(r0, t)
    row0 = c * 8 if isinstance(c, int) else pl.multiple_of(c * 8, 8)
    ab = ab_scr[pl.ds(row0, 8), :]
    cmr = cm_scr[pl.ds(row0, 8), :]

    tri = lax.broadcasted_iota(jnp.int32, (t, t), 0) >= lax.broadcasted_iota(jnp.int32, (t, t), 1)
    nh = ML_HEADS

    def heads(x, d):
        return jnp.stack([x[:, h * d:(h + 1) * d] for h in range(nh)], axis=0)

    def row_of(x, r):
        return jnp.stack([x[r + h:r + h + 1, :] for h in range(nh)], axis=0)

    def lane_rep(x, r):
        return jnp.stack([jnp.broadcast_to(x[r + h:r + h + 1, :], (t, t)).T for h in range(nh)], axis=0)

    def widen(x, n):
        return jnp.concatenate([x] * n, axis=2)

    q = heads(q_ref[rows, :], ML_DK)
    k = heads(k_ref[rows, :], ML_DK) * jnp.asarray(ML_DK ** -0.5, BF16)
    vext = jnp.concatenate([heads(v_ref[rows, :], ML_DV), jnp.ones((nh, t, ML_EXT - ML_DV), BF16)], axis=2)
    a_r = row_of(ab, 0)
    a_l, b_l, cm_l = lane_rep(ab, 0), lane_rep(ab, nh), lane_rep(cmr, 0)
    m_prev = m_ref[...][:, :, 0:1]

    m_l = jnp.maximum(cm_l, m_prev)
    w = jnp.where(tri[None], jnp.exp(a_r - m_l), 0.0)
    g_l = jnp.exp(m_prev - m_l)
    s = _bdot(q, k, 2, 2) * w
    tot = widen(g_l, 3) * _bdot(q, s_ref[...].astype(BF16), 2, 1) + _bdot(s.astype(BF16), vext, 2, 1)
    scale_l = 1.0 / jnp.maximum(jnp.abs(tot[:, :, ML_DV:]), jnp.exp(-(b_l + m_l)))
    hh = tot[:, :, :ML_DV] * widen(scale_l, 2)
    ssq_l = _bdot((hh * hh).astype(BF16), jnp.ones((nh, ML_DV, 128), BF16), 2, 1)
    hn = hh * widen(lax.rsqrt(ssq_l * (1.0 / ML_DV) + EPS), 2)
    for h in range(nh):
        sl = slice(h * ML_DV, (h + 1) * ML_DV)
        res = (hn[h] * g_ref[:, sl] * _sigmoid(o_ref[rows, sl].astype(F32))).astype(BF16)
        if n_store < t:
            out_ref[pl.ds(r0, n_store), sl] = res[:n_store, :]
        else:
            out_ref[rows, sl] = res

    b_last = row_of(ab, nh)[:, :, t - 1:t]
    m_x = jnp.maximum(m_prev, row_of(cmr, 0)[:, :, t - 1:t])
    vw = (vext.astype(F32) * widen(jnp.exp(a_l - m_x), 3)).astype(BF16)
    s_ref[...] = jnp.exp(m_prev - m_x) * s_ref[...] + _bdot(k, vw, 1, 1)
    m_ref[...] = jnp.broadcast_to(b_last + m_x, (nh, 1, 128))


def _mlstm_kernel(q_ref, k_ref, v_ref, o_ref, if_ref, bias_ref, g_ref, out_ref, s_ref, m_ref, ab_scr, cm_scr):
    s_ref[...] = jnp.zeros_like(s_ref)
    m_ref[...] = jnp.zeros_like(m_ref)
    _mlstm_gates(if_ref, bias_ref, ab_scr, cm_scr)
    refs = (q_ref, k_ref, v_ref, o_ref, g_ref, out_ref, s_ref, m_ref, ab_scr, cm_scr)

    _mlstm_chunk(0, 0, N_META, *refs)

    def body(i, carry):
        for k in range(ML_UNROLL):
            c = 1 + ML_UNROLL * i + k
            _mlstm_chunk(c, pl.multiple_of(N_META + (c - 1) * ML_T, N_META), ML_T, *refs)
        return carry

    lax.fori_loop(0, (ML_NCH - 1) // ML_UNROLL, body, 0)


def _mlstm(z, zift, bias, g, bsz):
    m = bsz * L
    qb = Z_MQ // ML_DIM

    def zspec(off):
        return pl.BlockSpec((L, ML_DIM), lambda b: (b, qb + off))

    return pl.pallas_call(
        _mlstm_kernel,
        grid=(bsz,),
        in_specs=[
            zspec(0), zspec(1), zspec(2), zspec(3),
            pl.BlockSpec((None, IF_ROWS, ATT_TW), lambda b: (b, 0, 0)),
            pl.BlockSpec((2 * ML_HEADS, 128), lambda b: (0, 0)),
            pl.BlockSpec((1, ML_DIM), lambda b: (0, 0)),
        ],
        out_specs=pl.BlockSpec((L, ML_DIM), lambda b: (b, 0)),
        out_shape=jax.ShapeDtypeStruct((m, ML_DIM), BF16),
        scratch_shapes=[pltpu.VMEM((ML_HEADS, ML_DK, ML_EXT), F32),
                        pltpu.VMEM((ML_HEADS, 1, 128), F32),
                        pltpu.VMEM((8 * ML_NCH, ML_T), F32),
                        pltpu.VMEM((8 * ML_NCH, ML_T), F32)],
        compiler_params=_params(("parallel",)),
        name="mlstm",
    )(z, z, z, z, zift, bias, g)


ATT_MB = 128
ATT_TW = SEQ + ATT_MB
ATT_HPS = 2


IF_ROWS = 16


def _attn_proj_kernel(u_ref, wq_ref, wv_ref, wif_ref, qt_ref, vt_ref, zift_ref):
    u_fr = u_ref[N_META:, :]
    u_meta = u_ref[0:ATT_MB, :]
    scale = DA_HD ** -0.5 * math.log2(math.e)
    wqif = jnp.concatenate([wq_ref[...].astype(BF16), wif_ref[...].astype(BF16)], axis=0)
    for lanes, u_part in ((slice(0, SEQ), u_fr), (slice(SEQ, ATT_TW), u_meta)):
        res = _dot_nt(wqif, u_part)
        qt_ref[:, lanes] = (res[:DA_DIM, :] * scale).astype(BF16)
        zift_ref[:, lanes] = res[DA_DIM:, :]
        vt_ref[:, lanes] = _dot_nt(wv_ref[...].astype(BF16), u_part).astype(BF16)


def _attn_proj(u, w_in_t, layer, bsz):
    def wspec(rows, row0):
        return _resident((pl.Element(rows), pl.Element(D_MODEL)), lambda b: (layer * W_IN + row0, 0))

    out = pl.BlockSpec((None, DA_DIM, ATT_TW), lambda b: (b, 0, 0))
    shape = jax.ShapeDtypeStruct((bsz, DA_DIM, ATT_TW), BF16)
    return pl.pallas_call(
        _attn_proj_kernel,
        grid=(bsz,),
        in_specs=[pl.BlockSpec((L, D_MODEL), lambda b: (b, 0)),
                  wspec(DA_DIM, W_IF_END), wspec(DA_DIM, W_IF_END + 2 * DA_DIM), wspec(IF_ROWS, W_M_END)],
        out_specs=[out, out, pl.BlockSpec((None, IF_ROWS, ATT_TW), lambda b: (b, 0, 0))],
        out_shape=[shape, shape, jax.ShapeDtypeStruct((bsz, IF_ROWS, ATT_TW), F32)],
        compiler_params=_params(("parallel",)),
        name="attn_qv_proj",
    )(u, w_in_t, w_in_t, w_in_t)


def _attn_scores(qt, k_ref, s_ref, n_frames, frame0):
    cols = qt.shape[1]
    ninf = -jnp.inf
    first_map = lax.broadcasted_iota(jnp.int32, (DA_VD, cols), 0) < DA_HD
    k_meta = k_ref[0:ATT_MB, :]
    meta_ok = lax.broadcasted_iota(jnp.int32, (ATT_MB, 1), 0) < N_META
    if n_frames:
        k_fr = k_ref[N_META:N_META + n_frames, :]
        qf = frame0 + lax.broadcasted_iota(jnp.int32, (1, cols), 1)
        kend = (qf // CHUNK + 1) * CHUNK
        kf = frame0 + lax.broadcasted_iota(jnp.int32, (n_frames - frame0, 1), 0)
        diag_ok = kf < kend
    for mp in range(2):
        qm = jnp.where(first_map if mp == 0 else jnp.logical_not(first_map), qt, jnp.zeros_like(qt))
        s_ref[mp, 0:ATT_MB, 0:cols] = jnp.where(meta_ok, _dot(k_meta, qm), ninf)
        if n_frames:
            s_fr = _dot(k_fr, qm)
            if frame0:
                s_ref[mp, ATT_MB:ATT_MB + frame0, 0:cols] = s_fr[:frame0, :]
            s_ref[mp, ATT_MB + frame0:ATT_MB + n_frames, 0:cols] = jnp.where(diag_ok, s_fr[frame0:, :], ninf)


def _attn_output(s_ref, vt_ref, cols, n_frames, lam, lam_init, g):
    nk = ATT_MB + n_frames
    probs = []
    for mp in range(2):
        s = s_ref[mp, 0:nk, 0:cols]
        p = jnp.exp2(s - jnp.max(s, axis=0, keepdims=True))
        probs.append((p, jnp.sum(p, axis=0, keepdims=True)))
    inv_l1 = 1.0 / probs[0][1]
    a = (probs[0][0] - probs[1][0] * (lam * probs[0][1] / probs[1][1])).astype(BF16)
    o = _dot(vt_ref[:, SEQ:ATT_TW], a[:ATT_MB, :])
    if n_frames:
        o = o + _dot(vt_ref[:, 0:n_frames], a[ATT_MB:, :])
    o = o * inv_l1
    o = o * lax.rsqrt(jnp.mean(o * o, axis=0, keepdims=True) + EPS)
    return o * g * (1.0 - lam_init)


def _attn_kernel(lam_init, qt_ref, k_ref, vt_ref, lam_ref, g_ref, out_ref, s_scr):
    lf = lam_ref[...]
    lam = (jnp.exp(jnp.sum(lf[0:1] * lf[1:2], axis=1, keepdims=True))
           - jnp.exp(jnp.sum(lf[2:3] * lf[3:4], axis=1, keepdims=True)) + lam_init)
    g = g_ref[...]
    blocks = [(SEQ, ATT_MB, 0, 0)] + [(f0, ATT_QB, f0 + ATT_QB, f0) for f0 in range(0, SEQ, ATT_QB)]
    tasks = [(hh,) + blk for hh in range(ATT_HPS) for blk in blocks]

    def head(ref, hh, axis):
        sl = pl.ds(hh * DA_VD, DA_VD)
        return ref.at[sl, :] if axis == 0 else ref.at[:, sl]

    def scores(i):
        hh, lane0, cols, n_frames, frame0 = tasks[i]
        qt = qt_ref[hh * DA_VD:(hh + 1) * DA_VD, lane0:lane0 + cols]
        _attn_scores(qt, head(k_ref, hh, 1), s_scr.at[i % 2], n_frames, frame0)

    scores(0)
    for i, (hh, lane0, cols, n_frames, frame0) in enumerate(tasks):
        if i + 1 < len(tasks):
            scores(i + 1)
        o = _attn_output(s_scr.at[i % 2], head(vt_ref, hh, 0), cols, n_frames, lam, lam_init, g)
        lanes = slice(hh * DA_VD, (hh + 1) * DA_VD)
        if n_frames:
            out_ref[N_META + frame0:N_META + frame0 + cols, lanes] = o.T.astype(BF16)
        else:
            out_ref[0:N_META, lanes] = o.T[0:N_META, :].astype(BF16)


def _attn(z, qt, vt, lam_p, g, layer, bsz):
    lam_init = 0.8 - 0.6 * math.exp(-0.3 * layer)
    m = bsz * L
    hw = ATT_HPS * DA_VD
    kb = Z_DK // hw

    def tspec():
        return pl.BlockSpec((None, hw, ATT_TW), lambda b, h: (b, h, 0))

    return pl.pallas_call(
        functools.partial(_attn_kernel, lam_init),
        grid=(bsz, DA_HEADS // ATT_HPS),
        in_specs=[
            tspec(),
            pl.BlockSpec((L, hw), lambda b, h: (b, kb + h)),
            tspec(),
            pl.BlockSpec((4, DA_HD), lambda b, h: (0, 0)),
            pl.BlockSpec((DA_VD, 1), lambda b, h: (0, 0)),
        ],
        out_specs=pl.BlockSpec((L, hw), lambda b, h: (b, h)),
        out_shape=jax.ShapeDtypeStruct((m, DA_DIM), BF16),
        scratch_shapes=[pltpu.VMEM((2, 2, ATT_TW, ATT_QB), F32)],
        compiler_params=_params(("parallel", "parallel")),
        name="diff_attn",
    )(qt, z, vt, lam_p, g)


MERGE_NC = 512


def _merge_kernel(tm, a_ref, ah_ref, hm_ref, hd_ref, gate_ref, h_ref, cw_ref, wa_ref, wm_ref, wd_ref, wo_ref,
                  g_ref, hn_ref, u_ref, mg_scr):
    ax, ab, ac = (slice(k * CONV_DIM, (k + 1) * CONV_DIM) for k in range(3))
    pos = _pos_in_batch(pl.program_id(0) * tm, tm)
    cur = a_ref[:, ac].astype(F32) * a_ref[:, ax].astype(F32)
    prev = ah_ref[:, ac].astype(F32) * ah_ref[:, ax].astype(F32)
    a_act = (a_ref[:, ab].astype(F32) * _causal_conv3(prev, cur, cw_ref[...], pos)).astype(BF16)
    hm = hm_ref[...]
    hd = hd_ref[...]
    for c0 in range(0, D_MODEL, MERGE_NC):
        cs = slice(c0, c0 + MERGE_NC)
        ga, gm, gd = (gate_ref[:, k * D_MODEL + c0:k * D_MODEL + c0 + MERGE_NC].astype(F32) for k in range(3))
        merged = (_sigmoid(ga) * _dot(a_act, wa_ref[:, cs])
                  + _sigmoid(gm) * _dot(hm, wm_ref[:, cs])
                  + _sigmoid(gd) * _dot(hd, wd_ref[:, cs]))
        mg_scr[:, cs] = merged.astype(BF16)
    hn = h_ref[...] + _dot(mg_scr[...], wo_ref[...])
    hn_ref[...] = hn
    u_ref[...] = _rms(hn, g_ref[...]).astype(BF16)


def _merge(z, hm, hd, h, layer, conv_a, w_br_a, w_br_m, w_br_d, w_out, g, tm=384):
    m = h.shape[0]
    a_w = 3 * CONV_DIM
    ab = Z_AX // a_w
    hb = tm // HALO

    def weight(rows):
        return _resident((None, rows, D_MODEL), lambda i: (layer, 0, 0))

    return pl.pallas_call(
        functools.partial(_merge_kernel, tm),
        grid=(m // tm,),
        in_specs=[
            pl.BlockSpec((tm, a_w), lambda i: (i, ab)),
            pl.BlockSpec((HALO, a_w), lambda i: (jnp.maximum(i * hb - 1, 0), ab)),
            pl.BlockSpec((tm, ML_DIM), lambda i: (i, 0)),
            pl.BlockSpec((tm, DA_DIM), lambda i: (i, 0)),
            pl.BlockSpec((tm, 3 * D_MODEL), lambda i: (i, 0)),
            pl.BlockSpec((tm, D_MODEL), lambda i: (i, 0)),
            _resident((3, CONV_DIM), lambda i: (0, 0)),
            weight(CONV_DIM), weight(ML_DIM), weight(DA_DIM), weight(D_MODEL),
            _resident((1, D_MODEL), lambda i: (0, 0)),
        ],
        out_specs=[
            pl.BlockSpec((tm, D_MODEL), lambda i: (i, 0)),
            pl.BlockSpec((tm, D_MODEL), lambda i: (i, 0)),
        ],
        out_shape=[jax.ShapeDtypeStruct((m, D_MODEL), F32),
                   jax.ShapeDtypeStruct((m, D_MODEL), BF16)],
        scratch_shapes=[pltpu.VMEM((tm, D_MODEL), BF16)],
        compiler_params=_params(("parallel",)),
        name="merge_out_proj",
    )(z, z, hm, hd, z, h, conv_a, w_br_a, w_br_m, w_br_d, w_out, g)


FFN_ROWS = 2


def _row_blocks(tm, n):
    units = tm // HALO
    edges = [HALO * (units * k // n) for k in range(n + 1)]
    return list(zip(edges[:-1], edges[1:]))


def _ffn_up_kernel(tm, u_ref, uh_ref, wa_ref, wb_ref, cw_ref, cb_ref, out_ref, w_scr):
    @pl.when(pl.program_id(1) == 0)
    def _():
        w_scr[0] = wa_ref[...].astype(BF16)
        w_scr[1] = wb_ref[...].astype(BF16)

    r_tile = pl.program_id(1) * tm
    a_prev = _dot(uh_ref[...], w_scr[0])
    for r0, r1 in _row_blocks(tm, FFN_ROWS):
        u = u_ref[r0:r1, :]
        a = _dot(u, w_scr[0])
        conv = _causal_conv3(a_prev, a, cw_ref[...], _pos_in_batch(r_tile + r0, r1 - r0)) + cb_ref[...]
        gelu = 0.5 * conv * (1.0 + lax.erf(conv * (2.0 ** -0.5)))
        out_ref[r0:r1, :] = (gelu * _dot(u, w_scr[1])).astype(BF16)
        a_prev = a[r1 - r0 - HALO:, :]


def _ffn_up(u, w_up, layer, conv_w, conv_b, tm=L, tn=512):
    m = u.shape[0]
    nb = D_FF // tn
    hb = tm // HALO
    return pl.pallas_call(
        functools.partial(_ffn_up_kernel, tm),
        grid=(nb, m // tm),
        in_specs=[
            pl.BlockSpec((tm, D_MODEL), lambda j, i: (i, 0)),
            pl.BlockSpec((HALO, D_MODEL), lambda j, i: (jnp.maximum(i * hb - 1, 0), 0)),
            pl.BlockSpec((None, D_MODEL, tn), lambda j, i: (layer, 0, j)),
            pl.BlockSpec((None, D_MODEL, tn), lambda j, i: (layer, 0, nb + j)),
            pl.BlockSpec((3, tn), lambda j, i: (0, j)),
            pl.BlockSpec((1, tn), lambda j, i: (0, j)),
        ],
        out_specs=pl.BlockSpec((tm, tn), lambda j, i: (i, j)),
        out_shape=jax.ShapeDtypeStruct((m, D_FF), BF16),
        scratch_shapes=[pltpu.VMEM((2, D_MODEL, tn), BF16)],
        compiler_params=_params(("parallel", "arbitrary")),
        name="ffn_up",
    )(u, u, w_up, w_up, conv_w, conv_b)


def _ffn_down_kernel(final, act_ref, w_ref, h_ref, g_ref, *out_refs):
    hn = h_ref[...] + _dot(act_ref[...], w_ref[...])
    y = _rms(hn, g_ref[...])
    if final:
        out_refs[0][...] = y
    else:
        out_refs[0][...] = hn
        out_refs[1][...] = y.astype(BF16)


def _ffn_down(act, w_down, layer, h, g, final):
    m = h.shape[0]
    if final:
        tm = 256
        m = m // L * SEQ
        per_batch = SEQ // tm

        def stream_row(i):
            return pl.multiple_of(i * tm + N_META * (i // per_batch + 1), N_META)

        act_spec = pl.BlockSpec((pl.Element(tm), pl.Element(D_FF)), lambda i: (stream_row(i), 0))
        h_spec = pl.BlockSpec((pl.Element(tm), pl.Element(D_MODEL)), lambda i: (stream_row(i), 0))
        row = pl.BlockSpec((tm, D_MODEL), lambda i: (i, 0))
        out_specs = [row]
        out_shape = [jax.ShapeDtypeStruct((m, D_MODEL), F32)]
    else:
        tm = 384
        row = pl.BlockSpec((tm, D_MODEL), lambda i: (i, 0))
        act_spec = pl.BlockSpec((tm, D_FF), lambda i: (i, 0))
        h_spec = row
        out_specs = [row, row]
        out_shape = [jax.ShapeDtypeStruct((m, D_MODEL), F32),
                     jax.ShapeDtypeStruct((m, D_MODEL), BF16)]
    return pl.pallas_call(
        functools.partial(_ffn_down_kernel, final),
        grid=(m // tm,),
        in_specs=[
            act_spec,
            _resident((None, D_FF, D_MODEL), lambda i: (layer, 0, 0)),
            h_spec,
            _resident((1, D_MODEL), lambda i: (0, 0)),
        ],
        out_specs=out_specs,
        out_shape=out_shape,
        compiler_params=_params(("parallel",)),
        name="ffn_down_final" if final else "ffn_down",
    )(act, w_down, h, g)


def kernel(x, meta, norm_mix, w_in, conv_a, b_if, ml_norm, da_lambda, da_norm, w_br_a, w_br_m, w_br_d,
           w_out, norm_ffn, w_up, conv_ffn, conv_ffn_b, w_down, norm_f):
    bsz, seq, d = x.shape
    assert (seq, d) == (SEQ, D_MODEL)
    depth = w_in.shape[0]
    w_in_t = jnp.swapaxes(w_in, 1, 2).reshape(depth * W_IN, D_MODEL)
    wa, wm, wd, wo, wdn = (w.astype(BF16) for w in (w_br_a, w_br_m, w_br_d, w_out, w_down))
    h, u = _embed(x.reshape(bsz * SEQ, D_MODEL), meta, norm_mix[0][None], bsz)
    for i in range(depth):
        z = _inproj(u, w_in_t, i)
        qt, vt, zift = _attn_proj(u, w_in_t, i, bsz)
        bias = jnp.broadcast_to(b_if[i].reshape(2 * ML_HEADS, 1), (2 * ML_HEADS, 128))
        hm = _mlstm(z, zift, bias, ml_norm[i][None], bsz)
        hd = _attn(z, qt, vt, da_lambda[i], da_norm[i][:, None], i, bsz)
        h, u = _merge(z, hm, hd, h, i, conv_a[i], wa, wm, wd, wo, norm_ffn[i][None])
        act = _ffn_up(u, w_up, i, conv_ffn[i], conv_ffn_b[i][None])
        final = i == depth - 1
        g_next = norm_f if final else norm_mix[i + 1]
        outs = _ffn_down(act, wdn, i, h, g_next[None], final)
        if final:
            y = outs[0]
        else:
            h, u = outs
    return y.reshape(bsz, SEQ, D_MODEL)
```

```python
import functools
import math

import jax
import jax.numpy as jnp
from jax import lax
from jax.experimental import pallas as pl
from jax.experimental.pallas import tpu as pltpu

F32 = jnp.float32
BF16 = jnp.bfloat16

D_MODEL = 2048
SEQ = 2048
N_META = 16
L = N_META + SEQ
CHUNK = 64
EPS = 1e-6

CONV_DIM = 512
ML_HEADS = 4
ML_DK = 256
ML_DV = 256
ML_DIM = ML_HEADS * ML_DV
DA_HEADS = 4
DA_HD = 64
DA_VD = 2 * DA_HD
DA_DIM = DA_HEADS * DA_VD
D_FF = 5632

Z_GATE = 0
Z_AX = 3 * D_MODEL
Z_DK = Z_AX + 3 * CONV_DIM
Z_MQ = Z_DK + DA_DIM
Z_N = Z_MQ + 4 * ML_DIM

ML_T = 128
ML_EXT = ML_DV + 128
HALO = 16
ATT_QB = 256
EMB_T = L // 3
VMEM_LIMIT = 56 * 1024 * 1024


def _dot(a, b):
    return jnp.dot(a, b, preferred_element_type=F32)


def _dot_nt(a, b):
    return lax.dot_general(a, b, (((1,), (1,)), ((), ())), preferred_element_type=F32)


def _dot_tn(a, b):
    return lax.dot_general(a, b, (((0,), (0,)), ((), ())), preferred_element_type=F32)


def _bdot(a, b, ca, cb):
    return lax.dot_general(a, b, (((ca,), (cb,)), ((0,), (0,))), preferred_element_type=F32)


def _rms(x, g):
    return x * lax.rsqrt(jnp.mean(x * x, axis=-1, keepdims=True) + EPS) * g


def _sigmoid(x):
    return 1.0 / (1.0 + jnp.exp(-x))


def _pos_in_batch(r0, tm):
    pos = r0 % L + lax.broadcasted_iota(jnp.int32, (tm, 1), 0)
    return jnp.where(pos >= L, pos - L, pos)


def _causal_conv3(prev, cur, w, pos):
    cc = jnp.concatenate([prev, cur], axis=0)
    x1 = jnp.where(pos >= 1, pltpu.roll(cc, 1, 0)[HALO:], 0.0)
    x2 = jnp.where(pos >= 2, pltpu.roll(cc, 2, 0)[HALO:], 0.0)
    return w[0:1] * x2 + w[1:2] * x1 + w[2:3] * cur


def _params(sem):
    return pltpu.CompilerParams(dimension_semantics=sem, vmem_limit_bytes=VMEM_LIMIT)


def _resident(shape, index_map):
    return pl.BlockSpec(shape, index_map, pipeline_mode=pl.Buffered(1))


def _embed_kernel(x_ref, meta_ref, g_ref, h_ref, u_ref):
    j = pl.program_id(1)

    @pl.when(j == 0)
    def _():
        rows = jnp.concatenate([meta_ref[...], x_ref[0:EMB_T - N_META, :]], axis=0)
        h_ref[...] = rows
        u_ref[...] = _rms(rows, g_ref[...]).astype(BF16)

    @pl.when(j > 0)
    def _():
        rows = x_ref[...]
        h_ref[...] = rows
        u_ref[...] = _rms(rows, g_ref[...]).astype(BF16)


def _embed(x2d, meta, g, bsz):
    nb = L // EMB_T
    m = bsz * L
    return pl.pallas_call(
        _embed_kernel,
        grid=(bsz, nb),
        in_specs=[
            pl.BlockSpec((pl.Element(EMB_T), pl.Element(D_MODEL)),
                         lambda b, j: (pl.multiple_of(b * SEQ + jnp.maximum(j * EMB_T - N_META, 0), 16), 0)),
            pl.BlockSpec((N_META, D_MODEL), lambda b, j: (0, 0)),
            pl.BlockSpec((1, D_MODEL), lambda b, j: (0, 0)),
        ],
        out_specs=[
            pl.BlockSpec((EMB_T, D_MODEL), lambda b, j: (b * nb + j, 0)),
            pl.BlockSpec((EMB_T, D_MODEL), lambda b, j: (b * nb + j, 0)),
        ],
        out_shape=[jax.ShapeDtypeStruct((m, D_MODEL), F32),
                   jax.ShapeDtypeStruct((m, D_MODEL), BF16)],
        compiler_params=_params(("parallel", "arbitrary")),
        name="embed_norm",
    )(x2d, meta, g)


W_A_END = 3 * CONV_DIM
W_M_END = W_A_END + 4 * ML_DIM
W_IF_END = W_M_END + 2 * ML_HEADS
W_D_END = W_IF_END + 3 * DA_DIM
W_IN = W_D_END + 3 * D_MODEL


IN_SRC = 512
IN_NSRC = 2


def _inproj_kernel(u_ref, *refs):
    wt_refs, z_ref = refs[:IN_NSRC], refs[IN_NSRC]
    u = u_ref[...]
    for k, wt_ref in enumerate(wt_refs):
        z_ref[:, k * IN_SRC:(k + 1) * IN_SRC] = _dot_nt(u, wt_ref[...].astype(BF16)).astype(BF16)


def _inproj(u, w_in_t, layer, tm=L):
    m = u.shape[0]
    tn = IN_NSRC * IN_SRC
    n_gate, n_a = 3 * D_MODEL // IN_SRC, 3 * CONV_DIM // IN_SRC

    def src_row(t):
        gate = W_D_END + IN_SRC * t
        cnv = IN_SRC * (t - n_gate)
        att = W_IF_END + DA_DIM + IN_SRC * (t - n_gate - n_a)
        mls = W_A_END + IN_SRC * (t - n_gate - n_a - 1)
        row = jnp.where(t < n_gate, gate,
                        jnp.where(t < n_gate + n_a, cnv, jnp.where(t < n_gate + n_a + 1, att, mls)))
        return pl.multiple_of(layer * W_IN + row, 8)

    def src_spec(k):
        return pl.BlockSpec((pl.Element(IN_SRC), pl.Element(D_MODEL)),
                            lambda i, j: (src_row(IN_NSRC * j + k), 0))

    return pl.pallas_call(
        _inproj_kernel,
        grid=(m // tm, Z_N // tn),
        in_specs=[pl.BlockSpec((tm, D_MODEL), lambda i, j: (i, 0))] + [src_spec(k) for k in range(IN_NSRC)],
        out_specs=pl.BlockSpec((tm, tn), lambda i, j: (i, j)),
        out_shape=jax.ShapeDtypeStruct((m, Z_N), BF16),
        compiler_params=_params(("parallel", "arbitrary")),
        name="in_proj",
    )(u, *([w_in_t] * IN_NSRC))


ML_NCH = 1 + SEQ // ML_T
ML_UNROLL = 2


def _mlstm_gates(ift_ref, bias_ref, ab_scr, cm_scr):
    t = ML_T
    ninf = -jnp.inf
    bias = bias_ref[...]
    starts = [SEQ] + [c * t for c in range(ML_NCH - 1)]
    gt = jnp.concatenate([ift_ref[0:8, l0:l0 + t] + bias for l0 in starts], axis=0)
    shape = (8 * ML_NCH, t)
    lane = lax.broadcasted_iota(jnp.int32, shape, 1)
    row = lax.broadcasted_iota(jnp.int32, shape, 0)
    is_i = (row & 7) < ML_HEADS
    log_sig = jnp.minimum(gt, 0.0) - jnp.log1p(jnp.exp(-jnp.abs(gt)))
    lg = jnp.where(is_i, gt, log_sig)
    weightless = jnp.logical_and(row < 8, lane >= N_META)
    lg = jnp.where(weightless, jnp.where(is_i, ninf, 0.0), lg)
    cs = jnp.where(is_i, 0.0, lg)
    sh = 1
    while sh < t:
        cs = cs + jnp.where(lane >= sh, pltpu.roll(cs, sh, 1), 0.0)
        sh *= 2
    a = jnp.where(is_i, lg - pltpu.roll(cs, 8 * ML_NCH - ML_HEADS, 0), ninf)
    cm = a
    sh = 1
    while sh < t:
        cm = jnp.maximum(cm, jnp.where(lane >= sh, pltpu.roll(cm, sh, 1), ninf))
        sh *= 2
    ab_scr[...] = jnp.where(is_i, a, cs)
    cm_scr[...] = cm


def _mlstm_chunk(c, r0, n_store, q_ref, k_ref, v_ref, o_ref, g_ref, out_ref, s_ref, m_ref, ab_scr, cm_scr):
    t = ML_T
    rows = pl.ds(r0, t)
    row0 = c * 8 if isinstance(c, int) else pl.multiple_of(c * 8, 8)
    ab = ab_scr[pl.ds(row0, 8), :]
    cmr = cm_scr[pl.ds(row0, 8), :]

    tri = lax.broadcasted_iota(jnp.int32, (t, t), 0) >= lax.broadcasted_iota(jnp.int32, (t, t), 1)
    nh = ML_HEADS

    def heads(x, d):
        return jnp.stack([x[:, h * d:(h + 1) * d] for h in range(nh)], axis=0)

    def row_of(x, r):
        return jnp.stack([x[r + h:r + h + 1, :] for h in range(nh)], axis=0)

    def lane_rep(x, r):
        return jnp.stack([jnp.broadcast_to(x[r + h:r + h + 1, :], (t, t)).T for h in range(nh)], axis=0)

    def widen(x, n):
        return jnp.concatenate([x] * n, axis=2)

    q = heads(q_ref[rows, :], ML_DK)
    k = heads(k_ref[rows, :], ML_DK) * jnp.asarray(ML_DK ** -0.5, BF16)
    vext = jnp.concatenate([heads(v_ref[rows, :], ML_DV), jnp.ones((nh, t, ML_EXT - ML_DV), BF16)], axis=2)
    a_r = row_of(ab, 0)
    a_l, b_l, cm_l = lane_rep(ab, 0), lane_rep(ab, nh), lane_rep(cmr, 0)
    m_prev = m_ref[...][:, :, 0:1]

    m_l = jnp.maximum(cm_l, m_prev)
    w = jnp.where(tri[None], jnp.exp(a_r - m_l), 0.0)
    g_l = jnp.exp(m_prev - m_l)
    s = _bdot(q, k, 2, 2) * w
    tot = widen(g_l, 3) * _bdot(q, s_ref[...].astype(BF16), 2, 1) + _bdot(s.astype(BF16), vext, 2, 1)
    scale_l = 1.0 / jnp.maximum(jnp.abs(tot[:, :, ML_DV:]), jnp.exp(-(b_l + m_l)))
    hh = tot[:, :, :ML_DV] * widen(scale_l, 2)
    ssq_l = _bdot((hh * hh).astype(BF16), jnp.ones((nh, ML_DV, 128), BF16), 2, 1)
    hn = hh * widen(lax.rsqrt(ssq_l * (1.0 / ML_DV) + EPS), 2)
    for h in range(nh):
        sl = slice(h * ML_DV, (h + 1) * ML_DV)
        res = (hn[h] * g_ref[:, sl] * _sigmoid(o_ref[rows, sl].astype(F32))).astype(BF16)
        if n_store < t:
            out_ref[pl.ds(r0, n_store), sl] = res[:n_store, :]
        else:
            out_ref[rows, sl] = res

    b_last = row_of(ab, nh)[:, :, t - 1:t]
    m_x = jnp.maximum(m_prev, row_of(cmr, 0)[:, :, t - 1:t])
    vw = (vext.astype(F32) * widen(jnp.exp(a_l - m_x), 3)).astype(BF16)
    s_ref[...] = jnp.exp(m_prev - m_x) * s_ref[...] + _bdot(k, vw, 1, 1)
    m_ref[...] = jnp.broadcast_to(b_last + m_x, (nh, 1, 128))


def _mlstm_kernel(q_ref, k_ref, v_ref, o_ref, if_ref, bias_ref, g_ref, out_ref, s_ref, m_ref, ab_scr, cm_scr):
    s_ref[...] = jnp.zeros_like(s_ref)
    m_ref[...] = jnp.zeros_like(m_ref)
    _mlstm_gates(if_ref, bias_ref, ab_scr, cm_scr)
    refs = (q_ref, k_ref, v_ref, o_ref, g_ref, out_ref, s_ref, m_ref, ab_scr, cm_scr)

    _mlstm_chunk(0, 0, N_META, *refs)

    def body(i, carry):
        for k in range(ML_UNROLL):
            c = 1 + ML_UNROLL * i + k
            _mlstm_chunk(c, pl.multiple_of(N_META + (c - 1) * ML_T, N_META), ML_T, *refs)
        return carry

    lax.fori_loop(0, (ML_NCH - 1) // ML_UNROLL, body, 0)


def _mlstm(z, zift, bias, g, bsz):
    m = bsz * L
    qb = Z_MQ // ML_DIM

    def zspec(off):
        return pl.BlockSpec((L, ML_DIM), lambda b: (b, qb + off))

    return pl.pallas_call(
        _mlstm_kernel,
        grid=(bsz,),
        in_specs=[
            zspec(0), zspec(1), zspec(2), zspec(3),
            pl.BlockSpec((None, IF_ROWS, ATT_TW), lambda b: (b, 0, 0)),
            pl.BlockSpec((2 * ML_HEADS, 128), lambda b: (0, 0)),
            pl.BlockSpec((1, ML_DIM), lambda b: (0, 0)),
        ],
        out_specs=pl.BlockSpec((L, ML_DIM), lambda b: (b, 0)),
        out_shape=jax.ShapeDtypeStruct((m, ML_DIM), BF16),
        scratch_shapes=[pltpu.VMEM((ML_HEADS, ML_DK, ML_EXT), F32),
                        pltpu.VMEM((ML_HEADS, 1, 128), F32),
                        pltpu.VMEM((8 * ML_NCH, ML_T), F32),
                        pltpu.VMEM((8 * ML_NCH, ML_T), F32)],
        compiler_params=_params(("parallel",)),
        name="mlstm",
    )(z, z, z, z, zift, bias, g)


ATT_MB = 128
ATT_TW = SEQ + ATT_MB
ATT_HPS = 2


IF_ROWS = 16


def _attn_proj_kernel(u_ref, wq_ref, wv_ref, wif_ref, qt_ref, vt_ref, zift_ref):
    u_fr = u_ref[N_META:, :]
    u_meta = u_ref[0:ATT_MB, :]
    scale = DA_HD ** -0.5 * math.log2(math.e)
    wqif = jnp.concatenate([wq_ref[...].astype(BF16), wif_ref[...].astype(BF16)], axis=0)
    for lanes, u_part in ((slice(0, SEQ), u_fr), (slice(SEQ, ATT_TW), u_meta)):
        res = _dot_nt(wqif, u_part)
        qt_ref[:, lanes] = (res[:DA_DIM, :] * scale).astype(BF16)
        zift_ref[:, lanes] = res[DA_DIM:, :]
        vt_ref[:, lanes] = _dot_nt(wv_ref[...].astype(BF16), u_part).astype(BF16)


def _attn_proj(u, w_in_t, layer, bsz):
    def wspec(rows, row0):
        return _resident((pl.Element(rows), pl.Element(D_MODEL)), lambda b: (layer * W_IN + row0, 0))

    out = pl.BlockSpec((None, DA_DIM, ATT_TW), lambda b: (b, 0, 0))
    shape = jax.ShapeDtypeStruct((bsz, DA_DIM, ATT_TW), BF16)
    return pl.pallas_call(
        _attn_proj_kernel,
        grid=(bsz,),
        in_specs=[pl.BlockSpec((L, D_MODEL), lambda b: (b, 0)),
                  wspec(DA_DIM, W_IF_END), wspec(DA_DIM, W_IF_END + 2 * DA_DIM), wspec(IF_ROWS, W_M_END)],
        out_specs=[out, out, pl.BlockSpec((None, IF_ROWS, ATT_TW), lambda b: (b, 0, 0))],
        out_shape=[shape, shape, jax.ShapeDtypeStruct((bsz, IF_ROWS, ATT_TW), F32)],
        compiler_params=_params(("parallel",)),
        name="attn_qv_proj",
    )(u, w_in_t, w_in_t, w_in_t)


def _attn_scores(qt, k_ref, s_ref, n_frames, frame0):
    cols = qt.shape[1]
    ninf = -jnp.inf
    first_map = lax.broadcasted_iota(jnp.int32, (DA_VD, cols), 0) < DA_HD
    k_meta = k_ref[0:ATT_MB, :]
    meta_ok = lax.broadcasted_iota(jnp.int32, (ATT_MB, 1), 0) < N_META
    if n_frames:
        k_fr = k_ref[N_META:N_META + n_frames, :]
        qf = frame0 + lax.broadcasted_iota(jnp.int32, (1, cols), 1)
        kend = (qf // CHUNK + 1) * CHUNK
        kf = frame0 + lax.broadcasted_iota(jnp.int32, (n_frames - frame0, 1), 0)
        diag_ok = kf < kend
    for mp in range(2):
        qm = jnp.where(first_map if mp == 0 else jnp.logical_not(first_map), qt, jnp.zeros_like(qt))
        s_ref[mp, 0:ATT_MB, 0:cols] = jnp.where(meta_ok, _dot(k_meta, qm), ninf)
        if n_frames:
            s_fr = _dot(k_fr, qm)
            if frame0:
                s_ref[mp, ATT_MB:ATT_MB + frame0, 0:cols] = s_fr[:frame0, :]
            s_ref[mp, ATT_MB + frame0:ATT_MB + n_frames, 0:cols] = jnp.where(diag_ok, s_fr[frame0:, :], ninf)


def _attn_output(s_ref, vt_ref, cols, n_frames, lam, lam_init, g):
    nk = ATT_MB + n_frames
    probs = []
    for mp in range(2):
        s = s_ref[mp, 0:nk, 0:cols]
        p = jnp.exp2(s - jnp.max(s, axis=0, keepdims=True))
        probs.append((p, jnp.sum(p, axis=0, keepdims=True)))
    inv_l1 = 1.0 / probs[0][1]
    a = (probs[0][0] - probs[1][0] * (lam * probs[0][1] / probs[1][1])).astype(BF16)
    o = _dot(vt_ref[:, SEQ:ATT_TW], a[:ATT_MB, :])
    if n_frames:
        o = o + _dot(vt_ref[:, 0:n_frames], a[ATT_MB:, :])
    o = o * inv_l1
    o = o * lax.rsqrt(jnp.mean(o * o, axis=0, keepdims=True) + EPS)
    return o * g * (1.0 - lam_init)


def _attn_kernel(lam_init, qt_ref, k_ref, vt_ref, lam_ref, g_ref, out_ref, s_scr):
    lf = lam_ref[...]
    lam = (jnp.exp(jnp.sum(lf[0:1] * lf[1:2], axis=1, keepdims=True))
           - jnp.exp(jnp.sum(lf[2:3] * lf[3:4], axis=1, keepdims=True)) + lam_init)
    g = g_ref[...]
    blocks = [(SEQ, ATT_MB, 0, 0)] + [(f0, ATT_QB, f0 + ATT_QB, f0) for f0 in range(0, SEQ, ATT_QB)]
    tasks = [(hh,) + blk for hh in range(ATT_HPS) for blk in blocks]

    def head(ref, hh, axis):
        sl = pl.ds(hh * DA_VD, DA_VD)
        return ref.at[sl, :] if axis == 0 else ref.at[:, sl]

    def scores(i):
        hh, lane0, cols, n_frames, frame0 = tasks[i]
        qt = qt_ref[hh * DA_VD:(hh + 1) * DA_VD, lane0:lane0 + cols]
        _attn_scores(qt, head(k_ref, hh, 1), s_scr.at[i % 2], n_frames, frame0)

    scores(0)
    for i, (hh, lane0, cols, n_frames, frame0) in enumerate(tasks):
        if i + 1 < len(tasks):
            scores(i + 1)
        o = _attn_output(s_scr.at[i % 2], head(vt_ref, hh, 0), cols, n_frames, lam, lam_init, g)
        lanes = slice(hh * DA_VD, (hh + 1) * DA_VD)
        if n_frames:
            out_ref[N_META + frame0:N_META + frame0 + cols, lanes] = o.T.astype(BF16)
        else:
            out_ref[0:N_META, lanes] = o.T[0:N_META, :].astype(BF16)


def _attn(z, qt, vt, lam_p, g, layer, bsz):
    lam_init = 0.8 - 0.6 * math.exp(-0.3 * layer)
    m = bsz * L
    hw = ATT_HPS * DA_VD
    kb = Z_DK // hw

    def tspec():
        return pl.BlockSpec((None, hw, ATT_TW), lambda b, h: (b, h, 0))

    return pl.pallas_call(
        functools.partial(_attn_kernel, lam_init),
        grid=(bsz, DA_HEADS // ATT_HPS),
        in_specs=[
            tspec(),
            pl.BlockSpec((L, hw), lambda b, h: (b, kb + h)),
            tspec(),
            pl.BlockSpec((4, DA_HD), lambda b, h: (0, 0)),
            pl.BlockSpec((DA_VD, 1), lambda b, h: (0, 0)),
        ],
        out_specs=pl.BlockSpec((L, hw), lambda b, h: (b, h)),
        out_shape=jax.ShapeDtypeStruct((m, DA_DIM), BF16),
        scratch_shapes=[pltpu.VMEM((2, 2, ATT_TW, ATT_QB), F32)],
        compiler_params=_params(("parallel", "parallel")),
        name="diff_attn",
    )(qt, z, vt, lam_p, g)


MERGE_NC = 1024


def _merge_kernel(tm, a_ref, ah_ref, hm_ref, hd_ref, gate_ref, h_ref, cw_ref, wa_ref, wm_ref, wd_ref, wo_ref,
                  g_ref, hn_ref, u_ref, mg_scr):
    ax, ab, ac = (slice(k * CONV_DIM, (k + 1) * CONV_DIM) for k in range(3))
    pos = _pos_in_batch(pl.program_id(0) * tm, tm)
    cur = a_ref[:, ac].astype(F32) * a_ref[:, ax].astype(F32)
    prev = ah_ref[:, ac].astype(F32) * ah_ref[:, ax].astype(F32)
    a_act = (a_ref[:, ab].astype(F32) * _causal_conv3(prev, cur, cw_ref[...], pos)).astype(BF16)
    hm = hm_ref[...]
    hd = hd_ref[...]
    for c0 in range(0, D_MODEL, MERGE_NC):
        cs = slice(c0, c0 + MERGE_NC)
        ga, gm, gd = (gate_ref[:, k * D_MODEL + c0:k * D_MODEL + c0 + MERGE_NC].astype(F32) for k in range(3))
        merged = (_sigmoid(ga) * _dot(a_act, wa_ref[:, cs])
                  + _sigmoid(gm) * _dot(hm, wm_ref[:, cs])
                  + _sigmoid(gd) * _dot(hd, wd_ref[:, cs]))
        mg_scr[:, cs] = merged.astype(BF16)
    hn = h_ref[...] + _dot(mg_scr[...], wo_ref[...])
    hn_ref[...] = hn
    u_ref[...] = _rms(hn, g_ref[...]).astype(BF16)


def _merge(z, hm, hd, h, layer, conv_a, w_br_a, w_br_m, w_br_d, w_out, g, tm=384):
    m = h.shape[0]
    a_w = 3 * CONV_DIM
    ab = Z_AX // a_w
    hb = tm // HALO

    def weight(rows):
        return _resident((None, rows, D_MODEL), lambda i: (layer, 0, 0))

    return pl.pallas_call(
        functools.partial(_merge_kernel, tm),
        grid=(m // tm,),
        in_specs=[
            pl.BlockSpec((tm, a_w), lambda i: (i, ab)),
            pl.BlockSpec((HALO, a_w), lambda i: (jnp.maximum(i * hb - 1, 0), ab)),
            pl.BlockSpec((tm, ML_DIM), lambda i: (i, 0)),
            pl.BlockSpec((tm, DA_DIM), lambda i: (i, 0)),
            pl.BlockSpec((tm, 3 * D_MODEL), lambda i: (i, 0)),
            pl.BlockSpec((tm, D_MODEL), lambda i: (i, 0)),
            _resident((3, CONV_DIM), lambda i: (0, 0)),
            weight(CONV_DIM), weight(ML_DIM), weight(DA_DIM), weight(D_MODEL),
            _resident((1, D_MODEL), lambda i: (0, 0)),
        ],
        out_specs=[
            pl.BlockSpec((tm, D_MODEL), lambda i: (i, 0)),
            pl.BlockSpec((tm, D_MODEL), lambda i: (i, 0)),
        ],
        out_shape=[jax.ShapeDtypeStruct((m, D_MODEL), F32),
                   jax.ShapeDtypeStruct((m, D_MODEL), BF16)],
        scratch_shapes=[pltpu.VMEM((tm, D_MODEL), BF16)],
        compiler_params=_params(("parallel",)),
        name="merge_out_proj",
    )(z, z, hm, hd, z, h, conv_a, w_br_a, w_br_m, w_br_d, w_out, g)


FFN_ROWS = 2


def _row_blocks(tm, n):
    units = tm // HALO
    edges = [HALO * (units * k // n) for k in range(n + 1)]
    return list(zip(edges[:-1], edges[1:]))


def _ffn_up_kernel(tm, u_ref, uh_ref, wa_ref, wb_ref, cw_ref, cb_ref, out_ref, w_scr):
    @pl.when(pl.program_id(1) == 0)
    def _():
        w_scr[0] = wa_ref[...].astype(BF16)
        w_scr[1] = wb_ref[...].astype(BF16)

    r_tile = pl.program_id(1) * tm
    a_prev = _dot(uh_ref[...], w_scr[0])
    for r0, r1 in _row_blocks(tm, FFN_ROWS):
        u = u_ref[r0:r1, :]
        a = _dot(u, w_scr[0])
        conv = _causal_conv3(a_prev, a, cw_ref[...], _pos_in_batch(r_tile + r0, r1 - r0)) + cb_ref[...]
        gelu = 0.5 * conv * (1.0 + lax.erf(conv * (2.0 ** -0.5)))
        out_ref[r0:r1, :] = (gelu * _dot(u, w_scr[1])).astype(BF16)
        a_prev = a[r1 - r0 - HALO:, :]


def _ffn_up(u, w_up, layer, conv_w, conv_b, tm=L, tn=512):
    m = u.shape[0]
    nb = D_FF // tn
    hb = tm // HALO
    return pl.pallas_call(
        functools.partial(_ffn_up_kernel, tm),
        grid=(nb, m // tm),
        in_specs=[
            pl.BlockSpec((tm, D_MODEL), lambda j, i: (i, 0)),
            pl.BlockSpec((HALO, D_MODEL), lambda j, i: (jnp.maximum(i * hb - 1, 0), 0)),
            pl.BlockSpec((None, D_MODEL, tn), lambda j, i: (layer, 0, j)),
            pl.BlockSpec((None, D_MODEL, tn), lambda j, i: (layer, 0, nb + j)),
            pl.BlockSpec((3, tn), lambda j, i: (0, j)),
            pl.BlockSpec((1, tn), lambda j, i: (0, j)),
        ],
        out_specs=pl.BlockSpec((tm, tn), lambda j, i: (i, j)),
        out_shape=jax.ShapeDtypeStruct((m, D_FF), BF16),
        scratch_shapes=[pltpu.VMEM((2, D_MODEL, tn), BF16)],
        compiler_params=_params(("parallel", "arbitrary")),
        name="ffn_up",
    )(u, u, w_up, w_up, conv_w, conv_b)


def _ffn_down_kernel(final, act_ref, w_ref, h_ref, g_ref, *out_refs):
    hn = h_ref[...] + _dot(act_ref[...], w_ref[...])
    y = _rms(hn, g_ref[...])
    if final:
        out_refs[0][...] = y
    else:
        out_refs[0][...] = hn
        out_refs[1][...] = y.astype(BF16)


def _ffn_down(act, w_down, layer, h, g, final):
    m = h.shape[0]
    if final:
        tm = 256
        m = m // L * SEQ
        per_batch = SEQ // tm

        def stream_row(i):
            return pl.multiple_of(i * tm + N_META * (i // per_batch + 1), N_META)

        act_spec = pl.BlockSpec((pl.Element(tm), pl.Element(D_FF)), lambda i: (stream_row(i), 0))
        h_spec = pl.BlockSpec((pl.Element(tm), pl.Element(D_MODEL)), lambda i: (stream_row(i), 0))
        row = pl.BlockSpec((tm, D_MODEL), lambda i: (i, 0))
        out_specs = [row]
        out_shape = [jax.ShapeDtypeStruct((m, D_MODEL), F32)]
    else:
        tm = 384
        row = pl.BlockSpec((tm, D_MODEL), lambda i: (i, 0))
        act_spec = pl.BlockSpec((tm, D_FF), lambda i: (i, 0))
        h_spec = row
        out_specs = [row, row]
        out_shape = [jax.ShapeDtypeStruct((m, D_MODEL), F32),
                     jax.ShapeDtypeStruct((m, D_MODEL), BF16)]
    return pl.pallas_call(
        functools.partial(_ffn_down_kernel, final),
        grid=(m // tm,),
        in_specs=[
            act_spec,
            _resident((None, D_FF, D_MODEL), lambda i: (layer, 0, 0)),
            h_spec,
            _resident((1, D_MODEL), lambda i: (0, 0)),
        ],
        out_specs=out_specs,
        out_shape=out_shape,
        compiler_params=_params(("parallel",)),
        name="ffn_down_final" if final else "ffn_down",
    )(act, w_down, h, g)


def kernel(x, meta, norm_mix, w_in, conv_a, b_if, ml_norm, da_lambda, da_norm, w_br_a, w_br_m, w_br_d,
           w_out, norm_ffn, w_up, conv_ffn, conv_ffn_b, w_down, norm_f):
    bsz, seq, d = x.shape
    assert (seq, d) == (SEQ, D_MODEL)
    depth = w_in.shape[0]
    w_in_t = jnp.swapaxes(w_in, 1, 2).reshape(depth * W_IN, D_MODEL)
    wa, wm, wd, wo, wdn = (w.astype(BF16) for w in (w_br_a, w_br_m, w_br_d, w_out, w_down))
    h, u = _embed(x.reshape(bsz * SEQ, D_MODEL), meta, norm_mix[0][None], bsz)
    for i in range(depth):
        z = _inproj(u, w_in_t, i)
        qt, vt, zift = _attn_proj(u, w_in_t, i, bsz)
        bias = jnp.broadcast_to(b_if[i].reshape(2 * ML_HEADS, 1), (2 * ML_HEADS, 128))
        hm = _mlstm(z, zift, bias, ml_norm[i][None], bsz)
        hd = _attn(z, qt, vt, da_lambda[i], da_norm[i][:, None], i, bsz)
        h, u = _merge(z, hm, hd, h, i, conv_a[i], wa, wm, wd, wo, norm_ffn[i][None])
        act = _ffn_up(u, w_up, i, conv_ffn[i], conv_ffn_b[i][None])
        final = i == depth - 1
        g_next = norm_f if final else norm_mix[i + 1]
        outs = _ffn_down(act, wdn, i, h, g_next[None], final)
        if final:
            y = outs[0]
        else:
            h, u = outs
    return y.reshape(bsz, SEQ, D_MODEL)
```

```python
import functools
import math

import jax
import jax.numpy as jnp
from jax import lax
from jax.experimental import pallas as pl
from jax.experimental.pallas import tpu as pltpu

F32 = jnp.float32
BF16 = jnp.bfloat16

D_MODEL = 2048
SEQ = 2048
N_META = 16
L = N_META + SEQ
CHUNK = 64
EPS = 1e-6

CONV_DIM = 512
ML_HEADS = 4
ML_DK = 256
ML_DV = 256
ML_DIM = ML_HEADS * ML_DV
DA_HEADS = 4
DA_HD = 64
DA_VD = 2 * DA_HD
DA_DIM = DA_HEADS * DA_VD
D_FF = 5632

Z_AX = 3 * D_MODEL
Z_DK = Z_AX + 3 * CONV_DIM
Z_MQ = Z_DK + DA_DIM
Z_N = Z_MQ + 4 * ML_DIM

LANES = 128
ML_T = LANES
ML_EXT = ML_DV + LANES
HALO = 16
ATT_QB = 256
EMB_T = L // 3
VMEM_LIMIT = 56 * 1024 * 1024


def _dot(a, b):
    return jnp.dot(a, b, preferred_element_type=F32)


def _dot_nt(a, b):
    return lax.dot_general(a, b, (((1,), (1,)), ((), ())), preferred_element_type=F32)


def _bdot(a, b, ca, cb):
    return lax.dot_general(a, b, (((ca,), (cb,)), ((0,), (0,))), preferred_element_type=F32)


def _rms(x, g):
    return x * lax.rsqrt(jnp.mean(x * x, axis=-1, keepdims=True) + EPS) * g


def _sigmoid(x):
    return 1.0 / (1.0 + jnp.exp(-x))


def _pos_in_batch(r0, tm):
    pos = r0 % L + lax.broadcasted_iota(jnp.int32, (tm, 1), 0)
    return jnp.where(pos >= L, pos - L, pos)


def _causal_conv3(prev, cur, w, pos):
    cc = jnp.concatenate([prev, cur], axis=0)
    x1 = jnp.where(pos >= 1, pltpu.roll(cc, 1, 0)[HALO:], 0.0)
    x2 = jnp.where(pos >= 2, pltpu.roll(cc, 2, 0)[HALO:], 0.0)
    return w[0:1] * x2 + w[1:2] * x1 + w[2:3] * cur


def _params(sem):
    return pltpu.CompilerParams(dimension_semantics=sem, vmem_limit_bytes=VMEM_LIMIT)


def _resident(shape, index_map):
    return pl.BlockSpec(shape, index_map, pipeline_mode=pl.Buffered(1))


def _embed_kernel(x_ref, meta_ref, g_ref, h_ref, u_ref):
    j = pl.program_id(1)

    @pl.when(j == 0)
    def _():
        rows = jnp.concatenate([meta_ref[...], x_ref[0:EMB_T - N_META, :]], axis=0)
        h_ref[...] = rows
        u_ref[...] = _rms(rows, g_ref[...]).astype(BF16)

    @pl.when(j > 0)
    def _():
        rows = x_ref[...]
        h_ref[...] = rows
        u_ref[...] = _rms(rows, g_ref[...]).astype(BF16)


def _embed(x2d, meta, g, bsz):
    nb = L // EMB_T
    m = bsz * L
    return pl.pallas_call(
        _embed_kernel,
        grid=(bsz, nb),
        in_specs=[
            pl.BlockSpec((pl.Element(EMB_T), pl.Element(D_MODEL)),
                         lambda b, j: (pl.multiple_of(b * SEQ + jnp.maximum(j * EMB_T - N_META, 0), 16), 0)),
            pl.BlockSpec((N_META, D_MODEL), lambda b, j: (0, 0)),
            pl.BlockSpec((1, D_MODEL), lambda b, j: (0, 0)),
        ],
        out_specs=[
            pl.BlockSpec((EMB_T, D_MODEL), lambda b, j: (b * nb + j, 0)),
            pl.BlockSpec((EMB_T, D_MODEL), lambda b, j: (b * nb + j, 0)),
        ],
        out_shape=[jax.ShapeDtypeStruct((m, D_MODEL), F32),
                   jax.ShapeDtypeStruct((m, D_MODEL), BF16)],
        compiler_params=_params(("parallel", "arbitrary")),
        name="embed_norm",
    )(x2d, meta, g)


W_A_END = 3 * CONV_DIM
W_M_END = W_A_END + 4 * ML_DIM
W_IF_END = W_M_END + 2 * ML_HEADS
W_D_END = W_IF_END + 3 * DA_DIM
W_IN = W_D_END + 3 * D_MODEL


IN_SRC = 512
IN_NSRC = 2


def _inproj_kernel(u_ref, *refs):
    wt_refs, z_ref = refs[:IN_NSRC], refs[IN_NSRC]
    u = u_ref[...]
    for k, wt_ref in enumerate(wt_refs):
        z_ref[:, k * IN_SRC:(k + 1) * IN_SRC] = _dot_nt(u, wt_ref[...].astype(BF16)).astype(BF16)


def _inproj(u, w_in_t, layer, tm=L):
    m = u.shape[0]
    tn = IN_NSRC * IN_SRC
    n_gate, n_a = 3 * D_MODEL // IN_SRC, 3 * CONV_DIM // IN_SRC

    def src_row(t):
        gate = W_D_END + IN_SRC * t
        cnv = IN_SRC * (t - n_gate)
        att = W_IF_END + DA_DIM + IN_SRC * (t - n_gate - n_a)
        mls = W_A_END + IN_SRC * (t - n_gate - n_a - 1)
        row = jnp.where(t < n_gate, gate,
                        jnp.where(t < n_gate + n_a, cnv, jnp.where(t < n_gate + n_a + 1, att, mls)))
        return pl.multiple_of(layer * W_IN + row, 8)

    def src_spec(k):
        return pl.BlockSpec((pl.Element(IN_SRC), pl.Element(D_MODEL)),
                            lambda i, j: (src_row(IN_NSRC * j + k), 0))

    return pl.pallas_call(
        _inproj_kernel,
        grid=(m // tm, Z_N // tn),
        in_specs=[pl.BlockSpec((tm, D_MODEL), lambda i, j: (i, 0))] + [src_spec(k) for k in range(IN_NSRC)],
        out_specs=pl.BlockSpec((tm, tn), lambda i, j: (i, j)),
        out_shape=jax.ShapeDtypeStruct((m, Z_N), BF16),
        compiler_params=_params(("parallel", "arbitrary")),
        name="in_proj",
    )(u, *([w_in_t] * IN_NSRC))


ML_NCH = 1 + SEQ // ML_T
ML_UNROLL = 2


def _mlstm_gates(ift_ref, bias_ref, ab_scr, cm_scr):
    t = ML_T
    ninf = -jnp.inf
    bias = bias_ref[...]
    starts = [SEQ] + [c * t for c in range(ML_NCH - 1)]
    gt = jnp.concatenate([ift_ref[0:8, l0:l0 + t] + bias for l0 in starts], axis=0)
    shape = (8 * ML_NCH, t)
    lane = lax.broadcasted_iota(jnp.int32, shape, 1)
    row = lax.broadcasted_iota(jnp.int32, shape, 0)
    is_i = (row & 7) < ML_HEADS
    log_sig = jnp.minimum(gt, 0.0) - jnp.log1p(jnp.exp(-jnp.abs(gt)))
    lg = jnp.where(is_i, gt, log_sig)
    weightless = jnp.logical_and(row < 8, lane >= N_META)
    lg = jnp.where(weightless, jnp.where(is_i, ninf, 0.0), lg)
    cs = jnp.where(is_i, 0.0, lg)
    sh = 1
    while sh < t:
        cs = cs + jnp.where(lane >= sh, pltpu.roll(cs, sh, 1), 0.0)
        sh *= 2
    a = jnp.where(is_i, lg - pltpu.roll(cs, 8 * ML_NCH - ML_HEADS, 0), ninf)
    cm = a
    sh = 1
    while sh < t:
        cm = jnp.maximum(cm, jnp.where(lane >= sh, pltpu.roll(cm, sh, 1), ninf))
        sh *= 2
    ab_scr[...] = jnp.where(is_i, a, cs)
    cm_scr[...] = cm


def _mlstm_chunk(c, r0, n_store, q_ref, k_ref, v_ref, o_ref, g_ref, out_ref, s_ref, m_ref, ab_scr, cm_scr):
    t = ML_T
    rows = pl.ds(r0, t)
    row0 = c * 8 if isinstance(c, int) else pl.multiple_of(c * 8, 8)
    ab = ab_scr[pl.ds(row0, 8), :]
    cmr = cm_scr[pl.ds(row0, 8), :]

    tri = lax.broadcasted_iota(jnp.int32, (t, t), 0) >= lax.broadcasted_iota(jnp.int32, (t, t), 1)
    nh = ML_HEADS

    def heads(x, d):
        return jnp.stack([x[:, h * d:(h + 1) * d] for h in range(nh)], axis=0)

    def row_of(x, r):
        return jnp.stack([x[r + h:r + h + 1, :] for h in range(nh)], axis=0)

    def lane_rep(x, r):
        return jnp.stack([jnp.broadcast_to(x[r + h:r + h + 1, :], (t, t)).T for h in range(nh)], axis=0)

    def widen(x, n):
        return jnp.concatenate([x] * n, axis=2)

    q = heads(q_ref[rows, :], ML_DK)
    k = heads(k_ref[rows, :], ML_DK) * jnp.asarray(ML_DK ** -0.5, BF16)
    vext = jnp.concatenate([heads(v_ref[rows, :], ML_DV), jnp.ones((nh, t, ML_EXT - ML_DV), BF16)], axis=2)
    a_r = row_of(ab, 0)
    a_l, b_l, cm_l = lane_rep(ab, 0), lane_rep(ab, nh), lane_rep(cmr, 0)
    m_prev = m_ref[...][:, :, 0:1]

    m_l = jnp.maximum(cm_l, m_prev)
    w = jnp.where(tri[None], jnp.exp(a_r - m_l), 0.0)
    g_l = jnp.exp(m_prev - m_l)
    s = _bdot(q, k, 2, 2) * w
    tot = widen(g_l, 3) * _bdot(q, s_ref[...].astype(BF16), 2, 1) + _bdot(s.astype(BF16), vext, 2, 1)
    scale_l = 1.0 / jnp.maximum(jnp.abs(tot[:, :, ML_DV:]), jnp.exp(-(b_l + m_l)))
    hh = tot[:, :, :ML_DV] * widen(scale_l, 2)
    ssq_l = _bdot((hh * hh).astype(BF16), jnp.ones((nh, ML_DV, LANES), BF16), 2, 1)
    hn = hh * widen(lax.rsqrt(ssq_l * (1.0 / ML_DV) + EPS), 2)
    for h in range(nh):
        sl = slice(h * ML_DV, (h + 1) * ML_DV)
        res = (hn[h] * g_ref[:, sl] * _sigmoid(o_ref[rows, sl].astype(F32))).astype(BF16)
        if n_store < t:
            out_ref[pl.ds(r0, n_store), sl] = res[:n_store, :]
        else:
            out_ref[rows, sl] = res

    b_last = row_of(ab, nh)[:, :, t - 1:t]
    m_x = jnp.maximum(m_prev, row_of(cmr, 0)[:, :, t - 1:t])
    vw = vext * widen(jnp.exp(a_l - m_x).astype(BF16), 3)
    s_ref[...] = jnp.exp(m_prev - m_x) * s_ref[...] + _bdot(k, vw, 1, 1)
    m_ref[...] = jnp.broadcast_to(b_last + m_x, (nh, 1, LANES))


def _mlstm_kernel(q_ref, k_ref, v_ref, o_ref, if_ref, bias_ref, g_ref, out_ref, s_ref, m_ref, ab_scr, cm_scr):
    s_ref[...] = jnp.zeros_like(s_ref)
    m_ref[...] = jnp.zeros_like(m_ref)
    _mlstm_gates(if_ref, bias_ref, ab_scr, cm_scr)
    refs = (q_ref, k_ref, v_ref, o_ref, g_ref, out_ref, s_ref, m_ref, ab_scr, cm_scr)

    _mlstm_chunk(0, 0, N_META, *refs)

    def body(i, carry):
        for k in range(ML_UNROLL):
            c = 1 + ML_UNROLL * i + k
            _mlstm_chunk(c, pl.multiple_of(N_META + (c - 1) * ML_T, N_META), ML_T, *refs)
        return carry

    lax.fori_loop(0, (ML_NCH - 1) // ML_UNROLL, body, 0)


def _mlstm(z, zift, bias, g, bsz):
    m = bsz * L
    qb = Z_MQ // ML_DIM

    def zspec(off):
        return pl.BlockSpec((L, ML_DIM), lambda b: (b, qb + off))

    return pl.pallas_call(
        _mlstm_kernel,
        grid=(bsz,),
        in_specs=[
            zspec(0), zspec(1), zspec(2), zspec(3),
            pl.BlockSpec((None, IF_ROWS, ATT_TW), lambda b: (b, 0, 0)),
            pl.BlockSpec((2 * ML_HEADS, LANES), lambda b: (0, 0)),
            pl.BlockSpec((1, ML_DIM), lambda b: (0, 0)),
        ],
        out_specs=pl.BlockSpec((L, ML_DIM), lambda b: (b, 0)),
        out_shape=jax.ShapeDtypeStruct((m, ML_DIM), BF16),
        scratch_shapes=[pltpu.VMEM((ML_HEADS, ML_DK, ML_EXT), F32),
                        pltpu.VMEM((ML_HEADS, 1, LANES), F32),
                        pltpu.VMEM((8 * ML_NCH, ML_T), F32),
                        pltpu.VMEM((8 * ML_NCH, ML_T), F32)],
        compiler_params=_params(("parallel",)),
        name="mlstm",
    )(z, z, z, z, zift, bias, g)


ATT_MB = LANES
ATT_TW = SEQ + ATT_MB
ATT_HPS = 2


IF_ROWS = 16


def _attn_proj_kernel(u_ref, wq_ref, wv_ref, wif_ref, qt_ref, vt_ref, zift_ref):
    u_fr = u_ref[N_META:, :]
    u_meta = u_ref[0:ATT_MB, :]
    scale = DA_HD ** -0.5 * math.log2(math.e)
    wqif = jnp.concatenate([wq_ref[...].astype(BF16), wif_ref[...].astype(BF16)], axis=0)
    for lanes, u_part in ((slice(0, SEQ), u_fr), (slice(SEQ, ATT_TW), u_meta)):
        res = _dot_nt(wqif, u_part)
        qt_ref[:, lanes] = (res[:DA_DIM, :] * scale).astype(BF16)
        zift_ref[:, lanes] = res[DA_DIM:, :]
        vt_ref[:, lanes] = _dot_nt(wv_ref[...].astype(BF16), u_part).astype(BF16)


def _attn_proj(u, w_in_t, layer, bsz):
    def wspec(rows, row0):
        return _resident((pl.Element(rows), pl.Element(D_MODEL)), lambda b: (layer * W_IN + row0, 0))

    out = pl.BlockSpec((None, DA_DIM, ATT_TW), lambda b: (b, 0, 0))
    shape = jax.ShapeDtypeStruct((bsz, DA_DIM, ATT_TW), BF16)
    return pl.pallas_call(
        _attn_proj_kernel,
        grid=(bsz,),
        in_specs=[pl.BlockSpec((L, D_MODEL), lambda b: (b, 0)),
                  wspec(DA_DIM, W_IF_END), wspec(DA_DIM, W_IF_END + 2 * DA_DIM), wspec(IF_ROWS, W_M_END)],
        out_specs=[out, out, pl.BlockSpec((None, IF_ROWS, ATT_TW), lambda b: (b, 0, 0))],
        out_shape=[shape, shape, jax.ShapeDtypeStruct((bsz, IF_ROWS, ATT_TW), F32)],
        compiler_params=_params(("parallel",)),
        name="attn_qv_proj",
    )(u, w_in_t, w_in_t, w_in_t)


def _attn_scores(qt, k_ref, s_ref, n_frames, frame0):
    cols = qt.shape[1]
    ninf = -jnp.inf
    first_map = lax.broadcasted_iota(jnp.int32, (DA_VD, cols), 0) < DA_HD
    k_meta = k_ref[0:ATT_MB, :]
    meta_ok = lax.broadcasted_iota(jnp.int32, (ATT_MB, 1), 0) < N_META
    if n_frames:
        k_fr = k_ref[N_META:N_META + n_frames, :]
        qf = frame0 + lax.broadcasted_iota(jnp.int32, (1, cols), 1)
        kend = (qf // CHUNK + 1) * CHUNK
        kf = frame0 + lax.broadcasted_iota(jnp.int32, (n_frames - frame0, 1), 0)
        diag_ok = kf < kend
    for mp in range(2):
        qm = jnp.where(first_map if mp == 0 else jnp.logical_not(first_map), qt, jnp.zeros_like(qt))
        s_ref[mp, 0:ATT_MB, 0:cols] = jnp.where(meta_ok, _dot(k_meta, qm), ninf)
        if n_frames:
            s_fr = _dot(k_fr, qm)
            if frame0:
                s_ref[mp, ATT_MB:ATT_MB + frame0, 0:cols] = s_fr[:frame0, :]
            s_ref[mp, ATT_MB + frame0:ATT_MB + n_frames, 0:cols] = jnp.where(diag_ok, s_fr[frame0:, :], ninf)


def _attn_output(s_ref, vt_ref, cols, n_frames, lam, lam_init, g):
    nk = ATT_MB + n_frames
    probs = []
    for mp in range(2):
        s = s_ref[mp, 0:nk, 0:cols]
        p = jnp.exp2(s - jnp.max(s, axis=0, keepdims=True))
        probs.append((p, jnp.sum(p, axis=0, keepdims=True)))
    inv_l1 = 1.0 / probs[0][1]
    a = (probs[0][0] - probs[1][0] * (lam * probs[0][1] / probs[1][1])).astype(BF16)
    o = _dot(vt_ref[:, SEQ:ATT_TW], a[:ATT_MB, :])
    if n_frames:
        o = o + _dot(vt_ref[:, 0:n_frames], a[ATT_MB:, :])
    o = o * inv_l1
    o = o * lax.rsqrt(jnp.mean(o * o, axis=0, keepdims=True) + EPS)
    return o * g * (1.0 - lam_init)


def _attn_kernel(lam_init, qt_ref, k_ref, vt_ref, lam_ref, g_ref, out_ref, s_scr):
    lf = lam_ref[...]
    lam = (jnp.exp(jnp.sum(lf[0:1] * lf[1:2], axis=1, keepdims=True))
           - jnp.exp(jnp.sum(lf[2:3] * lf[3:4], axis=1, keepdims=True)) + lam_init)
    g = g_ref[...]
    blocks = [(SEQ, ATT_MB, 0, 0)] + [(f0, ATT_QB, f0 + ATT_QB, f0) for f0 in range(0, SEQ, ATT_QB)]
    tasks = [(hh,) + blk for hh in range(ATT_HPS) for blk in blocks]

    def head(ref, hh, axis):
        sl = pl.ds(hh * DA_VD, DA_VD)
        return ref.at[sl, :] if axis == 0 else ref.at[:, sl]

    def scores(i):
        hh, lane0, cols, n_frames, frame0 = tasks[i]
        qt = qt_ref[hh * DA_VD:(hh + 1) * DA_VD, lane0:lane0 + cols]
        _attn_scores(qt, head(k_ref, hh, 1), s_scr.at[i % 2], n_frames, frame0)

    scores(0)
    for i, (hh, lane0, cols, n_frames, frame0) in enumerate(tasks):
        if i + 1 < len(tasks):
            scores(i + 1)
        o = _attn_output(s_scr.at[i % 2], head(vt_ref, hh, 0), cols, n_frames, lam, lam_init, g)
        lanes = slice(hh * DA_VD, (hh + 1) * DA_VD)
        if n_frames:
            out_ref[N_META + frame0:N_META + frame0 + cols, lanes] = o.T.astype(BF16)
        else:
            out_ref[0:N_META, lanes] = o.T[0:N_META, :].astype(BF16)


def _attn(z, qt, vt, lam_p, g, layer, bsz):
    lam_init = 0.8 - 0.6 * math.exp(-0.3 * layer)
    m = bsz * L
    hw = ATT_HPS * DA_VD
    kb = Z_DK // hw

    def tspec():
        return pl.BlockSpec((None, hw, ATT_TW), lambda b, h: (b, h, 0))

    return pl.pallas_call(
        functools.partial(_attn_kernel, lam_init),
        grid=(bsz, DA_HEADS // ATT_HPS),
        in_specs=[
            tspec(),
            pl.BlockSpec((L, hw), lambda b, h: (b, kb + h)),
            tspec(),
            pl.BlockSpec((4, DA_HD), lambda b, h: (0, 0)),
            pl.BlockSpec((DA_VD, 1), lambda b, h: (0, 0)),
        ],
        out_specs=pl.BlockSpec((L, hw), lambda b, h: (b, h)),
        out_shape=jax.ShapeDtypeStruct((m, DA_DIM), BF16),
        scratch_shapes=[pltpu.VMEM((2, 2, ATT_TW, ATT_QB), F32)],
        compiler_params=_params(("parallel", "parallel")),
        name="diff_attn",
    )(qt, z, vt, lam_p, g)


MERGE_NC = 1024


def _merge_kernel(tm, a_ref, ah_ref, hm_ref, hd_ref, gate_ref, h_ref, cw_ref, wa_ref, wm_ref, wd_ref, wo_ref,
                  g_ref, hn_ref, u_ref, mg_scr):
    ax, ab, ac = (slice(k * CONV_DIM, (k + 1) * CONV_DIM) for k in range(3))
    pos = _pos_in_batch(pl.program_id(0) * tm, tm)
    cur = a_ref[:, ac].astype(F32) * a_ref[:, ax].astype(F32)
    prev = ah_ref[:, ac].astype(F32) * ah_ref[:, ax].astype(F32)
    a_act = (a_ref[:, ab].astype(F32) * _causal_conv3(prev, cur, cw_ref[...], pos)).astype(BF16)
    hm = hm_ref[...]
    hd = hd_ref[...]
    for c0 in range(0, D_MODEL, MERGE_NC):
        cs = slice(c0, c0 + MERGE_NC)
        ga, gm, gd = (gate_ref[:, k * D_MODEL + c0:k * D_MODEL + c0 + MERGE_NC].astype(F32) for k in range(3))
        merged = (_sigmoid(ga) * _dot(a_act, wa_ref[:, cs])
                  + _sigmoid(gm) * _dot(hm, wm_ref[:, cs])
                  + _sigmoid(gd) * _dot(hd, wd_ref[:, cs]))
        mg_scr[:, cs] = merged.astype(BF16)
    hn = h_ref[...] + _dot(mg_scr[...], wo_ref[...])
    hn_ref[...] = hn
    u_ref[...] = _rms(hn, g_ref[...]).astype(BF16)


def _merge(z, hm, hd, h, layer, conv_a, w_br_a, w_br_m, w_br_d, w_out, g, tm=384):
    m = h.shape[0]
    a_w = 3 * CONV_DIM
    ab = Z_AX // a_w
    hb = tm // HALO

    def weight(rows):
        return _resident((None, rows, D_MODEL), lambda i: (layer, 0, 0))

    return pl.pallas_call(
        functools.partial(_merge_kernel, tm),
        grid=(m // tm,),
        in_specs=[
            pl.BlockSpec((tm, a_w), lambda i: (i, ab)),
            pl.BlockSpec((HALO, a_w), lambda i: (jnp.maximum(i * hb - 1, 0), ab)),
            pl.BlockSpec((tm, ML_DIM), lambda i: (i, 0)),
            pl.BlockSpec((tm, DA_DIM), lambda i: (i, 0)),
            pl.BlockSpec((tm, 3 * D_MODEL), lambda i: (i, 0)),
            pl.BlockSpec((tm, D_MODEL), lambda i: (i, 0)),
            _resident((3, CONV_DIM), lambda i: (0, 0)),
            weight(CONV_DIM), weight(ML_DIM), weight(DA_DIM), weight(D_MODEL),
            _resident((1, D_MODEL), lambda i: (0, 0)),
        ],
        out_specs=[
            pl.BlockSpec((tm, D_MODEL), lambda i: (i, 0)),
            pl.BlockSpec((tm, D_MODEL), lambda i: (i, 0)),
        ],
        out_shape=[jax.ShapeDtypeStruct((m, D_MODEL), F32),
                   jax.ShapeDtypeStruct((m, D_MODEL), BF16)],
        scratch_shapes=[pltpu.VMEM((tm, D_MODEL), BF16)],
        compiler_params=_params(("parallel",)),
        name="merge_out_proj",
    )(z, z, hm, hd, z, h, conv_a, w_br_a, w_br_m, w_br_d, w_out, g)


FFN_ROWS = 2


def _row_blocks(tm, n):
    units = tm // HALO
    edges = [HALO * (units * k // n) for k in range(n + 1)]
    return list(zip(edges[:-1], edges[1:]))


def _ffn_up_kernel(tm, u_ref, uh_ref, wa_ref, wb_ref, cw_ref, cb_ref, out_ref, w_scr):
    @pl.when(pl.program_id(1) == 0)
    def _():
        w_scr[0] = wa_ref[...].astype(BF16)
        w_scr[1] = wb_ref[...].astype(BF16)

    r_tile = pl.program_id(1) * tm
    a_prev = _dot(uh_ref[...], w_scr[0])
    for r0, r1 in _row_blocks(tm, FFN_ROWS):
        u = u_ref[r0:r1, :]
        a = _dot(u, w_scr[0])
        conv = _causal_conv3(a_prev, a, cw_ref[...], _pos_in_batch(r_tile + r0, r1 - r0)) + cb_ref[...]
        gelu = 0.5 * conv * (1.0 + lax.erf(conv * (2.0 ** -0.5)))
        out_ref[r0:r1, :] = (gelu * _dot(u, w_scr[1])).astype(BF16)
        a_prev = a[r1 - r0 - HALO:, :]


def _ffn_up(u, w_up, layer, conv_w, conv_b, tm=L, tn=512):
    m = u.shape[0]
    nb = D_FF // tn
    hb = tm // HALO
    return pl.pallas_call(
        functools.partial(_ffn_up_kernel, tm),
        grid=(nb, m // tm),
        in_specs=[
            pl.BlockSpec((tm, D_MODEL), lambda j, i: (i, 0)),
            pl.BlockSpec((HALO, D_MODEL), lambda j, i: (jnp.maximum(i * hb - 1, 0), 0)),
            pl.BlockSpec((None, D_MODEL, tn), lambda j, i: (layer, 0, j)),
            pl.BlockSpec((None, D_MODEL, tn), lambda j, i: (layer, 0, nb + j)),
            pl.BlockSpec((3, tn), lambda j, i: (0, j)),
            pl.BlockSpec((1, tn), lambda j, i: (0, j)),
        ],
        out_specs=pl.BlockSpec((tm, tn), lambda j, i: (i, j)),
        out_shape=jax.ShapeDtypeStruct((m, D_FF), BF16),
        scratch_shapes=[pltpu.VMEM((2, D_MODEL, tn), BF16)],
        compiler_params=_params(("parallel", "arbitrary")),
        name="ffn_up",
    )(u, u, w_up, w_up, conv_w, conv_b)


def _ffn_down_kernel(final, act_ref, w_ref, h_ref, g_ref, *out_refs):
    hn = h_ref[...] + _dot(act_ref[...], w_ref[...])
    y = _rms(hn, g_ref[...])
    if final:
        out_refs[0][...] = y
    else:
        out_refs[0][...] = hn
        out_refs[1][...] = y.astype(BF16)


def _ffn_down(act, w_down, layer, h, g, final):
    m = h.shape[0]
    if final:
        tm = 256
        m = m // L * SEQ
        per_batch = SEQ // tm

        def stream_row(i):
            return pl.multiple_of(i * tm + N_META * (i // per_batch + 1), N_META)

        act_spec = pl.BlockSpec((pl.Element(tm), pl.Element(D_FF)), lambda i: (stream_row(i), 0))
        h_spec = pl.BlockSpec((pl.Element(tm), pl.Element(D_MODEL)), lambda i: (stream_row(i), 0))
        row = pl.BlockSpec((tm, D_MODEL), lambda i: (i, 0))
        out_specs = [row]
        out_shape = [jax.ShapeDtypeStruct((m, D_MODEL), F32)]
    else:
        tm = 384
        row = pl.BlockSpec((tm, D_MODEL), lambda i: (i, 0))
        act_spec = pl.BlockSpec((tm, D_FF), lambda i: (i, 0))
        h_spec = row
        out_specs = [row, row]
        out_shape = [jax.ShapeDtypeStruct((m, D_MODEL), F32),
                     jax.ShapeDtypeStruct((m, D_MODEL), BF16)]
    return pl.pallas_call(
        functools.partial(_ffn_down_kernel, final),
        grid=(m // tm,),
        in_specs=[
            act_spec,
            _resident((None, D_FF, D_MODEL), lambda i: (layer, 0, 0)),
            h_spec,
            _resident((1, D_MODEL), lambda i: (0, 0)),
        ],
        out_specs=out_specs,
        out_shape=out_shape,
        compiler_params=_params(("parallel",)),
        name="ffn_down_final" if final else "ffn_down",
    )(act, w_down, h, g)


def kernel(x, meta, norm_mix, w_in, conv_a, b_if, ml_norm, da_lambda, da_norm, w_br_a, w_br_m, w_br_d,
           w_out, norm_ffn, w_up, conv_ffn, conv_ffn_b, w_down, norm_f):
    bsz, seq, d = x.shape
    assert (seq, d) == (SEQ, D_MODEL)
    depth = w_in.shape[0]
    w_in_t = jnp.swapaxes(w_in, 1, 2).reshape(depth * W_IN, D_MODEL)
    wa, wm, wd, wo, wdn = (w.astype(BF16) for w in (w_br_a, w_br_m, w_br_d, w_out, w_down))
    h, u = _embed(x.reshape(bsz * SEQ, D_MODEL), meta, norm_mix[0][None], bsz)
    for i in range(depth):
        z = _inproj(u, w_in_t, i)
        qt, vt, zift = _attn_proj(u, w_in_t, i, bsz)
        bias = jnp.broadcast_to(b_if[i].reshape(2 * ML_HEADS, 1), (2 * ML_HEADS, LANES))
        hm = _mlstm(z, zift, bias, ml_norm[i][None], bsz)
        hd = _attn(z, qt, vt, da_lambda[i], da_norm[i][:, None], i, bsz)
        h, u = _merge(z, hm, hd, h, i, conv_a[i], wa, wm, wd, wo, norm_ffn[i][None])
        act = _ffn_up(u, w_up, i, conv_ffn[i], conv_ffn_b[i][None])
        final = i == depth - 1
        g_next = norm_f if final else norm_mix[i + 1]
        outs = _ffn_down(act, wdn, i, h, g_next[None], final)
        if final:
            y = outs[0]
        else:
            h, u = outs
    return y.reshape(bsz, SEQ, D_MODEL)
```

```python
import functools
import math

import jax
import jax.numpy as jnp
from jax import lax
from jax.experimental import pallas as pl
from jax.experimental.pallas import tpu as pltpu

F32 = jnp.float32
BF16 = jnp.bfloat16

D_MODEL = 2048
SEQ = 2048
N_META = 16
L = N_META + SEQ
CHUNK = 64
EPS = 1e-6

CONV_DIM = 512
ML_HEADS = 4
ML_DK = 256
ML_DV = 256
ML_DIM = ML_HEADS * ML_DV
DA_HEADS = 4
DA_HD = 64
DA_VD = 2 * DA_HD
DA_DIM = DA_HEADS * DA_VD
D_FF = 5632

Z_AX = 3 * D_MODEL
Z_DK = Z_AX + 3 * CONV_DIM
Z_MQ = Z_DK + DA_DIM
Z_N = Z_MQ + 4 * ML_DIM

LANES = 128
ML_T = LANES
ML_EXT = ML_DV + LANES
HALO = 16
ATT_QB = 256
EMB_T = L // 3
VMEM_LIMIT = 56 * 1024 * 1024


def _dot(a, b):
    return jnp.dot(a, b, preferred_element_type=F32)


def _dot_nt(a, b):
    return lax.dot_general(a, b, (((1,), (1,)), ((), ())), preferred_element_type=F32)


def _bdot(a, b, ca, cb):
    return lax.dot_general(a, b, (((ca,), (cb,)), ((0,), (0,))), preferred_element_type=F32)


def _rms(x, g):
    return x * lax.rsqrt(jnp.mean(x * x, axis=-1, keepdims=True) + EPS) * g


def _sigmoid(x):
    return 1.0 / (1.0 + jnp.exp(-x))


def _pos_in_batch(r0, tm):
    pos = r0 % L + lax.broadcasted_iota(jnp.int32, (tm, 1), 0)
    return jnp.where(pos >= L, pos - L, pos)


def _causal_conv3(prev, cur, w, pos):
    cc = jnp.concatenate([prev, cur], axis=0)
    x1 = jnp.where(pos >= 1, pltpu.roll(cc, 1, 0)[HALO:], 0.0)
    x2 = jnp.where(pos >= 2, pltpu.roll(cc, 2, 0)[HALO:], 0.0)
    return w[0:1] * x2 + w[1:2] * x1 + w[2:3] * cur


def _params(sem):
    return pltpu.CompilerParams(dimension_semantics=sem, vmem_limit_bytes=VMEM_LIMIT)


def _resident(shape, index_map):
    return pl.BlockSpec(shape, index_map, pipeline_mode=pl.Buffered(1))


def _embed_kernel(x_ref, meta_ref, g_ref, h_ref, u_ref):
    j = pl.program_id(1)

    @pl.when(j == 0)
    def _():
        rows = jnp.concatenate([meta_ref[...], x_ref[0:EMB_T - N_META, :]], axis=0)
        h_ref[...] = rows
        u_ref[...] = _rms(rows, g_ref[...]).astype(BF16)

    @pl.when(j > 0)
    def _():
        rows = x_ref[...]
        h_ref[...] = rows
        u_ref[...] = _rms(rows, g_ref[...]).astype(BF16)


def _embed(x2d, meta, g, bsz):
    nb = L // EMB_T
    m = bsz * L
    return pl.pallas_call(
        _embed_kernel,
        grid=(bsz, nb),
        in_specs=[
            pl.BlockSpec((pl.Element(EMB_T), pl.Element(D_MODEL)),
                         lambda b, j: (pl.multiple_of(b * SEQ + jnp.maximum(j * EMB_T - N_META, 0), 16), 0)),
            pl.BlockSpec((N_META, D_MODEL), lambda b, j: (0, 0)),
            pl.BlockSpec((1, D_MODEL), lambda b, j: (0, 0)),
        ],
        out_specs=[
            pl.BlockSpec((EMB_T, D_MODEL), lambda b, j: (b * nb + j, 0)),
            pl.BlockSpec((EMB_T, D_MODEL), lambda b, j: (b * nb + j, 0)),
        ],
        out_shape=[jax.ShapeDtypeStruct((m, D_MODEL), F32),
                   jax.ShapeDtypeStruct((m, D_MODEL), BF16)],
        compiler_params=_params(("parallel", "arbitrary")),
        name="embed_norm",
    )(x2d, meta, g)


W_A_END = 3 * CONV_DIM
W_M_END = W_A_END + 4 * ML_DIM
W_IF_END = W_M_END + 2 * ML_HEADS
W_D_END = W_IF_END + 3 * DA_DIM
W_IN = W_D_END + 3 * D_MODEL


IN_SRC = 512
IN_NSRC = 2


def _inproj_kernel(u_ref, *refs):
    wt_refs, z_ref = refs[:IN_NSRC], refs[IN_NSRC]
    u = u_ref[...]
    for k, wt_ref in enumerate(wt_refs):
        z_ref[:, k * IN_SRC:(k + 1) * IN_SRC] = _dot_nt(u, wt_ref[...].astype(BF16)).astype(BF16)


def _inproj(u, w_in_t, layer, tm=L):
    m = u.shape[0]
    tn = IN_NSRC * IN_SRC
    n_gate, n_a = 3 * D_MODEL // IN_SRC, 3 * CONV_DIM // IN_SRC

    def src_row(t):
        gate = W_D_END + IN_SRC * t
        cnv = IN_SRC * (t - n_gate)
        att = W_IF_END + DA_DIM + IN_SRC * (t - n_gate - n_a)
        mls = W_A_END + IN_SRC * (t - n_gate - n_a - 1)
        row = jnp.where(t < n_gate, gate,
                        jnp.where(t < n_gate + n_a, cnv, jnp.where(t < n_gate + n_a + 1, att, mls)))
        return pl.multiple_of(layer * W_IN + row, 8)

    def src_spec(k):
        return pl.BlockSpec((pl.Element(IN_SRC), pl.Element(D_MODEL)),
                            lambda i, j: (src_row(IN_NSRC * j + k), 0))

    return pl.pallas_call(
        _inproj_kernel,
        grid=(m // tm, Z_N // tn),
        in_specs=[pl.BlockSpec((tm, D_MODEL), lambda i, j: (i, 0))] + [src_spec(k) for k in range(IN_NSRC)],
        out_specs=pl.BlockSpec((tm, tn), lambda i, j: (i, j)),
        out_shape=jax.ShapeDtypeStruct((m, Z_N), BF16),
        compiler_params=_params(("parallel", "arbitrary")),
        name="in_proj",
    )(u, *([w_in_t] * IN_NSRC))


ML_NCH = 1 + SEQ // ML_T
ML_UNROLL = 2


def _mlstm_gates(ift_ref, bias_ref, ab_scr, cm_scr):
    t = ML_T
    ninf = -jnp.inf
    bias = bias_ref[...]
    starts = [SEQ] + [c * t for c in range(ML_NCH - 1)]
    gt = jnp.concatenate([ift_ref[0:8, l0:l0 + t] + bias for l0 in starts], axis=0)
    shape = (8 * ML_NCH, t)
    lane = lax.broadcasted_iota(jnp.int32, shape, 1)
    row = lax.broadcasted_iota(jnp.int32, shape, 0)
    is_i = (row & 7) < ML_HEADS
    log_sig = jnp.minimum(gt, 0.0) - jnp.log1p(jnp.exp(-jnp.abs(gt)))
    lg = jnp.where(is_i, gt, log_sig)
    weightless = jnp.logical_and(row < 8, lane >= N_META)
    lg = jnp.where(weightless, jnp.where(is_i, ninf, 0.0), lg)
    cs = jnp.where(is_i, 0.0, lg)
    sh = 1
    while sh < t:
        cs = cs + jnp.where(lane >= sh, pltpu.roll(cs, sh, 1), 0.0)
        sh *= 2
    a = jnp.where(is_i, lg - pltpu.roll(cs, 8 * ML_NCH - ML_HEADS, 0), ninf)
    cm = a
    sh = 1
    while sh < t:
        cm = jnp.maximum(cm, jnp.where(lane >= sh, pltpu.roll(cm, sh, 1), ninf))
        sh *= 2
    ab_scr[...] = jnp.where(is_i, a, cs)
    cm_scr[...] = cm


def _mlstm_chunk(c, r0, n_store, q_ref, k_ref, v_ref, o_ref, g_ref, out_ref, s_ref, m_ref, ab_scr, cm_scr):
    t = ML_T
    rows = pl.ds(r0, t)
    row0 = c * 8 if isinstance(c, int) else pl.multiple_of(c * 8, 8)
    ab = ab_scr[pl.ds(row0, 8), :]
    cmr = cm_scr[pl.ds(row0, 8), :]

    tri = lax.broadcasted_iota(jnp.int32, (t, t), 0) >= lax.broadcasted_iota(jnp.int32, (t, t), 1)
    nh = ML_HEADS

    def heads(x, d):
        return jnp.stack([x[:, h * d:(h + 1) * d] for h in range(nh)], axis=0)

    def row_of(x, r):
        return jnp.stack([x[r + h:r + h + 1, :] for h in range(nh)], axis=0)

    def lane_rep(x, r):
        return jnp.stack([jnp.broadcast_to(x[r + h:r + h + 1, :], (t, t)).T for h in range(nh)], axis=0)

    def widen(x, n):
        return jnp.concatenate([x] * n, axis=2)

    q = heads(q_ref[rows, :], ML_DK)
    k = heads(k_ref[rows, :], ML_DK) * jnp.asarray(ML_DK ** -0.5, BF16)
    vext = jnp.concatenate([heads(v_ref[rows, :], ML_DV), jnp.ones((nh, t, ML_EXT - ML_DV), BF16)], axis=2)
    a_r = row_of(ab, 0)
    a_l, b_l, cm_l = lane_rep(ab, 0), lane_rep(ab, nh), lane_rep(cmr, 0)
    m_prev = m_ref[...][:, :, 0:1]

    m_l = jnp.maximum(cm_l, m_prev)
    w = jnp.where(tri[None], jnp.exp(a_r - m_l), 0.0)
    g_l = jnp.exp(m_prev - m_l)
    s = _bdot(q, k, 2, 2) * w
    tot = widen(g_l, 3) * _bdot(q, s_ref[...].astype(BF16), 2, 1) + _bdot(s.astype(BF16), vext, 2, 1)
    scale_l = 1.0 / jnp.maximum(jnp.abs(tot[:, :, ML_DV:]), jnp.exp(-(b_l + m_l)))
    hh = tot[:, :, :ML_DV] * widen(scale_l, 2)
    ssq_l = _bdot((hh * hh).astype(BF16), jnp.ones((nh, ML_DV, LANES), BF16), 2, 1)
    hn = hh * widen(lax.rsqrt(ssq_l * (1.0 / ML_DV) + EPS), 2)
    for h in range(nh):
        sl = slice(h * ML_DV, (h + 1) * ML_DV)
        res = (hn[h] * g_ref[:, sl] * _sigmoid(o_ref[rows, sl].astype(F32))).astype(BF16)
        if n_store < t:
            out_ref[pl.ds(r0, n_store), sl] = res[:n_store, :]
        else:
            out_ref[rows, sl] = res

    b_last = row_of(ab, nh)[:, :, t - 1:t]
    m_x = jnp.maximum(m_prev, row_of(cmr, 0)[:, :, t - 1:t])
    vw = vext * widen(jnp.exp(a_l - m_x).astype(BF16), 3)
    s_ref[...] = jnp.exp(m_prev - m_x) * s_ref[...] + _bdot(k, vw, 1, 1)
    m_ref[...] = jnp.broadcast_to(b_last + m_x, (nh, 1, LANES))


def _mlstm_kernel(q_ref, k_ref, v_ref, o_ref, if_ref, bias_ref, g_ref, out_ref, s_ref, m_ref, ab_scr, cm_scr):
    s_ref[...] = jnp.zeros_like(s_ref)
    m_ref[...] = jnp.zeros_like(m_ref)
    _mlstm_gates(if_ref, bias_ref, ab_scr, cm_scr)
    refs = (q_ref, k_ref, v_ref, o_ref, g_ref, out_ref, s_ref, m_ref, ab_scr, cm_scr)

    _mlstm_chunk(0, 0, N_META, *refs)

    def body(i, carry):
        for k in range(ML_UNROLL):
            c = 1 + ML_UNROLL * i + k
            _mlstm_chunk(c, pl.multiple_of(N_META + (c - 1) * ML_T, N_META), ML_T, *refs)
        return carry

    lax.fori_loop(0, (ML_NCH - 1) // ML_UNROLL, body, 0)


def _mlstm(z, zift, bias, g, bsz):
    m = bsz * L
    qb = Z_MQ // ML_DIM

    def zspec(off):
        return pl.BlockSpec((L, ML_DIM), lambda b: (b, qb + off))

    return pl.pallas_call(
        _mlstm_kernel,
        grid=(bsz,),
        in_specs=[
            zspec(0), zspec(1), zspec(2), zspec(3),
            pl.BlockSpec((None, IF_ROWS, ATT_TW), lambda b: (b, 0, 0)),
            pl.BlockSpec((2 * ML_HEADS, LANES), lambda b: (0, 0)),
            pl.BlockSpec((1, ML_DIM), lambda b: (0, 0)),
        ],
        out_specs=pl.BlockSpec((L, ML_DIM), lambda b: (b, 0)),
        out_shape=jax.ShapeDtypeStruct((m, ML_DIM), BF16),
        scratch_shapes=[pltpu.VMEM((ML_HEADS, ML_DK, ML_EXT), F32),
                        pltpu.VMEM((ML_HEADS, 1, LANES), F32),
                        pltpu.VMEM((8 * ML_NCH, ML_T), F32),
                        pltpu.VMEM((8 * ML_NCH, ML_T), F32)],
        compiler_params=_params(("parallel",)),
        name="mlstm",
    )(z, z, z, z, zift, bias, g)


ATT_MB = LANES
ATT_TW = SEQ + ATT_MB
ATT_HPS = 2


IF_ROWS = 16


def _attn_proj_kernel(u_ref, wq_ref, wv_ref, wif_ref, qt_ref, vt_ref, zift_ref):
    u_fr = u_ref[N_META:, :]
    u_meta = u_ref[0:ATT_MB, :]
    scale = DA_HD ** -0.5 * math.log2(math.e)
    wqif = jnp.concatenate([wq_ref[...].astype(BF16), wif_ref[...].astype(BF16)], axis=0)
    for lanes, u_part in ((slice(0, SEQ), u_fr), (slice(SEQ, ATT_TW), u_meta)):
        res = _dot_nt(wqif, u_part)
        qt_ref[:, lanes] = (res[:DA_DIM, :] * scale).astype(BF16)
        zift_ref[:, lanes] = res[DA_DIM:, :]
        vt_ref[:, lanes] = _dot_nt(wv_ref[...].astype(BF16), u_part).astype(BF16)


def _attn_proj(u, w_in_t, layer, bsz):
    def wspec(rows, row0):
        return _resident((pl.Element(rows), pl.Element(D_MODEL)), lambda b: (layer * W_IN + row0, 0))

    out = pl.BlockSpec((None, DA_DIM, ATT_TW), lambda b: (b, 0, 0))
    shape = jax.ShapeDtypeStruct((bsz, DA_DIM, ATT_TW), BF16)
    return pl.pallas_call(
        _attn_proj_kernel,
        grid=(bsz,),
        in_specs=[pl.BlockSpec((L, D_MODEL), lambda b: (b, 0)),
                  wspec(DA_DIM, W_IF_END), wspec(DA_DIM, W_IF_END + 2 * DA_DIM), wspec(IF_ROWS, W_M_END)],
        out_specs=[out, out, pl.BlockSpec((None, IF_ROWS, ATT_TW), lambda b: (b, 0, 0))],
        out_shape=[shape, shape, jax.ShapeDtypeStruct((bsz, IF_ROWS, ATT_TW), F32)],
        compiler_params=_params(("parallel",)),
        name="attn_qv_proj",
    )(u, w_in_t, w_in_t, w_in_t)


def _attn_scores(qt, k_ref, s_ref, n_frames, frame0):
    cols = qt.shape[1]
    ninf = -jnp.inf
    first_map = lax.broadcasted_iota(jnp.int32, (DA_VD, cols), 0) < DA_HD
    k_meta = k_ref[0:ATT_MB, :]
    meta_ok = lax.broadcasted_iota(jnp.int32, (ATT_MB, 1), 0) < N_META
    if n_frames:
        k_fr = k_ref[N_META:N_META + n_frames, :]
        qf = frame0 + lax.broadcasted_iota(jnp.int32, (1, cols), 1)
        kend = (qf // CHUNK + 1) * CHUNK
        kf = frame0 + lax.broadcasted_iota(jnp.int32, (n_frames - frame0, 1), 0)
        diag_ok = kf < kend
    for mp in range(2):
        qm = jnp.where(first_map if mp == 0 else jnp.logical_not(first_map), qt, jnp.zeros_like(qt))
        s_ref[mp, 0:ATT_MB, 0:cols] = jnp.where(meta_ok, _dot(k_meta, qm), ninf)
        if n_frames:
            s_fr = _dot(k_fr, qm)
            if frame0:
                s_ref[mp, ATT_MB:ATT_MB + frame0, 0:cols] = s_fr[:frame0, :]
            s_ref[mp, ATT_MB + frame0:ATT_MB + n_frames, 0:cols] = jnp.where(diag_ok, s_fr[frame0:, :], ninf)


def _attn_output(s_ref, vt_ref, cols, n_frames, lam, lam_init, g):
    nk = ATT_MB + n_frames
    probs = []
    for mp in range(2):
        s = s_ref[mp, 0:nk, 0:cols]
        p = jnp.exp2(s - jnp.max(s, axis=0, keepdims=True))
        probs.append((p, jnp.sum(p, axis=0, keepdims=True)))
    inv_l1 = 1.0 / probs[0][1]
    a = (probs[0][0] - probs[1][0] * (lam * probs[0][1] / probs[1][1])).astype(BF16)
    o = _dot(vt_ref[:, SEQ:ATT_TW], a[:ATT_MB, :])
    if n_frames:
        o = o + _dot(vt_ref[:, 0:n_frames], a[ATT_MB:, :])
    o = o * inv_l1
    o = o * lax.rsqrt(jnp.mean(o * o, axis=0, keepdims=True) + EPS)
    return o * g * (1.0 - lam_init)


def _attn_kernel(lam_init, qt_ref, k_ref, vt_ref, lam_ref, g_ref, out_ref, s_scr):
    lf = lam_ref[...]
    lam = (jnp.exp(jnp.sum(lf[0:1] * lf[1:2], axis=1, keepdims=True))
           - jnp.exp(jnp.sum(lf[2:3] * lf[3:4], axis=1, keepdims=True)) + lam_init)
    g = g_ref[...]
    blocks = [(SEQ, ATT_MB, 0, 0)] + [(f0, ATT_QB, f0 + ATT_QB, f0) for f0 in range(0, SEQ, ATT_QB)]
    tasks = [(hh,) + blk for hh in range(ATT_HPS) for blk in blocks]

    def head(ref, hh, axis):
        sl = pl.ds(hh * DA_VD, DA_VD)
        return ref.at[sl, :] if axis == 0 else ref.at[:, sl]

    def scores(i):
        hh, lane0, cols, n_frames, frame0 = tasks[i]
        qt = qt_ref[hh * DA_VD:(hh + 1) * DA_VD, lane0:lane0 + cols]
        _attn_scores(qt, head(k_ref, hh, 1), s_scr.at[i % 2], n_frames, frame0)

    scores(0)
    for i, (hh, lane0, cols, n_frames, frame0) in enumerate(tasks):
        if i + 1 < len(tasks):
            scores(i + 1)
        o = _attn_output(s_scr.at[i % 2], head(vt_ref, hh, 0), cols, n_frames, lam, lam_init, g)
        lanes = slice(hh * DA_VD, (hh + 1) * DA_VD)
        if n_frames:
            out_ref[N_META + frame0:N_META + frame0 + cols, lanes] = o.T.astype(BF16)
        else:
            out_ref[0:N_META, lanes] = o.T[0:N_META, :].astype(BF16)


def _attn(z, qt, vt, lam_p, g, layer, bsz):
    lam_init = 0.8 - 0.6 * math.exp(-0.3 * layer)
    m = bsz * L
    hw = ATT_HPS * DA_VD
    kb = Z_DK // hw

    def tspec():
        return pl.BlockSpec((None, hw, ATT_TW), lambda b, h: (b, h, 0))

    return pl.pallas_call(
        functools.partial(_attn_kernel, lam_init),
        grid=(bsz, DA_HEADS // ATT_HPS),
        in_specs=[
            tspec(),
            pl.BlockSpec((L, hw), lambda b, h: (b, kb + h)),
            tspec(),
            pl.BlockSpec((4, DA_HD), lambda b, h: (0, 0)),
            pl.BlockSpec((DA_VD, 1), lambda b, h: (0, 0)),
        ],
        out_specs=pl.BlockSpec((L, hw), lambda b, h: (b, h)),
        out_shape=jax.ShapeDtypeStruct((m, DA_DIM), BF16),
        scratch_shapes=[pltpu.VMEM((2, 2, ATT_TW, ATT_QB), F32)],
        compiler_params=_params(("parallel", "parallel")),
        name="diff_attn",
    )(qt, z, vt, lam_p, g)


MERGE_NC = 2048


def _merge_kernel(tm, a_ref, ah_ref, hm_ref, hd_ref, gate_ref, h_ref, cw_ref, wa_ref, wm_ref, wd_ref, wo_ref,
                  g_ref, hn_ref, u_ref, mg_scr):
    ax, ab, ac = (slice(k * CONV_DIM, (k + 1) * CONV_DIM) for k in range(3))
    pos = _pos_in_batch(pl.program_id(0) * tm, tm)
    cur = a_ref[:, ac].astype(F32) * a_ref[:, ax].astype(F32)
    prev = ah_ref[:, ac].astype(F32) * ah_ref[:, ax].astype(F32)
    a_act = (a_ref[:, ab].astype(F32) * _causal_conv3(prev, cur, cw_ref[...], pos)).astype(BF16)
    hm = hm_ref[...]
    hd = hd_ref[...]
    for c0 in range(0, D_MODEL, MERGE_NC):
        cs = slice(c0, c0 + MERGE_NC)
        ga, gm, gd = (gate_ref[:, k * D_MODEL + c0:k * D_MODEL + c0 + MERGE_NC].astype(F32) for k in range(3))
        merged = (_sigmoid(ga) * _dot(a_act, wa_ref[:, cs])
                  + _sigmoid(gm) * _dot(hm, wm_ref[:, cs])
                  + _sigmoid(gd) * _dot(hd, wd_ref[:, cs]))
        mg_scr[:, cs] = merged.astype(BF16)
    hn = h_ref[...] + _dot(mg_scr[...], wo_ref[...])
    hn_ref[...] = hn
    u_ref[...] = _rms(hn, g_ref[...]).astype(BF16)


def _merge(z, hm, hd, h, layer, conv_a, w_br_a, w_br_m, w_br_d, w_out, g, tm=384):
    m = h.shape[0]
    a_w = 3 * CONV_DIM
    ab = Z_AX // a_w
    hb = tm // HALO

    def weight(rows):
        return _resident((None, rows, D_MODEL), lambda i: (layer, 0, 0))

    return pl.pallas_call(
        functools.partial(_merge_kernel, tm),
        grid=(m // tm,),
        in_specs=[
            pl.BlockSpec((tm, a_w), lambda i: (i, ab)),
            pl.BlockSpec((HALO, a_w), lambda i: (jnp.maximum(i * hb - 1, 0), ab)),
            pl.BlockSpec((tm, ML_DIM), lambda i: (i, 0)),
            pl.BlockSpec((tm, DA_DIM), lambda i: (i, 0)),
            pl.BlockSpec((tm, 3 * D_MODEL), lambda i: (i, 0)),
            pl.BlockSpec((tm, D_MODEL), lambda i: (i, 0)),
            _resident((3, CONV_DIM), lambda i: (0, 0)),
            weight(CONV_DIM), weight(ML_DIM), weight(DA_DIM), weight(D_MODEL),
            _resident((1, D_MODEL), lambda i: (0, 0)),
        ],
        out_specs=[
            pl.BlockSpec((tm, D_MODEL), lambda i: (i, 0)),
            pl.BlockSpec((tm, D_MODEL), lambda i: (i, 0)),
        ],
        out_shape=[jax.ShapeDtypeStruct((m, D_MODEL), F32),
                   jax.ShapeDtypeStruct((m, D_MODEL), BF16)],
        scratch_shapes=[pltpu.VMEM((tm, D_MODEL), BF16)],
        compiler_params=_params(("parallel",)),
        name="merge_out_proj",
    )(z, z, hm, hd, z, h, conv_a, w_br_a, w_br_m, w_br_d, w_out, g)


FFN_ROWS = 2


def _row_blocks(tm, n):
    units = tm // HALO
    edges = [HALO * (units * k // n) for k in range(n + 1)]
    return list(zip(edges[:-1], edges[1:]))


def _ffn_up_kernel(tm, u_ref, uh_ref, wa_ref, wb_ref, cw_ref, cb_ref, out_ref, w_scr):
    @pl.when(pl.program_id(1) == 0)
    def _():
        w_scr[0] = wa_ref[...].astype(BF16)
        w_scr[1] = wb_ref[...].astype(BF16)

    r_tile = pl.program_id(1) * tm
    a_prev = _dot(uh_ref[...], w_scr[0])
    for r0, r1 in _row_blocks(tm, FFN_ROWS):
        u = u_ref[r0:r1, :]
        a = _dot(u, w_scr[0])
        conv = _causal_conv3(a_prev, a, cw_ref[...], _pos_in_batch(r_tile + r0, r1 - r0)) + cb_ref[...]
        gelu = 0.5 * conv * (1.0 + lax.erf(conv * (2.0 ** -0.5)))
        out_ref[r0:r1, :] = (gelu * _dot(u, w_scr[1])).astype(BF16)
        a_prev = a[r1 - r0 - HALO:, :]


def _ffn_up(u, w_up, layer, conv_w, conv_b, tm=L, tn=512):
    m = u.shape[0]
    nb = D_FF // tn
    hb = tm // HALO
    return pl.pallas_call(
        functools.partial(_ffn_up_kernel, tm),
        grid=(nb, m // tm),
        in_specs=[
            pl.BlockSpec((tm, D_MODEL), lambda j, i: (i, 0)),
            pl.BlockSpec((HALO, D_MODEL), lambda j, i: (jnp.maximum(i * hb - 1, 0), 0)),
            pl.BlockSpec((None, D_MODEL, tn), lambda j, i: (layer, 0, j)),
            pl.BlockSpec((None, D_MODEL, tn), lambda j, i: (layer, 0, nb + j)),
            pl.BlockSpec((3, tn), lambda j, i: (0, j)),
            pl.BlockSpec((1, tn), lambda j, i: (0, j)),
        ],
        out_specs=pl.BlockSpec((tm, tn), lambda j, i: (i, j)),
        out_shape=jax.ShapeDtypeStruct((m, D_FF), BF16),
        scratch_shapes=[pltpu.VMEM((2, D_MODEL, tn), BF16)],
        compiler_params=_params(("parallel", "arbitrary")),
        name="ffn_up",
    )(u, u, w_up, w_up, conv_w, conv_b)


def _ffn_down_kernel(final, act_ref, w_ref, h_ref, g_ref, *out_refs):
    hn = h_ref[...] + _dot(act_ref[...], w_ref[...])
    y = _rms(hn, g_ref[...])
    if final:
        out_refs[0][...] = y
    else:
        out_refs[0][...] = hn
        out_refs[1][...] = y.astype(BF16)


def _ffn_down(act, w_down, layer, h, g, final):
    m = h.shape[0]
    if final:
        tm = 256
        m = m // L * SEQ
        per_batch = SEQ // tm

        def stream_row(i):
            return pl.multiple_of(i * tm + N_META * (i // per_batch + 1), N_META)

        act_spec = pl.BlockSpec((pl.Element(tm), pl.Element(D_FF)), lambda i: (stream_row(i), 0))
        h_spec = pl.BlockSpec((pl.Element(tm), pl.Element(D_MODEL)), lambda i: (stream_row(i), 0))
        row = pl.BlockSpec((tm, D_MODEL), lambda i: (i, 0))
        out_specs = [row]
        out_shape = [jax.ShapeDtypeStruct((m, D_MODEL), F32)]
    else:
        tm = 384
        row = pl.BlockSpec((tm, D_MODEL), lambda i: (i, 0))
        act_spec = pl.BlockSpec((tm, D_FF), lambda i: (i, 0))
        h_spec = row
        out_specs = [row, row]
        out_shape = [jax.ShapeDtypeStruct((m, D_MODEL), F32),
                     jax.ShapeDtypeStruct((m, D_MODEL), BF16)]
    return pl.pallas_call(
        functools.partial(_ffn_down_kernel, final),
        grid=(m // tm,),
        in_specs=[
            act_spec,
            _resident((None, D_FF, D_MODEL), lambda i: (layer, 0, 0)),
            h_spec,
            _resident((1, D_MODEL), lambda i: (0, 0)),
        ],
        out_specs=out_specs,
        out_shape=out_shape,
        compiler_params=_params(("parallel",)),
        name="ffn_down_final" if final else "ffn_down",
    )(act, w_down, h, g)


def kernel(x, meta, norm_mix, w_in, conv_a, b_if, ml_norm, da_lambda, da_norm, w_br_a, w_br_m, w_br_d,
           w_out, norm_ffn, w_up, conv_ffn, conv_ffn_b, w_down, norm_f):
    bsz, seq, d = x.shape
    assert (seq, d) == (SEQ, D_MODEL)
    depth = w_in.shape[0]
    w_in_t = jnp.swapaxes(w_in, 1, 2).reshape(depth * W_IN, D_MODEL)
    wa, wm, wd, wo, wdn = (w.astype(BF16) for w in (w_br_a, w_br_m, w_br_d, w_out, w_down))
    h, u = _embed(x.reshape(bsz * SEQ, D_MODEL), meta, norm_mix[0][None], bsz)
    for i in range(depth):
        z = _inproj(u, w_in_t, i)
        qt, vt, zift = _attn_proj(u, w_in_t, i, bsz)
        bias = jnp.broadcast_to(b_if[i].reshape(2 * ML_HEADS, 1), (2 * ML_HEADS, LANES))
        hm = _mlstm(z, zift, bias, ml_norm[i][None], bsz)
        hd = _attn(z, qt, vt, da_lambda[i], da_norm[i][:, None], i, bsz)
        h, u = _merge(z, hm, hd, h, i, conv_a[i], wa, wm, wd, wo, norm_ffn[i][None])
        act = _ffn_up(u, w_up, i, conv_ffn[i], conv_ffn_b[i][None])
        final = i == depth - 1
        g_next = norm_f if final else norm_mix[i + 1]
        outs = _ffn_down(act, wdn, i, h, g_next[None], final)
        if final:
            y = outs[0]
        else:
            h, u = outs
    return y.reshape(bsz, SEQ, D_MODEL)
```

```python
import functools
import math

import jax
import jax.numpy as jnp
from jax import lax
from jax.experimental import pallas as pl
from jax.experimental.pallas import tpu as pltpu

F32 = jnp.float32
BF16 = jnp.bfloat16

D_MODEL = 2048
SEQ = 2048
N_META = 16
L = N_META + SEQ
CHUNK = 64
EPS = 1e-6

CONV_DIM = 512
ML_HEADS = 4
ML_DK = 256
ML_DV = 256
ML_DIM = ML_HEADS * ML_DV
DA_HEADS = 4
DA_HD = 64
DA_VD = 2 * DA_HD
DA_DIM = DA_HEADS * DA_VD
D_FF = 5632

Z_AX = 3 * D_MODEL
Z_DK = Z_AX + 3 * CONV_DIM
Z_MQ = Z_DK + DA_DIM
Z_N = Z_MQ + 4 * ML_DIM

LANES = 128
ML_T = LANES
ML_EXT = ML_DV + LANES
HALO = 16
ATT_QB = 256
EMB_T = L // 3
VMEM_LIMIT = 56 * 1024 * 1024


def _dot(a, b):
    return jnp.dot(a, b, preferred_element_type=F32)


def _dot_nt(a, b):
    return lax.dot_general(a, b, (((1,), (1,)), ((), ())), preferred_element_type=F32)


def _bdot(a, b, ca, cb):
    return lax.dot_general(a, b, (((ca,), (cb,)), ((0,), (0,))), preferred_element_type=F32)


def _rms(x, g):
    return x * lax.rsqrt(jnp.mean(x * x, axis=-1, keepdims=True) + EPS) * g


def _sigmoid(x):
    return 1.0 / (1.0 + jnp.exp(-x))


def _pos_in_batch(r0, tm):
    pos = r0 % L + lax.broadcasted_iota(jnp.int32, (tm, 1), 0)
    return jnp.where(pos >= L, pos - L, pos)


def _causal_conv3(prev, cur, w, pos):
    cc = jnp.concatenate([prev, cur], axis=0)
    x1 = jnp.where(pos >= 1, pltpu.roll(cc, 1, 0)[HALO:], 0.0)
    x2 = jnp.where(pos >= 2, pltpu.roll(cc, 2, 0)[HALO:], 0.0)
    return w[0:1] * x2 + w[1:2] * x1 + w[2:3] * cur


def _params(sem):
    return pltpu.CompilerParams(dimension_semantics=sem, vmem_limit_bytes=VMEM_LIMIT)


def _resident(shape, index_map):
    return pl.BlockSpec(shape, index_map, pipeline_mode=pl.Buffered(1))


def _embed_kernel(x_ref, meta_ref, g_ref, h_ref, u_ref):
    j = pl.program_id(1)

    @pl.when(j == 0)
    def _():
        rows = jnp.concatenate([meta_ref[...], x_ref[0:EMB_T - N_META, :]], axis=0)
        h_ref[...] = rows
        u_ref[...] = _rms(rows, g_ref[...]).astype(BF16)

    @pl.when(j > 0)
    def _():
        rows = x_ref[...]
        h_ref[...] = rows
        u_ref[...] = _rms(rows, g_ref[...]).astype(BF16)


def _embed(x2d, meta, g, bsz):
    nb = L // EMB_T
    m = bsz * L
    return pl.pallas_call(
        _embed_kernel,
        grid=(bsz, nb),
        in_specs=[
            pl.BlockSpec((pl.Element(EMB_T), pl.Element(D_MODEL)),
                         lambda b, j: (pl.multiple_of(b * SEQ + jnp.maximum(j * EMB_T - N_META, 0), 16), 0)),
            pl.BlockSpec((N_META, D_MODEL), lambda b, j: (0, 0)),
            pl.BlockSpec((1, D_MODEL), lambda b, j: (0, 0)),
        ],
        out_specs=[
            pl.BlockSpec((EMB_T, D_MODEL), lambda b, j: (b * nb + j, 0)),
            pl.BlockSpec((EMB_T, D_MODEL), lambda b, j: (b * nb + j, 0)),
        ],
        out_shape=[jax.ShapeDtypeStruct((m, D_MODEL), F32),
                   jax.ShapeDtypeStruct((m, D_MODEL), BF16)],
        compiler_params=_params(("parallel", "arbitrary")),
        name="embed_norm",
    )(x2d, meta, g)


W_A_END = 3 * CONV_DIM
W_M_END = W_A_END + 4 * ML_DIM
W_IF_END = W_M_END + 2 * ML_HEADS
W_D_END = W_IF_END + 3 * DA_DIM
W_IN = W_D_END + 3 * D_MODEL


IN_SRC = 512
IN_NSRC = 2


def _inproj_kernel(u_ref, *refs):
    wt_refs, z_ref = refs[:IN_NSRC], refs[IN_NSRC]
    u = u_ref[...]
    for k, wt_ref in enumerate(wt_refs):
        z_ref[:, k * IN_SRC:(k + 1) * IN_SRC] = _dot_nt(u, wt_ref[...].astype(BF16)).astype(BF16)


def _inproj(u, w_in_t, layer, tm=L):
    m = u.shape[0]
    tn = IN_NSRC * IN_SRC
    n_gate, n_a = 3 * D_MODEL // IN_SRC, 3 * CONV_DIM // IN_SRC

    def src_row(t):
        gate = W_D_END + IN_SRC * t
        cnv = IN_SRC * (t - n_gate)
        att = W_IF_END + DA_DIM + IN_SRC * (t - n_gate - n_a)
        mls = W_A_END + IN_SRC * (t - n_gate - n_a - 1)
        row = jnp.where(t < n_gate, gate,
                        jnp.where(t < n_gate + n_a, cnv, jnp.where(t < n_gate + n_a + 1, att, mls)))
        return pl.multiple_of(layer * W_IN + row, 8)

    def src_spec(k):
        return pl.BlockSpec((pl.Element(IN_SRC), pl.Element(D_MODEL)),
                            lambda i, j: (src_row(IN_NSRC * j + k), 0))

    return pl.pallas_call(
        _inproj_kernel,
        grid=(m // tm, Z_N // tn),
        in_specs=[pl.BlockSpec((tm, D_MODEL), lambda i, j: (i, 0))] + [src_spec(k) for k in range(IN_NSRC)],
        out_specs=pl.BlockSpec((tm, tn), lambda i, j: (i, j)),
        out_shape=jax.ShapeDtypeStruct((m, Z_N), BF16),
        compiler_params=_params(("parallel", "arbitrary")),
        name="in_proj",
    )(u, *([w_in_t] * IN_NSRC))


ML_NCH = 1 + SEQ // ML_T
ML_UNROLL = 2


def _mlstm_gates(ift_ref, bias_ref, ab_scr, cm_scr):
    t = ML_T
    ninf = -jnp.inf
    bias = bias_ref[...]
    starts = [SEQ] + [c * t for c in range(ML_NCH - 1)]
    gt = jnp.concatenate([ift_ref[0:8, l0:l0 + t] + bias for l0 in starts], axis=0)
    shape = (8 * ML_NCH, t)
    lane = lax.broadcasted_iota(jnp.int32, shape, 1)
    row = lax.broadcasted_iota(jnp.int32, shape, 0)
    is_i = (row & 7) < ML_HEADS
    log_sig = jnp.minimum(gt, 0.0) - jnp.log1p(jnp.exp(-jnp.abs(gt)))
    lg = jnp.where(is_i, gt, log_sig)
    weightless = jnp.logical_and(row < 8, lane >= N_META)
    lg = jnp.where(weightless, jnp.where(is_i, ninf, 0.0), lg)
    cs = jnp.where(is_i, 0.0, lg)
    sh = 1
    while sh < t:
        cs = cs + jnp.where(lane >= sh, pltpu.roll(cs, sh, 1), 0.0)
        sh *= 2
    a = jnp.where(is_i, lg - pltpu.roll(cs, 8 * ML_NCH - ML_HEADS, 0), ninf)
    cm = a
    sh = 1
    while sh < t:
        cm = jnp.maximum(cm, jnp.where(lane >= sh, pltpu.roll(cm, sh, 1), ninf))
        sh *= 2
    ab_scr[...] = jnp.where(is_i, a, cs)
    cm_scr[...] = cm


def _mlstm_chunk(c, r0, n_store, q_ref, k_ref, v_ref, o_ref, g_ref, out_ref, s_ref, m_ref, ab_scr, cm_scr):
    t = ML_T
    rows = pl.ds(r0, t)
    row0 = c * 8 if isinstance(c, int) else pl.multiple_of(c * 8, 8)
    ab = ab_scr[pl.ds(row0, 8), :]
    cmr = cm_scr[pl.ds(row0, 8), :]

    tri = lax.broadcasted_iota(jnp.int32, (t, t), 0) >= lax.broadcasted_iota(jnp.int32, (t, t), 1)
    nh = ML_HEADS

    def heads(x, d):
        return jnp.stack([x[:, h * d:(h + 1) * d] for h in range(nh)], axis=0)

    def row_of(x, r):
        return jnp.stack([x[r + h:r + h + 1, :] for h in range(nh)], axis=0)

    def lane_rep(x, r):
        return jnp.stack([jnp.broadcast_to(x[r + h:r + h + 1, :], (t, t)).T for h in range(nh)], axis=0)

    def widen(x, n):
        return jnp.concatenate([x] * n, axis=2)

    q = heads(q_ref[rows, :], ML_DK)
    k = heads(k_ref[rows, :], ML_DK) * jnp.asarray(ML_DK ** -0.5, BF16)
    vext = jnp.concatenate([heads(v_ref[rows, :], ML_DV), jnp.ones((nh, t, ML_EXT - ML_DV), BF16)], axis=2)
    a_r = row_of(ab, 0)
    a_l, b_l, cm_l = lane_rep(ab, 0), lane_rep(ab, nh), lane_rep(cmr, 0)
    m_prev = m_ref[...][:, :, 0:1]

    m_l = jnp.maximum(cm_l, m_prev)
    w = jnp.where(tri[None], jnp.exp(a_r - m_l), 0.0)
    g_l = jnp.exp(m_prev - m_l)
    s = _bdot(q, k, 2, 2) * w
    tot = widen(g_l, 3) * _bdot(q, s_ref[...].astype(BF16), 2, 1) + _bdot(s.astype(BF16), vext, 2, 1)
    scale_l = 1.0 / jnp.maximum(jnp.abs(tot[:, :, ML_DV:]), jnp.exp(-(b_l + m_l)))
    hh = tot[:, :, :ML_DV] * widen(scale_l, 2)
    ssq_l = _bdot((hh * hh).astype(BF16), jnp.ones((nh, ML_DV, LANES), BF16), 2, 1)
    hn = hh * widen(lax.rsqrt(ssq_l * (1.0 / ML_DV) + EPS), 2)
    for h in range(nh):
        sl = slice(h * ML_DV, (h + 1) * ML_DV)
        res = (hn[h] * g_ref[:, sl] * _sigmoid(o_ref[rows, sl].astype(F32))).astype(BF16)
        if n_store < t:
            out_ref[pl.ds(r0, n_store), sl] = res[:n_store, :]
        else:
            out_ref[rows, sl] = res

    b_last = row_of(ab, nh)[:, :, t - 1:t]
    m_x = jnp.maximum(m_prev, row_of(cmr, 0)[:, :, t - 1:t])
    vw = vext * widen(jnp.exp(a_l - m_x).astype(BF16), 3)
    s_ref[...] = jnp.exp(m_prev - m_x) * s_ref[...] + _bdot(k, vw, 1, 1)
    m_ref[...] = jnp.broadcast_to(b_last + m_x, (nh, 1, LANES))


def _mlstm_kernel(q_ref, k_ref, v_ref, o_ref, if_ref, bias_ref, g_ref, out_ref, s_ref, m_ref, ab_scr, cm_scr):
    s_ref[...] = jnp.zeros_like(s_ref)
    m_ref[...] = jnp.zeros_like(m_ref)
    _mlstm_gates(if_ref, bias_ref, ab_scr, cm_scr)
    refs = (q_ref, k_ref, v_ref, o_ref, g_ref, out_ref, s_ref, m_ref, ab_scr, cm_scr)

    _mlstm_chunk(0, 0, N_META, *refs)

    def body(i, carry):
        for k in range(ML_UNROLL):
            c = 1 + ML_UNROLL * i + k
            _mlstm_chunk(c, pl.multiple_of(N_META + (c - 1) * ML_T, N_META), ML_T, *refs)
        return carry

    lax.fori_loop(0, (ML_NCH - 1) // ML_UNROLL, body, 0)


def _mlstm(z, zift, bias, g, bsz):
    m = bsz * L
    qb = Z_MQ // ML_DIM

    def zspec(off):
        return pl.BlockSpec((L, ML_DIM), lambda b: (b, qb + off))

    return pl.pallas_call(
        _mlstm_kernel,
        grid=(bsz,),
        in_specs=[
            zspec(0), zspec(1), zspec(2), zspec(3),
            pl.BlockSpec((None, IF_ROWS, ATT_TW), lambda b: (b, 0, 0)),
            pl.BlockSpec((2 * ML_HEADS, LANES), lambda b: (0, 0)),
            pl.BlockSpec((1, ML_DIM), lambda b: (0, 0)),
        ],
        out_specs=pl.BlockSpec((L, ML_DIM), lambda b: (b, 0)),
        out_shape=jax.ShapeDtypeStruct((m, ML_DIM), BF16),
        scratch_shapes=[pltpu.VMEM((ML_HEADS, ML_DK, ML_EXT), F32),
                        pltpu.VMEM((ML_HEADS, 1, LANES), F32),
                        pltpu.VMEM((8 * ML_NCH, ML_T), F32),
                        pltpu.VMEM((8 * ML_NCH, ML_T), F32)],
        compiler_params=_params(("parallel",)),
        name="mlstm",
    )(z, z, z, z, zift, bias, g)


ATT_MB = LANES
ATT_TW = SEQ + ATT_MB
ATT_HPS = 2


IF_ROWS = 16


def _attn_proj_kernel(u_ref, wq_ref, wv_ref, wif_ref, qt_ref, vt_ref, zift_ref):
    u_fr = u_ref[N_META:, :]
    u_meta = u_ref[0:ATT_MB, :]
    scale = DA_HD ** -0.5 * math.log2(math.e)
    wqif = jnp.concatenate([wq_ref[...].astype(BF16), wif_ref[...].astype(BF16)], axis=0)
    for lanes, u_part in ((slice(0, SEQ), u_fr), (slice(SEQ, ATT_TW), u_meta)):
        res = _dot_nt(wqif, u_part)
        qt_ref[:, lanes] = (res[:DA_DIM, :] * scale).astype(BF16)
        zift_ref[:, lanes] = res[DA_DIM:, :]
        vt_ref[:, lanes] = _dot_nt(wv_ref[...].astype(BF16), u_part).astype(BF16)


def _attn_proj(u, w_in_t, layer, bsz):
    def wspec(rows, row0):
        return _resident((pl.Element(rows), pl.Element(D_MODEL)), lambda b: (layer * W_IN + row0, 0))

    out = pl.BlockSpec((None, DA_DIM, ATT_TW), lambda b: (b, 0, 0))
    shape = jax.ShapeDtypeStruct((bsz, DA_DIM, ATT_TW), BF16)
    return pl.pallas_call(
        _attn_proj_kernel,
        grid=(bsz,),
        in_specs=[pl.BlockSpec((L, D_MODEL), lambda b: (b, 0)),
                  wspec(DA_DIM, W_IF_END), wspec(DA_DIM, W_IF_END + 2 * DA_DIM), wspec(IF_ROWS, W_M_END)],
        out_specs=[out, out, pl.BlockSpec((None, IF_ROWS, ATT_TW), lambda b: (b, 0, 0))],
        out_shape=[shape, shape, jax.ShapeDtypeStruct((bsz, IF_ROWS, ATT_TW), F32)],
        compiler_params=_params(("parallel",)),
        name="attn_qv_proj",
    )(u, w_in_t, w_in_t, w_in_t)


def _attn_scores(qt, k_ref, s_ref, n_frames, frame0):
    cols = qt.shape[1]
    ninf = -jnp.inf
    first_map = lax.broadcasted_iota(jnp.int32, (DA_VD, cols), 0) < DA_HD
    k_meta = k_ref[0:ATT_MB, :]
    meta_ok = lax.broadcasted_iota(jnp.int32, (ATT_MB, 1), 0) < N_META
    if n_frames:
        k_fr = k_ref[N_META:N_META + n_frames, :]
        qf = frame0 + lax.broadcasted_iota(jnp.int32, (1, cols), 1)
        kend = (qf // CHUNK + 1) * CHUNK
        kf = frame0 + lax.broadcasted_iota(jnp.int32, (n_frames - frame0, 1), 0)
        diag_ok = kf < kend
    for mp in range(2):
        qm = jnp.where(first_map if mp == 0 else jnp.logical_not(first_map), qt, jnp.zeros_like(qt))
        s_ref[mp, 0:ATT_MB, 0:cols] = jnp.where(meta_ok, _dot(k_meta, qm), ninf)
        if n_frames:
            s_fr = _dot(k_fr, qm)
            if frame0:
                s_ref[mp, ATT_MB:ATT_MB + frame0, 0:cols] = s_fr[:frame0, :]
            s_ref[mp, ATT_MB + frame0:ATT_MB + n_frames, 0:cols] = jnp.where(diag_ok, s_fr[frame0:, :], ninf)


def _attn_output(s_ref, vt_ref, cols, n_frames, lam, lam_init, g):
    nk = ATT_MB + n_frames
    probs = []
    for mp in range(2):
        s = s_ref[mp, 0:nk, 0:cols]
        p = jnp.exp2(s - jnp.max(s, axis=0, keepdims=True))
        probs.append((p, jnp.sum(p, axis=0, keepdims=True)))
    inv_l1 = 1.0 / probs[0][1]
    a = (probs[0][0] - probs[1][0] * (lam * probs[0][1] / probs[1][1])).astype(BF16)
    o = _dot(vt_ref[:, SEQ:ATT_TW], a[:ATT_MB, :])
    if n_frames:
        o = o + _dot(vt_ref[:, 0:n_frames], a[ATT_MB:, :])
    o = o * inv_l1
    o = o * lax.rsqrt(jnp.mean(o * o, axis=0, keepdims=True) + EPS)
    return o * g * (1.0 - lam_init)


def _attn_kernel(lam_init, qt_ref, k_ref, vt_ref, lam_ref, g_ref, out_ref, s_scr):
    lf = lam_ref[...]
    lam = (jnp.exp(jnp.sum(lf[0:1] * lf[1:2], axis=1, keepdims=True))
           - jnp.exp(jnp.sum(lf[2:3] * lf[3:4], axis=1, keepdims=True)) + lam_init)
    g = g_ref[...]
    blocks = [(SEQ, ATT_MB, 0, 0)] + [(f0, ATT_QB, f0 + ATT_QB, f0) for f0 in range(0, SEQ, ATT_QB)]
    tasks = [(hh,) + blk for hh in range(ATT_HPS) for blk in blocks]

    def head(ref, hh, axis):
        sl = pl.ds(hh * DA_VD, DA_VD)
        return ref.at[sl, :] if axis == 0 else ref.at[:, sl]

    def scores(i):
        hh, lane0, cols, n_frames, frame0 = tasks[i]
        qt = qt_ref[hh * DA_VD:(hh + 1) * DA_VD, lane0:lane0 + cols]
        _attn_scores(qt, head(k_ref, hh, 1), s_scr.at[i % 2], n_frames, frame0)

    scores(0)
    for i, (hh, lane0, cols, n_frames, frame0) in enumerate(tasks):
        if i + 1 < len(tasks):
            scores(i + 1)
        o = _attn_output(s_scr.at[i % 2], head(vt_ref, hh, 0), cols, n_frames, lam, lam_init, g)
        lanes = slice(hh * DA_VD, (hh + 1) * DA_VD)
        if n_frames:
            out_ref[N_META + frame0:N_META + frame0 + cols, lanes] = o.T.astype(BF16)
        else:
            out_ref[0:N_META, lanes] = o.T[0:N_META, :].astype(BF16)


def _attn(z, qt, vt, lam_p, g, layer, bsz):
    lam_init = 0.8 - 0.6 * math.exp(-0.3 * layer)
    m = bsz * L
    hw = ATT_HPS * DA_VD
    kb = Z_DK // hw

    def tspec():
        return pl.BlockSpec((None, hw, ATT_TW), lambda b, h: (b, h, 0))

    return pl.pallas_call(
        functools.partial(_attn_kernel, lam_init),
        grid=(bsz, DA_HEADS // ATT_HPS),
        in_specs=[
            tspec(),
            pl.BlockSpec((L, hw), lambda b, h: (b, kb + h)),
            tspec(),
            pl.BlockSpec((4, DA_HD), lambda b, h: (0, 0)),
            pl.BlockSpec((DA_VD, 1), lambda b, h: (0, 0)),
        ],
        out_specs=pl.BlockSpec((L, hw), lambda b, h: (b, h)),
        out_shape=jax.ShapeDtypeStruct((m, DA_DIM), BF16),
        scratch_shapes=[pltpu.VMEM((2, 2, ATT_TW, ATT_QB), F32)],
        compiler_params=_params(("parallel", "parallel")),
        name="diff_attn",
    )(qt, z, vt, lam_p, g)


def _merge_kernel(tm, a_ref, ah_ref, hm_ref, hd_ref, gate_ref, h_ref, cw_ref, wa_ref, wm_ref, wd_ref, wo_ref,
                  g_ref, hn_ref, u_ref, mg_scr):
    ax, ab, ac = (slice(k * CONV_DIM, (k + 1) * CONV_DIM) for k in range(3))
    pos = _pos_in_batch(pl.program_id(0) * tm, tm)
    cur = a_ref[:, ac].astype(F32) * a_ref[:, ax].astype(F32)
    prev = ah_ref[:, ac].astype(F32) * ah_ref[:, ax].astype(F32)
    a_act = (a_ref[:, ab].astype(F32) * _causal_conv3(prev, cur, cw_ref[...], pos)).astype(BF16)
    ga, gm, gd = (gate_ref[:, k * D_MODEL:(k + 1) * D_MODEL].astype(F32) for k in range(3))
    merged = (_sigmoid(ga) * _dot(a_act, wa_ref[...])
              + _sigmoid(gm) * _dot(hm_ref[...], wm_ref[...])
              + _sigmoid(gd) * _dot(hd_ref[...], wd_ref[...]))
    mg_scr[...] = merged.astype(BF16)
    hn = h_ref[...] + _dot(mg_scr[...], wo_ref[...])
    hn_ref[...] = hn
    u_ref[...] = _rms(hn, g_ref[...]).astype(BF16)


def _merge(z, hm, hd, h, layer, conv_a, w_br_a, w_br_m, w_br_d, w_out, g, tm=384):
    m = h.shape[0]
    a_w = 3 * CONV_DIM
    ab = Z_AX // a_w
    hb = tm // HALO

    def weight(rows):
        return _resident((None, rows, D_MODEL), lambda i: (layer, 0, 0))

    return pl.pallas_call(
        functools.partial(_merge_kernel, tm),
        grid=(m // tm,),
        in_specs=[
            pl.BlockSpec((tm, a_w), lambda i: (i, ab)),
            pl.BlockSpec((HALO, a_w), lambda i: (jnp.maximum(i * hb - 1, 0), ab)),
            pl.BlockSpec((tm, ML_DIM), lambda i: (i, 0)),
            pl.BlockSpec((tm, DA_DIM), lambda i: (i, 0)),
            pl.BlockSpec((tm, 3 * D_MODEL), lambda i: (i, 0)),
            pl.BlockSpec((tm, D_MODEL), lambda i: (i, 0)),
            _resident((3, CONV_DIM), lambda i: (0, 0)),
            weight(CONV_DIM), weight(ML_DIM), weight(DA_DIM), weight(D_MODEL),
            _resident((1, D_MODEL), lambda i: (0, 0)),
        ],
        out_specs=[
            pl.BlockSpec((tm, D_MODEL), lambda i: (i, 0)),
            pl.BlockSpec((tm, D_MODEL), lambda i: (i, 0)),
        ],
        out_shape=[jax.ShapeDtypeStruct((m, D_MODEL), F32),
                   jax.ShapeDtypeStruct((m, D_MODEL), BF16)],
        scratch_shapes=[pltpu.VMEM((tm, D_MODEL), BF16)],
        compiler_params=_params(("parallel",)),
        name="merge_out_proj",
    )(z, z, hm, hd, z, h, conv_a, w_br_a, w_br_m, w_br_d, w_out, g)


FFN_ROWS = 2


def _row_blocks(tm, n):
    units = tm // HALO
    edges = [HALO * (units * k // n) for k in range(n + 1)]
    return list(zip(edges[:-1], edges[1:]))


def _ffn_up_kernel(tm, u_ref, uh_ref, wa_ref, wb_ref, cw_ref, cb_ref, out_ref, w_scr):
    @pl.when(pl.program_id(1) == 0)
    def _():
        w_scr[0] = wa_ref[...].astype(BF16)
        w_scr[1] = wb_ref[...].astype(BF16)

    r_tile = pl.program_id(1) * tm
    a_prev = _dot(uh_ref[...], w_scr[0])
    for r0, r1 in _row_blocks(tm, FFN_ROWS):
        u = u_ref[r0:r1, :]
        a = _dot(u, w_scr[0])
        conv = _causal_conv3(a_prev, a, cw_ref[...], _pos_in_batch(r_tile + r0, r1 - r0)) + cb_ref[...]
        gelu = 0.5 * conv * (1.0 + lax.erf(conv * (2.0 ** -0.5)))
        out_ref[r0:r1, :] = (gelu * _dot(u, w_scr[1])).astype(BF16)
        a_prev = a[r1 - r0 - HALO:, :]


def _ffn_up(u, w_up, layer, conv_w, conv_b, tm=L, tn=512):
    m = u.shape[0]
    nb = D_FF // tn
    hb = tm // HALO
    return pl.pallas_call(
        functools.partial(_ffn_up_kernel, tm),
        grid=(nb, m // tm),
        in_specs=[
            pl.BlockSpec((tm, D_MODEL), lambda j, i: (i, 0)),
            pl.BlockSpec((HALO, D_MODEL), lambda j, i: (jnp.maximum(i * hb - 1, 0), 0)),
            pl.BlockSpec((None, D_MODEL, tn), lambda j, i: (layer, 0, j)),
            pl.BlockSpec((None, D_MODEL, tn), lambda j, i: (layer, 0, nb + j)),
            pl.BlockSpec((3, tn), lambda j, i: (0, j)),
            pl.BlockSpec((1, tn), lambda j, i: (0, j)),
        ],
        out_specs=pl.BlockSpec((tm, tn), lambda j, i: (i, j)),
        out_shape=jax.ShapeDtypeStruct((m, D_FF), BF16),
        scratch_shapes=[pltpu.VMEM((2, D_MODEL, tn), BF16)],
        compiler_params=_params(("parallel", "arbitrary")),
        name="ffn_up",
    )(u, u, w_up, w_up, conv_w, conv_b)


def _ffn_down_kernel(final, act_ref, w_ref, h_ref, g_ref, *out_refs):
    hn = h_ref[...] + _dot(act_ref[...], w_ref[...])
    y = _rms(hn, g_ref[...])
    if final:
        out_refs[0][...] = y
    else:
        out_refs[0][...] = hn
        out_refs[1][...] = y.astype(BF16)


def _ffn_down(act, w_down, layer, h, g, final):
    m = h.shape[0]
    if final:
        tm = 256
        m = m // L * SEQ
        per_batch = SEQ // tm

        def stream_row(i):
            return pl.multiple_of(i * tm + N_META * (i // per_batch + 1), N_META)

        act_spec = pl.BlockSpec((pl.Element(tm), pl.Element(D_FF)), lambda i: (stream_row(i), 0))
        h_spec = pl.BlockSpec((pl.Element(tm), pl.Element(D_MODEL)), lambda i: (stream_row(i), 0))
        row = pl.BlockSpec((tm, D_MODEL), lambda i: (i, 0))
        out_specs = [row]
        out_shape = [jax.ShapeDtypeStruct((m, D_MODEL), F32)]
    else:
        tm = 384
        row = pl.BlockSpec((tm, D_MODEL), lambda i: (i, 0))
        act_spec = pl.BlockSpec((tm, D_FF), lambda i: (i, 0))
        h_spec = row
        out_specs = [row, row]
        out_shape = [jax.ShapeDtypeStruct((m, D_MODEL), F32),
                     jax.ShapeDtypeStruct((m, D_MODEL), BF16)]
    return pl.pallas_call(
        functools.partial(_ffn_down_kernel, final),
        grid=(m // tm,),
        in_specs=[
            act_spec,
            _resident((None, D_FF, D_MODEL), lambda i: (layer, 0, 0)),
            h_spec,
            _resident((1, D_MODEL), lambda i: (0, 0)),
        ],
        out_specs=out_specs,
        out_shape=out_shape,
        compiler_params=_params(("parallel",)),
        name="ffn_down_final" if final else "ffn_down",
    )(act, w_down, h, g)


def kernel(x, meta, norm_mix, w_in, conv_a, b_if, ml_norm, da_lambda, da_norm, w_br_a, w_br_m, w_br_d,
           w_out, norm_ffn, w_up, conv_ffn, conv_ffn_b, w_down, norm_f):
    bsz, seq, d = x.shape
    assert (seq, d) == (SEQ, D_MODEL)
    depth = w_in.shape[0]
    w_in_t = jnp.swapaxes(w_in, 1, 2).reshape(depth * W_IN, D_MODEL)
    wa, wm, wd, wo, wdn = (w.astype(BF16) for w in (w_br_a, w_br_m, w_br_d, w_out, w_down))
    h, u = _embed(x.reshape(bsz * SEQ, D_MODEL), meta, norm_mix[0][None], bsz)
    for i in range(depth):
        z = _inproj(u, w_in_t, i)
        qt, vt, zift = _attn_proj(u, w_in_t, i, bsz)
        bias = jnp.broadcast_to(b_if[i].reshape(2 * ML_HEADS, 1), (2 * ML_HEADS, LANES))
        hm = _mlstm(z, zift, bias, ml_norm[i][None], bsz)
        hd = _attn(z, qt, vt, da_lambda[i], da_norm[i][:, None], i, bsz)
        h, u = _merge(z, hm, hd, h, i, conv_a[i], wa, wm, wd, wo, norm_ffn[i][None])
        act = _ffn_up(u, w_up, i, conv_ffn[i], conv_ffn_b[i][None])
        final = i == depth - 1
        g_next = norm_f if final else norm_mix[i + 1]
        outs = _ffn_down(act, wdn, i, h, g_next[None], final)
        if final:
            y = outs[0]
        else:
            h, u = outs
    return y.reshape(bsz, SEQ, D_MODEL)
```

```python
import functools
import math

import jax
import jax.numpy as jnp
from jax import lax
from jax.experimental import pallas as pl
from jax.experimental.pallas import tpu as pltpu

F32 = jnp.float32
BF16 = jnp.bfloat16

D_MODEL = 2048
SEQ = 2048
N_META = 16
L = N_META + SEQ
CHUNK = 64
EPS = 1e-6

CONV_DIM = 512
ML_HEADS = 4
ML_DK = 256
ML_DV = 256
ML_DIM = ML_HEADS * ML_DV
DA_HEADS = 4
DA_HD = 64
DA_VD = 2 * DA_HD
DA_DIM = DA_HEADS * DA_VD
D_FF = 5632

Z_AX = 3 * D_MODEL
Z_DK = Z_AX + 3 * CONV_DIM
Z_MQ = Z_DK + DA_DIM
Z_N = Z_MQ + 4 * ML_DIM

LANES = 128
ML_T = LANES
ML_EXT = ML_DV + LANES
HALO = 16
ATT_QB = 256
EMB_T = L // 3
VMEM_LIMIT = 56 * 1024 * 1024


def _dot(a, b):
    return jnp.dot(a, b, preferred_element_type=F32)


def _dot_nt(a, b):
    return lax.dot_general(a, b, (((1,), (1,)), ((), ())), preferred_element_type=F32)


def _bdot(a, b, ca, cb):
    return lax.dot_general(a, b, (((ca,), (cb,)), ((0,), (0,))), preferred_element_type=F32)


def _rms(x, g):
    return x * lax.rsqrt(jnp.mean(x * x, axis=-1, keepdims=True) + EPS) * g


def _sigmoid(x):
    return 1.0 / (1.0 + jnp.exp(-x))


def _pos_in_batch(r0, tm):
    pos = r0 % L + lax.broadcasted_iota(jnp.int32, (tm, 1), 0)
    return jnp.where(pos >= L, pos - L, pos)


def _causal_conv3(prev, cur, w, pos):
    cc = jnp.concatenate([prev, cur], axis=0)
    x1 = jnp.where(pos >= 1, pltpu.roll(cc, 1, 0)[HALO:], 0.0)
    x2 = jnp.where(pos >= 2, pltpu.roll(cc, 2, 0)[HALO:], 0.0)
    return w[0:1] * x2 + w[1:2] * x1 + w[2:3] * cur


def _params(sem):
    return pltpu.CompilerParams(dimension_semantics=sem, vmem_limit_bytes=VMEM_LIMIT)


def _resident(shape, index_map):
    return pl.BlockSpec(shape, index_map, pipeline_mode=pl.Buffered(1))


def _embed_kernel(x_ref, meta_ref, g_ref, h_ref, u_ref):
    j = pl.program_id(1)

    @pl.when(j == 0)
    def _():
        rows = jnp.concatenate([meta_ref[...], x_ref[0:EMB_T - N_META, :]], axis=0)
        h_ref[...] = rows
        u_ref[...] = _rms(rows, g_ref[...]).astype(BF16)

    @pl.when(j > 0)
    def _():
        rows = x_ref[...]
        h_ref[...] = rows
        u_ref[...] = _rms(rows, g_ref[...]).astype(BF16)


def _embed(x2d, meta, g, bsz):
    nb = L // EMB_T
    m = bsz * L
    return pl.pallas_call(
        _embed_kernel,
        grid=(bsz, nb),
        in_specs=[
            pl.BlockSpec((pl.Element(EMB_T), pl.Element(D_MODEL)),
                         lambda b, j: (pl.multiple_of(b * SEQ + jnp.maximum(j * EMB_T - N_META, 0), 16), 0)),
            pl.BlockSpec((N_META, D_MODEL), lambda b, j: (0, 0)),
            pl.BlockSpec((1, D_MODEL), lambda b, j: (0, 0)),
        ],
        out_specs=[
            pl.BlockSpec((EMB_T, D_MODEL), lambda b, j: (b * nb + j, 0)),
            pl.BlockSpec((EMB_T, D_MODEL), lambda b, j: (b * nb + j, 0)),
        ],
        out_shape=[jax.ShapeDtypeStruct((m, D_MODEL), F32),
                   jax.ShapeDtypeStruct((m, D_MODEL), BF16)],
        compiler_params=_params(("parallel", "arbitrary")),
        name="embed_norm",
    )(x2d, meta, g)


W_A_END = 3 * CONV_DIM
W_M_END = W_A_END + 4 * ML_DIM
W_IF_END = W_M_END + 2 * ML_HEADS
W_D_END = W_IF_END + 3 * DA_DIM
W_IN = W_D_END + 3 * D_MODEL


IN_SRC = 512
IN_NSRC = 2


def _inproj_kernel(u_ref, *refs):
    wt_refs, z_ref = refs[:IN_NSRC], refs[IN_NSRC]
    u = u_ref[...]
    for k, wt_ref in enumerate(wt_refs):
        z_ref[:, k * IN_SRC:(k + 1) * IN_SRC] = _dot_nt(u, wt_ref[...].astype(BF16)).astype(BF16)


def _inproj(u, w_in_t, layer, tm=L):
    m = u.shape[0]
    tn = IN_NSRC * IN_SRC
    n_gate, n_a = 3 * D_MODEL // IN_SRC, 3 * CONV_DIM // IN_SRC

    def src_row(t):
        gate = W_D_END + IN_SRC * t
        cnv = IN_SRC * (t - n_gate)
        att = W_IF_END + DA_DIM + IN_SRC * (t - n_gate - n_a)
        mls = W_A_END + IN_SRC * (t - n_gate - n_a - 1)
        row = jnp.where(t < n_gate, gate,
                        jnp.where(t < n_gate + n_a, cnv, jnp.where(t < n_gate + n_a + 1, att, mls)))
        return pl.multiple_of(layer * W_IN + row, 8)

    def src_spec(k):
        return pl.BlockSpec((pl.Element(IN_SRC), pl.Element(D_MODEL)),
                            lambda i, j: (src_row(IN_NSRC * j + k), 0))

    return pl.pallas_call(
        _inproj_kernel,
        grid=(m // tm, Z_N // tn),
        in_specs=[pl.BlockSpec((tm, D_MODEL), lambda i, j: (i, 0))] + [src_spec(k) for k in range(IN_NSRC)],
        out_specs=pl.BlockSpec((tm, tn), lambda i, j: (i, j)),
        out_shape=jax.ShapeDtypeStruct((m, Z_N), BF16),
        compiler_params=_params(("parallel", "arbitrary")),
        name="in_proj",
    )(u, *([w_in_t] * IN_NSRC))


ML_NCH = 1 + SEQ // ML_T
ML_UNROLL = 2


def _mlstm_gates(ift_ref, bias_ref, ab_scr, cm_scr):
    t = ML_T
    ninf = -jnp.inf
    bias = bias_ref[...]
    starts = [SEQ] + [c * t for c in range(ML_NCH - 1)]
    gt = jnp.concatenate([ift_ref[0:8, l0:l0 + t] + bias for l0 in starts], axis=0)
    shape = (8 * ML_NCH, t)
    lane = lax.broadcasted_iota(jnp.int32, shape, 1)
    row = lax.broadcasted_iota(jnp.int32, shape, 0)
    is_i = (row & 7) < ML_HEADS
    log_sig = jnp.minimum(gt, 0.0) - jnp.log1p(jnp.exp(-jnp.abs(gt)))
    lg = jnp.where(is_i, gt, log_sig)
    weightless = jnp.logical_and(row < 8, lane >= N_META)
    lg = jnp.where(weightless, jnp.where(is_i, ninf, 0.0), lg)
    cs = jnp.where(is_i, 0.0, lg)
    sh = 1
    while sh < t:
        cs = cs + jnp.where(lane >= sh, pltpu.roll(cs, sh, 1), 0.0)
        sh *= 2
    a = jnp.where(is_i, lg - pltpu.roll(cs, 8 * ML_NCH - ML_HEADS, 0), ninf)
    cm = a
    sh = 1
    while sh < t:
        cm = jnp.maximum(cm, jnp.where(lane >= sh, pltpu.roll(cm, sh, 1), ninf))
        sh *= 2
    ab_scr[...] = jnp.where(is_i, a, cs)
    cm_scr[...] = cm


def _mlstm_chunk(c, r0, n_store, q_ref, k_ref, v_ref, o_ref, g_ref, out_ref, s_ref, m_ref, ab_scr, cm_scr):
    t = ML_T
    rows = pl.ds(r0, t)
    row0 = c * 8 if isinstance(c, int) else pl.multiple_of(c * 8, 8)
    ab = ab_scr[pl.ds(row0, 8), :]
    cmr = cm_scr[pl.ds(row0, 8), :]

    tri = lax.broadcasted_iota(jnp.int32, (t, t), 0) >= lax.broadcasted_iota(jnp.int32, (t, t), 1)
    nh = ML_HEADS

    def heads(x, d):
        return jnp.stack([x[:, h * d:(h + 1) * d] for h in range(nh)], axis=0)

    def row_of(x, r):
        return jnp.stack([x[r + h:r + h + 1, :] for h in range(nh)], axis=0)

    def lane_rep(x, r):
        return jnp.stack([jnp.broadcast_to(x[r + h:r + h + 1, :], (t, t)).T for h in range(nh)], axis=0)

    def widen(x, n):
        return jnp.concatenate([x] * n, axis=2)

    q = heads(q_ref[rows, :], ML_DK)
    k = heads(k_ref[rows, :], ML_DK) * jnp.asarray(ML_DK ** -0.5, BF16)
    vext = jnp.concatenate([heads(v_ref[rows, :], ML_DV), jnp.ones((nh, t, ML_EXT - ML_DV), BF16)], axis=2)
    a_r = row_of(ab, 0)
    a_l, b_l, cm_l = lane_rep(ab, 0), lane_rep(ab, nh), lane_rep(cmr, 0)
    m_prev = m_ref[...][:, :, 0:1]

    m_l = jnp.maximum(cm_l, m_prev)
    w = jnp.where(tri[None], jnp.exp(a_r - m_l), 0.0)
    g_l = jnp.exp(m_prev - m_l)
    s = _bdot(q, k, 2, 2) * w
    tot = widen(g_l, 3) * _bdot(q, s_ref[...].astype(BF16), 2, 1) + _bdot(s.astype(BF16), vext, 2, 1)
    scale_l = 1.0 / jnp.maximum(jnp.abs(tot[:, :, ML_DV:]), jnp.exp(-(b_l + m_l)))
    hh = tot[:, :, :ML_DV] * widen(scale_l, 2)
    ssq_l = _bdot((hh * hh).astype(BF16), jnp.ones((nh, ML_DV, LANES), BF16), 2, 1)
    hn = hh * widen(lax.rsqrt(ssq_l * (1.0 / ML_DV) + EPS), 2)
    for h in range(nh):
        sl = slice(h * ML_DV, (h + 1) * ML_DV)
        res = (hn[h] * g_ref[:, sl] * _sigmoid(o_ref[rows, sl].astype(F32))).astype(BF16)
        if n_store < t:
            out_ref[pl.ds(r0, n_store), sl] = res[:n_store, :]
        else:
            out_ref[rows, sl] = res

    b_last = row_of(ab, nh)[:, :, t - 1:t]
    m_x = jnp.maximum(m_prev, row_of(cmr, 0)[:, :, t - 1:t])
    vw = vext * widen(jnp.exp(a_l - m_x).astype(BF16), 3)
    s_ref[...] = jnp.exp(m_prev - m_x) * s_ref[...] + _bdot(k, vw, 1, 1)
    m_ref[...] = jnp.broadcast_to(b_last + m_x, (nh, 1, LANES))


def _mlstm_kernel(q_ref, k_ref, v_ref, o_ref, if_ref, bias_ref, g_ref, out_ref, s_ref, m_ref, ab_scr, cm_scr):
    s_ref[...] = jnp.zeros_like(s_ref)
    m_ref[...] = jnp.zeros_like(m_ref)
    _mlstm_gates(if_ref, bias_ref, ab_scr, cm_scr)
    refs = (q_ref, k_ref, v_ref, o_ref, g_ref, out_ref, s_ref, m_ref, ab_scr, cm_scr)

    _mlstm_chunk(0, 0, N_META, *refs)

    def body(i, carry):
        for k in range(ML_UNROLL):
            c = 1 + ML_UNROLL * i + k
            _mlstm_chunk(c, pl.multiple_of(N_META + (c - 1) * ML_T, N_META), ML_T, *refs)
        return carry

    lax.fori_loop(0, (ML_NCH - 1) // ML_UNROLL, body, 0)


def _mlstm(z, zift, bias, g, bsz):
    m = bsz * L
    qb = Z_MQ // ML_DIM

    def zspec(off):
        return pl.BlockSpec((L, ML_DIM), lambda b: (b, qb + off))

    return pl.pallas_call(
        _mlstm_kernel,
        grid=(bsz,),
        in_specs=[
            zspec(0), zspec(1), zspec(2), zspec(3),
            pl.BlockSpec((None, IF_ROWS, ATT_TW), lambda b: (b, 0, 0)),
            pl.BlockSpec((2 * ML_HEADS, LANES), lambda b: (0, 0)),
            pl.BlockSpec((1, ML_DIM), lambda b: (0, 0)),
        ],
        out_specs=pl.BlockSpec((L, ML_DIM), lambda b: (b, 0)),
        out_shape=jax.ShapeDtypeStruct((m, ML_DIM), BF16),
        scratch_shapes=[pltpu.VMEM((ML_HEADS, ML_DK, ML_EXT), F32),
                        pltpu.VMEM((ML_HEADS, 1, LANES), F32),
                        pltpu.VMEM((8 * ML_NCH, ML_T), F32),
                        pltpu.VMEM((8 * ML_NCH, ML_T), F32)],
        compiler_params=_params(("parallel",)),
        name="mlstm",
    )(z, z, z, z, zift, bias, g)


ATT_MB = LANES
ATT_TW = SEQ + ATT_MB
ATT_HPS = 2


IF_ROWS = 16


def _attn_proj_kernel(u_ref, wq_ref, wv_ref, wif_ref, qt_ref, vt_ref, zift_ref):
    u_fr = u_ref[N_META:, :]
    u_meta = u_ref[0:ATT_MB, :]
    scale = DA_HD ** -0.5 * math.log2(math.e)
    wqif = jnp.concatenate([wq_ref[...].astype(BF16), wif_ref[...].astype(BF16)], axis=0)
    for lanes, u_part in ((slice(0, SEQ), u_fr), (slice(SEQ, ATT_TW), u_meta)):
        res = _dot_nt(wqif, u_part)
        qt_ref[:, lanes] = (res[:DA_DIM, :] * scale).astype(BF16)
        zift_ref[:, lanes] = res[DA_DIM:, :]
        vt_ref[:, lanes] = _dot_nt(wv_ref[...].astype(BF16), u_part).astype(BF16)


def _attn_proj(u, w_in_t, layer, bsz):
    def wspec(rows, row0):
        return _resident((pl.Element(rows), pl.Element(D_MODEL)), lambda b: (layer * W_IN + row0, 0))

    out = pl.BlockSpec((None, DA_DIM, ATT_TW), lambda b: (b, 0, 0))
    shape = jax.ShapeDtypeStruct((bsz, DA_DIM, ATT_TW), BF16)
    return pl.pallas_call(
        _attn_proj_kernel,
        grid=(bsz,),
        in_specs=[pl.BlockSpec((L, D_MODEL), lambda b: (b, 0)),
                  wspec(DA_DIM, W_IF_END), wspec(DA_DIM, W_IF_END + 2 * DA_DIM), wspec(IF_ROWS, W_M_END)],
        out_specs=[out, out, pl.BlockSpec((None, IF_ROWS, ATT_TW), lambda b: (b, 0, 0))],
        out_shape=[shape, shape, jax.ShapeDtypeStruct((bsz, IF_ROWS, ATT_TW), F32)],
        compiler_params=_params(("parallel",)),
        name="attn_qv_proj",
    )(u, w_in_t, w_in_t, w_in_t)


def _attn_scores(qt, k_ref, s_ref, n_frames, frame0):
    cols = qt.shape[1]
    ninf = -jnp.inf
    first_map = lax.broadcasted_iota(jnp.int32, (DA_VD, cols), 0) < DA_HD
    k_meta = k_ref[0:ATT_MB, :]
    meta_ok = lax.broadcasted_iota(jnp.int32, (ATT_MB, 1), 0) < N_META
    if n_frames:
        k_fr = k_ref[N_META:N_META + n_frames, :]
        qf = frame0 + lax.broadcasted_iota(jnp.int32, (1, cols), 1)
        kend = (qf // CHUNK + 1) * CHUNK
        kf = frame0 + lax.broadcasted_iota(jnp.int32, (n_frames - frame0, 1), 0)
        diag_ok = kf < kend
    for mp in range(2):
        qm = jnp.where(first_map if mp == 0 else jnp.logical_not(first_map), qt, jnp.zeros_like(qt))
        s_ref[mp, 0:ATT_MB, 0:cols] = jnp.where(meta_ok, _dot(k_meta, qm), ninf)
        if n_frames:
            s_fr = _dot(k_fr, qm)
            if frame0:
                s_ref[mp, ATT_MB:ATT_MB + frame0, 0:cols] = s_fr[:frame0, :]
            s_ref[mp, ATT_MB + frame0:ATT_MB + n_frames, 0:cols] = jnp.where(diag_ok, s_fr[frame0:, :], ninf)


def _attn_output(s_ref, vt_ref, cols, n_frames, lam, lam_init, g):
    nk = ATT_MB + n_frames
    probs = []
    for mp in range(2):
        s = s_ref[mp, 0:nk, 0:cols]
        p = jnp.exp2(s - jnp.max(s, axis=0, keepdims=True))
        probs.append((p, jnp.sum(p, axis=0, keepdims=True)))
    inv_l1 = 1.0 / probs[0][1]
    a = (probs[0][0] - probs[1][0] * (lam * probs[0][1] / probs[1][1])).astype(BF16)
    o = _dot(vt_ref[:, SEQ:ATT_TW], a[:ATT_MB, :])
    if n_frames:
        o = o + _dot(vt_ref[:, 0:n_frames], a[ATT_MB:, :])
    o = o * inv_l1
    o = o * lax.rsqrt(jnp.mean(o * o, axis=0, keepdims=True) + EPS)
    return o * g * (1.0 - lam_init)


def _attn_kernel(lam_init, qt_ref, k_ref, vt_ref, lam_ref, g_ref, out_ref, s_scr):
    lf = lam_ref[...]
    lam = (jnp.exp(jnp.sum(lf[0:1] * lf[1:2], axis=1, keepdims=True))
           - jnp.exp(jnp.sum(lf[2:3] * lf[3:4], axis=1, keepdims=True)) + lam_init)
    g = g_ref[...]
    blocks = [(SEQ, ATT_MB, 0, 0)] + [(f0, ATT_QB, f0 + ATT_QB, f0) for f0 in range(0, SEQ, ATT_QB)]
    tasks = [(hh,) + blk for hh in range(ATT_HPS) for blk in blocks]

    def head(ref, hh, axis):
        sl = pl.ds(hh * DA_VD, DA_VD)
        return ref.at[sl, :] if axis == 0 else ref.at[:, sl]

    def scores(i):
        hh, lane0, cols, n_frames, frame0 = tasks[i]
        qt = qt_ref[hh * DA_VD:(hh + 1) * DA_VD, lane0:lane0 + cols]
        _attn_scores(qt, head(k_ref, hh, 1), s_scr.at[i % 2], n_frames, frame0)

    scores(0)
    for i, (hh, lane0, cols, n_frames, frame0) in enumerate(tasks):
        if i + 1 < len(tasks):
            scores(i + 1)
        o = _attn_output(s_scr.at[i % 2], head(vt_ref, hh, 0), cols, n_frames, lam, lam_init, g)
        lanes = slice(hh * DA_VD, (hh + 1) * DA_VD)
        if n_frames:
            out_ref[N_META + frame0:N_META + frame0 + cols, lanes] = o.T.astype(BF16)
        else:
            out_ref[0:N_META, lanes] = o.T[0:N_META, :].astype(BF16)


def _attn(z, qt, vt, lam_p, g, layer, bsz):
    lam_init = 0.8 - 0.6 * math.exp(-0.3 * layer)
    m = bsz * L
    hw = ATT_HPS * DA_VD
    kb = Z_DK // hw

    def tspec():
        return pl.BlockSpec((None, hw, ATT_TW), lambda b, h: (b, h, 0))

    return pl.pallas_call(
        functools.partial(_attn_kernel, lam_init),
        grid=(bsz, DA_HEADS // ATT_HPS),
        in_specs=[
            tspec(),
            pl.BlockSpec((L, hw), lambda b, h: (b, kb + h)),
            tspec(),
            pl.BlockSpec((4, DA_HD), lambda b, h: (0, 0)),
            pl.BlockSpec((DA_VD, 1), lambda b, h: (0, 0)),
        ],
        out_specs=pl.BlockSpec((L, hw), lambda b, h: (b, h)),
        out_shape=jax.ShapeDtypeStruct((m, DA_DIM), BF16),
        scratch_shapes=[pltpu.VMEM((2, 2, ATT_TW, ATT_QB), F32)],
        compiler_params=_params(("parallel", "parallel")),
        name="diff_attn",
    )(qt, z, vt, lam_p, g)


def _merge_kernel(tm, a_ref, ah_ref, hm_ref, hd_ref, gate_ref, h_ref, cw_ref, wa_ref, wm_ref, wd_ref, wo_ref,
                  g_ref, hn_ref, u_ref, mg_scr):
    ax, ab, ac = (slice(k * CONV_DIM, (k + 1) * CONV_DIM) for k in range(3))
    pos = _pos_in_batch(pl.program_id(0) * tm, tm)
    cur = a_ref[:, ac].astype(F32) * a_ref[:, ax].astype(F32)
    prev = ah_ref[:, ac].astype(F32) * ah_ref[:, ax].astype(F32)
    a_act = (a_ref[:, ab].astype(F32) * _causal_conv3(prev, cur, cw_ref[...], pos)).astype(BF16)
    ga, gm, gd = (gate_ref[:, k * D_MODEL:(k + 1) * D_MODEL].astype(F32) for k in range(3))
    merged = (_sigmoid(ga) * _dot(a_act, wa_ref[...])
              + _sigmoid(gm) * _dot(hm_ref[...], wm_ref[...])
              + _sigmoid(gd) * _dot(hd_ref[...], wd_ref[...]))
    mg_scr[...] = merged.astype(BF16)
    hn = h_ref[...] + _dot(mg_scr[...], wo_ref[...])
    hn_ref[...] = hn
    u_ref[...] = _rms(hn, g_ref[...]).astype(BF16)


def _merge(z, hm, hd, h, layer, conv_a, w_br_a, w_br_m, w_br_d, w_out, g, tm=384):
    m = h.shape[0]
    a_w = 3 * CONV_DIM
    ab = Z_AX // a_w
    hb = tm // HALO

    def weight(rows):
        return _resident((None, rows, D_MODEL), lambda i: (layer, 0, 0))

    return pl.pallas_call(
        functools.partial(_merge_kernel, tm),
        grid=(m // tm,),
        in_specs=[
            pl.BlockSpec((tm, a_w), lambda i: (i, ab)),
            pl.BlockSpec((HALO, a_w), lambda i: (jnp.maximum(i * hb - 1, 0), ab)),
            pl.BlockSpec((tm, ML_DIM), lambda i: (i, 0)),
            pl.BlockSpec((tm, DA_DIM), lambda i: (i, 0)),
            pl.BlockSpec((tm, 3 * D_MODEL), lambda i: (i, 0)),
            pl.BlockSpec((tm, D_MODEL), lambda i: (i, 0)),
            _resident((3, CONV_DIM), lambda i: (0, 0)),
            weight(CONV_DIM), weight(ML_DIM), weight(DA_DIM), weight(D_MODEL),
            _resident((1, D_MODEL), lambda i: (0, 0)),
        ],
        out_specs=[
            pl.BlockSpec((tm, D_MODEL), lambda i: (i, 0)),
            pl.BlockSpec((tm, D_MODEL), lambda i: (i, 0)),
        ],
        out_shape=[jax.ShapeDtypeStruct((m, D_MODEL), F32),
                   jax.ShapeDtypeStruct((m, D_MODEL), BF16)],
        scratch_shapes=[pltpu.VMEM((tm, D_MODEL), BF16)],
        compiler_params=_params(("parallel",)),
        name="merge_out_proj",
    )(z, z, hm, hd, z, h, conv_a, w_br_a, w_br_m, w_br_d, w_out, g)


FFN_ROWS = 2


def _row_blocks(tm, n):
    units = tm // HALO
    edges = [HALO * (units * k // n) for k in range(n + 1)]
    return list(zip(edges[:-1], edges[1:]))


def _ffn_up_kernel(u_ref, wa_ref, wb_ref, cw_ref, cb_ref, out_ref, w_scr):
    @pl.when(pl.program_id(1) == 0)
    def _():
        w_scr[0] = wa_ref[...].astype(BF16)
        w_scr[1] = wb_ref[...].astype(BF16)

    a_prev = jnp.zeros((HALO, out_ref.shape[1]), F32)
    for r0, r1 in _row_blocks(L, FFN_ROWS):
        u = u_ref[r0:r1, :]
        a = _dot(u, w_scr[0])
        pos = r0 + lax.broadcasted_iota(jnp.int32, (r1 - r0, 1), 0)
        conv = _causal_conv3(a_prev, a, cw_ref[...], pos) + cb_ref[...]
        gelu = 0.5 * conv * (1.0 + lax.erf(conv * (2.0 ** -0.5)))
        out_ref[r0:r1, :] = (gelu * _dot(u, w_scr[1])).astype(BF16)
        a_prev = a[r1 - r0 - HALO:, :]


def _ffn_up(u, w_up, layer, conv_w, conv_b, tn=512):
    m = u.shape[0]
    nb = D_FF // tn
    tm = L
    return pl.pallas_call(
        _ffn_up_kernel,
        grid=(nb, m // tm),
        in_specs=[
            pl.BlockSpec((tm, D_MODEL), lambda j, i: (i, 0)),
            pl.BlockSpec((None, D_MODEL, tn), lambda j, i: (layer, 0, j)),
            pl.BlockSpec((None, D_MODEL, tn), lambda j, i: (layer, 0, nb + j)),
            pl.BlockSpec((3, tn), lambda j, i: (0, j)),
            pl.BlockSpec((1, tn), lambda j, i: (0, j)),
        ],
        out_specs=pl.BlockSpec((tm, tn), lambda j, i: (i, j)),
        out_shape=jax.ShapeDtypeStruct((m, D_FF), BF16),
        scratch_shapes=[pltpu.VMEM((2, D_MODEL, tn), BF16)],
        compiler_params=_params(("parallel", "arbitrary")),
        name="ffn_up",
    )(u, w_up, w_up, conv_w, conv_b)


def _ffn_down_kernel(final, act_ref, w_ref, h_ref, g_ref, *out_refs):
    hn = h_ref[...] + _dot(act_ref[...], w_ref[...])
    y = _rms(hn, g_ref[...])
    if final:
        out_refs[0][...] = y
    else:
        out_refs[0][...] = hn
        out_refs[1][...] = y.astype(BF16)


def _ffn_down(act, w_down, layer, h, g, final):
    m = h.shape[0]
    if final:
        tm = 256
        m = m // L * SEQ
        per_batch = SEQ // tm

        def stream_row(i):
            return pl.multiple_of(i * tm + N_META * (i // per_batch + 1), N_META)

        act_spec = pl.BlockSpec((pl.Element(tm), pl.Element(D_FF)), lambda i: (stream_row(i), 0))
        h_spec = pl.BlockSpec((pl.Element(tm), pl.Element(D_MODEL)), lambda i: (stream_row(i), 0))
        row = pl.BlockSpec((tm, D_MODEL), lambda i: (i, 0))
        out_specs = [row]
        out_shape = [jax.ShapeDtypeStruct((m, D_MODEL), F32)]
    else:
        tm = 384
        row = pl.BlockSpec((tm, D_MODEL), lambda i: (i, 0))
        act_spec = pl.BlockSpec((tm, D_FF), lambda i: (i, 0))
        h_spec = row
        out_specs = [row, row]
        out_shape = [jax.ShapeDtypeStruct((m, D_MODEL), F32),
                     jax.ShapeDtypeStruct((m, D_MODEL), BF16)]
    return pl.pallas_call(
        functools.partial(_ffn_down_kernel, final),
        grid=(m // tm,),
        in_specs=[
            act_spec,
            _resident((None, D_FF, D_MODEL), lambda i: (layer, 0, 0)),
            h_spec,
            _resident((1, D_MODEL), lambda i: (0, 0)),
        ],
        out_specs=out_specs,
        out_shape=out_shape,
        compiler_params=_params(("parallel",)),
        name="ffn_down_final" if final else "ffn_down",
    )(act, w_down, h, g)


def kernel(x, meta, norm_mix, w_in, conv_a, b_if, ml_norm, da_lambda, da_norm, w_br_a, w_br_m, w_br_d,
           w_out, norm_ffn, w_up, conv_ffn, conv_ffn_b, w_down, norm_f):
    bsz, seq, d = x.shape
    assert (seq, d) == (SEQ, D_MODEL)
    depth = w_in.shape[0]
    w_in_t = jnp.swapaxes(w_in, 1, 2).reshape(depth * W_IN, D_MODEL)
    wa, wm, wd, wo, wdn = (w.astype(BF16) for w in (w_br_a, w_br_m, w_br_d, w_out, w_down))
    h, u = _embed(x.reshape(bsz * SEQ, D_MODEL), meta, norm_mix[0][None], bsz)
    for i in range(depth):
        z = _inproj(u, w_in_t, i)
        qt, vt, zift = _attn_proj(u, w_in_t, i, bsz)
        bias = jnp.broadcast_to(b_if[i].reshape(2 * ML_HEADS, 1), (2 * ML_HEADS, LANES))
        hm = _mlstm(z, zift, bias, ml_norm[i][None], bsz)
        hd = _attn(z, qt, vt, da_lambda[i], da_norm[i][:, None], i, bsz)
        h, u = _merge(z, hm, hd, h, i, conv_a[i], wa, wm, wd, wo, norm_ffn[i][None])
        act = _ffn_up(u, w_up, i, conv_ffn[i], conv_ffn_b[i][None])
        final = i == depth - 1
        g_next = norm_f if final else norm_mix[i + 1]
        outs = _ffn_down(act, wdn, i, h, g_next[None], final)
        if final:
            y = outs[0]
        else:
            h, u = outs
    return y.reshape(bsz, SEQ, D_MODEL)
```

```python
import functools
import math

import jax
import jax.numpy as jnp
from jax import lax
from jax.experimental import pallas as pl
from jax.experimental.pallas import tpu as pltpu

F32 = jnp.float32
BF16 = jnp.bfloat16

D_MODEL = 2048
SEQ = 2048
N_META = 16
L = N_META + SEQ
CHUNK = 64
EPS = 1e-6

CONV_DIM = 512
ML_HEADS = 4
ML_DK = 256
ML_DV = 256
ML_DIM = ML_HEADS * ML_DV
DA_HEADS = 4
DA_HD = 64
DA_VD = 2 * DA_HD
DA_DIM = DA_HEADS * DA_VD
D_FF = 5632

Z_AX = 3 * D_MODEL
Z_DK = Z_AX + 3 * CONV_DIM
Z_MQ = Z_DK + DA_DIM
Z_N = Z_MQ + 4 * ML_DIM

LANES = 128
ML_T = LANES
ML_EXT = ML_DV + LANES
HALO = 16
ATT_QB = 256
EMB_T = L // 3
VMEM_LIMIT = 56 * 1024 * 1024


def _dot(a, b):
    return jnp.dot(a, b, preferred_element_type=F32)


def _dot_nt(a, b):
    return lax.dot_general(a, b, (((1,), (1,)), ((), ())), preferred_element_type=F32)


def _bdot(a, b, ca, cb):
    return lax.dot_general(a, b, (((ca,), (cb,)), ((0,), (0,))), preferred_element_type=F32)


def _rms(x, g):
    return x * lax.rsqrt(jnp.mean(x * x, axis=-1, keepdims=True) + EPS) * g


def _sigmoid(x):
    return 1.0 / (1.0 + jnp.exp(-x))


def _pos_in_batch(r0, tm):
    pos = r0 % L + lax.broadcasted_iota(jnp.int32, (tm, 1), 0)
    return jnp.where(pos >= L, pos - L, pos)


def _causal_conv3(prev, cur, w, pos):
    cc = jnp.concatenate([prev, cur], axis=0)
    x1 = jnp.where(pos >= 1, pltpu.roll(cc, 1, 0)[HALO:], 0.0)
    x2 = jnp.where(pos >= 2, pltpu.roll(cc, 2, 0)[HALO:], 0.0)
    return w[0:1] * x2 + w[1:2] * x1 + w[2:3] * cur


def _params(sem):
    return pltpu.CompilerParams(dimension_semantics=sem, vmem_limit_bytes=VMEM_LIMIT)


def _resident(shape, index_map):
    return pl.BlockSpec(shape, index_map, pipeline_mode=pl.Buffered(1))


def _embed_kernel(x_ref, meta_ref, g_ref, h_ref, u_ref):
    j = pl.program_id(1)

    @pl.when(j == 0)
    def _():
        rows = jnp.concatenate([meta_ref[...], x_ref[0:EMB_T - N_META, :]], axis=0)
        h_ref[...] = rows
        u_ref[...] = _rms(rows, g_ref[...]).astype(BF16)

    @pl.when(j > 0)
    def _():
        rows = x_ref[...]
        h_ref[...] = rows
        u_ref[...] = _rms(rows, g_ref[...]).astype(BF16)


def _embed(x2d, meta, g, bsz):
    nb = L // EMB_T
    m = bsz * L
    return pl.pallas_call(
        _embed_kernel,
        grid=(bsz, nb),
        in_specs=[
            pl.BlockSpec((pl.Element(EMB_T), pl.Element(D_MODEL)),
                         lambda b, j: (pl.multiple_of(b * SEQ + jnp.maximum(j * EMB_T - N_META, 0), 16), 0)),
            pl.BlockSpec((N_META, D_MODEL), lambda b, j: (0, 0)),
            pl.BlockSpec((1, D_MODEL), lambda b, j: (0, 0)),
        ],
        out_specs=[
            pl.BlockSpec((EMB_T, D_MODEL), lambda b, j: (b * nb + j, 0)),
            pl.BlockSpec((EMB_T, D_MODEL), lambda b, j: (b * nb + j, 0)),
        ],
        out_shape=[jax.ShapeDtypeStruct((m, D_MODEL), F32),
                   jax.ShapeDtypeStruct((m, D_MODEL), BF16)],
        compiler_params=_params(("parallel", "arbitrary")),
        name="embed_norm",
    )(x2d, meta, g)


W_A_END = 3 * CONV_DIM
W_M_END = W_A_END + 4 * ML_DIM
W_IF_END = W_M_END + 2 * ML_HEADS
W_D_END = W_IF_END + 3 * DA_DIM
W_IN = W_D_END + 3 * D_MODEL


IN_SRC = 512
IN_NSRC = 2


def _inproj_kernel(u_ref, *refs):
    wt_refs, z_ref = refs[:IN_NSRC], refs[IN_NSRC]
    for k, wt_ref in enumerate(wt_refs):
        wt = wt_ref[...].astype(BF16)
        for r0, r1 in ((0, 1040), (1040, L)):
            z_ref[r0:r1, k * IN_SRC:(k + 1) * IN_SRC] = _dot_nt(u_ref[r0:r1, :], wt).astype(BF16)


def _inproj(u, w_in_t, layer, tm=L):
    m = u.shape[0]
    tn = IN_NSRC * IN_SRC
    n_gate, n_a = 3 * D_MODEL // IN_SRC, 3 * CONV_DIM // IN_SRC

    def src_row(t):
        gate = W_D_END + IN_SRC * t
        cnv = IN_SRC * (t - n_gate)
        att = W_IF_END + DA_DIM + IN_SRC * (t - n_gate - n_a)
        mls = W_A_END + IN_SRC * (t - n_gate - n_a - 1)
        row = jnp.where(t < n_gate, gate,
                        jnp.where(t < n_gate + n_a, cnv, jnp.where(t < n_gate + n_a + 1, att, mls)))
        return pl.multiple_of(layer * W_IN + row, 8)

    def src_spec(k):
        return pl.BlockSpec((pl.Element(IN_SRC), pl.Element(D_MODEL)),
                            lambda i, j: (src_row(IN_NSRC * j + k), 0))

    return pl.pallas_call(
        _inproj_kernel,
        grid=(m // tm, Z_N // tn),
        in_specs=[pl.BlockSpec((tm, D_MODEL), lambda i, j: (i, 0))] + [src_spec(k) for k in range(IN_NSRC)],
        out_specs=pl.BlockSpec((tm, tn), lambda i, j: (i, j)),
        out_shape=jax.ShapeDtypeStruct((m, Z_N), BF16),
        compiler_params=_params(("parallel", "arbitrary")),
        name="in_proj",
    )(u, *([w_in_t] * IN_NSRC))


ML_NCH = 1 + SEQ // ML_T
ML_UNROLL = 2


def _mlstm_gates(ift_ref, bias_ref, ab_scr, cm_scr):
    t = ML_T
    ninf = -jnp.inf
    bias = bias_ref[...]
    starts = [SEQ] + [c * t for c in range(ML_NCH - 1)]
    gt = jnp.concatenate([ift_ref[0:8, l0:l0 + t] + bias for l0 in starts], axis=0)
    shape = (8 * ML_NCH, t)
    lane = lax.broadcasted_iota(jnp.int32, shape, 1)
    row = lax.broadcasted_iota(jnp.int32, shape, 0)
    is_i = (row & 7) < ML_HEADS
    log_sig = jnp.minimum(gt, 0.0) - jnp.log1p(jnp.exp(-jnp.abs(gt)))
    lg = jnp.where(is_i, gt, log_sig)
    weightless = jnp.logical_and(row < 8, lane >= N_META)
    lg = jnp.where(weightless, jnp.where(is_i, ninf, 0.0), lg)
    cs = jnp.where(is_i, 0.0, lg)
    sh = 1
    while sh < t:
        cs = cs + jnp.where(lane >= sh, pltpu.roll(cs, sh, 1), 0.0)
        sh *= 2
    a = jnp.where(is_i, lg - pltpu.roll(cs, 8 * ML_NCH - ML_HEADS, 0), ninf)
    cm = a
    sh = 1
    while sh < t:
        cm = jnp.maximum(cm, jnp.where(lane >= sh, pltpu.roll(cm, sh, 1), ninf))
        sh *= 2
    ab_scr[...] = jnp.where(is_i, a, cs)
    cm_scr[...] = cm


def _mlstm_chunk(c, r0, n_store, q_ref, k_ref, v_ref, o_ref, g_ref, out_ref, s_ref, m_ref, ab_scr, cm_scr):
    t = ML_T
    rows = pl.ds(r0, t)
    row0 = c * 8 if isinstance(c, int) else pl.multiple_of(c * 8, 8)
    ab = ab_scr[pl.ds(row0, 8), :]
    cmr = cm_scr[pl.ds(row0, 8), :]

    tri = lax.broadcasted_iota(jnp.int32, (t, t), 0) >= lax.broadcasted_iota(jnp.int32, (t, t), 1)
    nh = ML_HEADS

    def heads(x, d):
        return jnp.stack([x[:, h * d:(h + 1) * d] for h in range(nh)], axis=0)

    def row_of(x, r):
        return jnp.stack([x[r + h:r + h + 1, :] for h in range(nh)], axis=0)

    def lane_rep(x, r):
        return jnp.stack([jnp.broadcast_to(x[r + h:r + h + 1, :], (t, t)).T for h in range(nh)], axis=0)

    def widen(x, n):
        return jnp.concatenate([x] * n, axis=2)

    q = heads(q_ref[rows, :], ML_DK)
    k = heads(k_ref[rows, :], ML_DK) * jnp.asarray(ML_DK ** -0.5, BF16)
    vext = jnp.concatenate([heads(v_ref[rows, :], ML_DV), jnp.ones((nh, t, ML_EXT - ML_DV), BF16)], axis=2)
    a_r = row_of(ab, 0)
    a_l, b_l, cm_l = lane_rep(ab, 0), lane_rep(ab, nh), lane_rep(cmr, 0)
    m_prev = m_ref[...][:, :, 0:1]

    m_l = jnp.maximum(cm_l, m_prev)
    w = jnp.where(tri[None], jnp.exp(a_r - m_l), 0.0)
    g_l = jnp.exp(m_prev - m_l)
    s = _bdot(q, k, 2, 2) * w
    tot = widen(g_l, 3) * _bdot(q, s_ref[...].astype(BF16), 2, 1) + _bdot(s.astype(BF16), vext, 2, 1)
    scale_l = 1.0 / jnp.maximum(jnp.abs(tot[:, :, ML_DV:]), jnp.exp(-(b_l + m_l)))
    hh = tot[:, :, :ML_DV] * widen(scale_l, 2)
    ssq_l = _bdot((hh * hh).astype(BF16), jnp.ones((nh, ML_DV, LANES), BF16), 2, 1)
    hn = hh * widen(lax.rsqrt(ssq_l * (1.0 / ML_DV) + EPS), 2)
    for h in range(nh):
        sl = slice(h * ML_DV, (h + 1) * ML_DV)
        res = (hn[h] * g_ref[:, sl] * _sigmoid(o_ref[rows, sl].astype(F32))).astype(BF16)
        if n_store < t:
            out_ref[pl.ds(r0, n_store), sl] = res[:n_store, :]
        else:
            out_ref[rows, sl] = res

    b_last = row_of(ab, nh)[:, :, t - 1:t]
    m_x = jnp.maximum(m_prev, row_of(cmr, 0)[:, :, t - 1:t])
    vw = vext * widen(jnp.exp(a_l - m_x).astype(BF16), 3)
    s_ref[...] = jnp.exp(m_prev - m_x) * s_ref[...] + _bdot(k, vw, 1, 1)
    m_ref[...] = jnp.broadcast_to(b_last + m_x, (nh, 1, LANES))


def _mlstm_kernel(q_ref, k_ref, v_ref, o_ref, if_ref, bias_ref, g_ref, out_ref, s_ref, m_ref, ab_scr, cm_scr):
    s_ref[...] = jnp.zeros_like(s_ref)
    m_ref[...] = jnp.zeros_like(m_ref)
    _mlstm_gates(if_ref, bias_ref, ab_scr, cm_scr)
    refs = (q_ref, k_ref, v_ref, o_ref, g_ref, out_ref, s_ref, m_ref, ab_scr, cm_scr)

    _mlstm_chunk(0, 0, N_META, *refs)

    def body(i, carry):
        for k in range(ML_UNROLL):
            c = 1 + ML_UNROLL * i + k
            _mlstm_chunk(c, pl.multiple_of(N_META + (c - 1) * ML_T, N_META), ML_T, *refs)
        return carry

    lax.fori_loop(0, (ML_NCH - 1) // ML_UNROLL, body, 0)


def _mlstm(z, zift, bias, g, bsz):
    m = bsz * L
    qb = Z_MQ // ML_DIM

    def zspec(off):
        return pl.BlockSpec((L, ML_DIM), lambda b: (b, qb + off))

    return pl.pallas_call(
        _mlstm_kernel,
        grid=(bsz,),
        in_specs=[
            zspec(0), zspec(1), zspec(2), zspec(3),
            pl.BlockSpec((None, IF_ROWS, ATT_TW), lambda b: (b, 0, 0)),
            pl.BlockSpec((2 * ML_HEADS, LANES), lambda b: (0, 0)),
            pl.BlockSpec((1, ML_DIM), lambda b: (0, 0)),
        ],
        out_specs=pl.BlockSpec((L, ML_DIM), lambda b: (b, 0)),
        out_shape=jax.ShapeDtypeStruct((m, ML_DIM), BF16),
        scratch_shapes=[pltpu.VMEM((ML_HEADS, ML_DK, ML_EXT), F32),
                        pltpu.VMEM((ML_HEADS, 1, LANES), F32),
                        pltpu.VMEM((8 * ML_NCH, ML_T), F32),
                        pltpu.VMEM((8 * ML_NCH, ML_T), F32)],
        compiler_params=_params(("parallel",)),
        name="mlstm",
    )(z, z, z, z, zift, bias, g)


ATT_MB = LANES
ATT_TW = SEQ + ATT_MB
ATT_HPS = 2


IF_ROWS = 16


def _attn_proj_kernel(u_ref, wq_ref, wv_ref, wif_ref, qt_ref, vt_ref, zift_ref):
    u_fr = u_ref[N_META:, :]
    u_meta = u_ref[0:ATT_MB, :]
    scale = DA_HD ** -0.5 * math.log2(math.e)
    wqif = jnp.concatenate([wq_ref[...].astype(BF16), wif_ref[...].astype(BF16)], axis=0)
    for lanes, u_part in ((slice(0, SEQ), u_fr), (slice(SEQ, ATT_TW), u_meta)):
        res = _dot_nt(wqif, u_part)
        qt_ref[:, lanes] = (res[:DA_DIM, :] * scale).astype(BF16)
        zift_ref[:, lanes] = res[DA_DIM:, :]
        vt_ref[:, lanes] = _dot_nt(wv_ref[...].astype(BF16), u_part).astype(BF16)


def _attn_proj(u, w_in_t, layer, bsz):
    def wspec(rows, row0):
        return _resident((pl.Element(rows), pl.Element(D_MODEL)), lambda b: (layer * W_IN + row0, 0))

    out = pl.BlockSpec((None, DA_DIM, ATT_TW), lambda b: (b, 0, 0))
    shape = jax.ShapeDtypeStruct((bsz, DA_DIM, ATT_TW), BF16)
    return pl.pallas_call(
        _attn_proj_kernel,
        grid=(bsz,),
        in_specs=[pl.BlockSpec((L, D_MODEL), lambda b: (b, 0)),
                  wspec(DA_DIM, W_IF_END), wspec(DA_DIM, W_IF_END + 2 * DA_DIM), wspec(IF_ROWS, W_M_END)],
        out_specs=[out, out, pl.BlockSpec((None, IF_ROWS, ATT_TW), lambda b: (b, 0, 0))],
        out_shape=[shape, shape, jax.ShapeDtypeStruct((bsz, IF_ROWS, ATT_TW), F32)],
        compiler_params=_params(("parallel",)),
        name="attn_qv_proj",
    )(u, w_in_t, w_in_t, w_in_t)


def _attn_scores(qt, k_ref, s_ref, n_frames, frame0):
    cols = qt.shape[1]
    ninf = -jnp.inf
    first_map = lax.broadcasted_iota(jnp.int32, (DA_VD, cols), 0) < DA_HD
    k_meta = k_ref[0:ATT_MB, :]
    meta_ok = lax.broadcasted_iota(jnp.int32, (ATT_MB, 1), 0) < N_META
    if n_frames:
        k_fr = k_ref[N_META:N_META + n_frames, :]
        qf = frame0 + lax.broadcasted_iota(jnp.int32, (1, cols), 1)
        kend = (qf // CHUNK + 1) * CHUNK
        kf = frame0 + lax.broadcasted_iota(jnp.int32, (n_frames - frame0, 1), 0)
        diag_ok = kf < kend
    for mp in range(2):
        qm = jnp.where(first_map if mp == 0 else jnp.logical_not(first_map), qt, jnp.zeros_like(qt))
        s_ref[mp, 0:ATT_MB, 0:cols] = jnp.where(meta_ok, _dot(k_meta, qm), ninf)
        if n_frames:
            s_fr = _dot(k_fr, qm)
            if frame0:
                s_ref[mp, ATT_MB:ATT_MB + frame0, 0:cols] = s_fr[:frame0, :]
            s_ref[mp, ATT_MB + frame0:ATT_MB + n_frames, 0:cols] = jnp.where(diag_ok, s_fr[frame0:, :], ninf)


def _attn_output(s_ref, vt_ref, cols, n_frames, lam, lam_init, g):
    nk = ATT_MB + n_frames
    probs = []
    for mp in range(2):
        s = s_ref[mp, 0:nk, 0:cols]
        p = jnp.exp2(s - jnp.max(s, axis=0, keepdims=True))
        probs.append((p, jnp.sum(p, axis=0, keepdims=True)))
    inv_l1 = 1.0 / probs[0][1]
    a = (probs[0][0] - probs[1][0] * (lam * probs[0][1] / probs[1][1])).astype(BF16)
    o = _dot(vt_ref[:, SEQ:ATT_TW], a[:ATT_MB, :])
    if n_frames:
        o = o + _dot(vt_ref[:, 0:n_frames], a[ATT_MB:, :])
    o = o * inv_l1
    o = o * lax.rsqrt(jnp.mean(o * o, axis=0, keepdims=True) + EPS)
    return o * g * (1.0 - lam_init)


def _attn_kernel(lam_init, qt_ref, k_ref, vt_ref, lam_ref, g_ref, out_ref, s_scr):
    lf = lam_ref[...]
    lam = (jnp.exp(jnp.sum(lf[0:1] * lf[1:2], axis=1, keepdims=True))
           - jnp.exp(jnp.sum(lf[2:3] * lf[3:4], axis=1, keepdims=True)) + lam_init)
    g = g_ref[...]
    blocks = [(SEQ, ATT_MB, 0, 0)] + [(f0, ATT_QB, f0 + ATT_QB, f0) for f0 in range(0, SEQ, ATT_QB)]
    tasks = [(hh,) + blk for hh in range(ATT_HPS) for blk in blocks]

    def head(ref, hh, axis):
        sl = pl.ds(hh * DA_VD, DA_VD)
        return ref.at[sl, :] if axis == 0 else ref.at[:, sl]

    def scores(i):
        hh, lane0, cols, n_frames, frame0 = tasks[i]
        qt = qt_ref[hh * DA_VD:(hh + 1) * DA_VD, lane0:lane0 + cols]
        _attn_scores(qt, head(k_ref, hh, 1), s_scr.at[i % 2], n_frames, frame0)

    scores(0)
    for i, (hh, lane0, cols, n_frames, frame0) in enumerate(tasks):
        if i + 1 < len(tasks):
            scores(i + 1)
        o = _attn_output(s_scr.at[i % 2], head(vt_ref, hh, 0), cols, n_frames, lam, lam_init, g)
        lanes = slice(hh * DA_VD, (hh + 1) * DA_VD)
        if n_frames:
            out_ref[N_META + frame0:N_META + frame0 + cols, lanes] = o.T.astype(BF16)
        else:
            out_ref[0:N_META, lanes] = o.T[0:N_META, :].astype(BF16)


def _attn(z, qt, vt, lam_p, g, layer, bsz):
    lam_init = 0.8 - 0.6 * math.exp(-0.3 * layer)
    m = bsz * L
    hw = ATT_HPS * DA_VD
    kb = Z_DK // hw

    def tspec():
        return pl.BlockSpec((None, hw, ATT_TW), lambda b, h: (b, h, 0))

    return pl.pallas_call(
        functools.partial(_attn_kernel, lam_init),
        grid=(bsz, DA_HEADS // ATT_HPS),
        in_specs=[
            tspec(),
            pl.BlockSpec((L, hw), lambda b, h: (b, kb + h)),
            tspec(),
            pl.BlockSpec((4, DA_HD), lambda b, h: (0, 0)),
            pl.BlockSpec((DA_VD, 1), lambda b, h: (0, 0)),
        ],
        out_specs=pl.BlockSpec((L, hw), lambda b, h: (b, h)),
        out_shape=jax.ShapeDtypeStruct((m, DA_DIM), BF16),
        scratch_shapes=[pltpu.VMEM((2, 2, ATT_TW, ATT_QB), F32)],
        compiler_params=_params(("parallel", "parallel")),
        name="diff_attn",
    )(qt, z, vt, lam_p, g)


def _merge_kernel(tm, a_ref, ah_ref, hm_ref, hd_ref, gate_ref, h_ref, cw_ref, wa_ref, wm_ref, wd_ref, wo_ref,
                  g_ref, hn_ref, u_ref, mg_scr):
    ax, ab, ac = (slice(k * CONV_DIM, (k + 1) * CONV_DIM) for k in range(3))
    pos = _pos_in_batch(pl.program_id(0) * tm, tm)
    cur = a_ref[:, ac].astype(F32) * a_ref[:, ax].astype(F32)
    prev = ah_ref[:, ac].astype(F32) * ah_ref[:, ax].astype(F32)
    a_act = (a_ref[:, ab].astype(F32) * _causal_conv3(prev, cur, cw_ref[...], pos)).astype(BF16)
    ga, gm, gd = (gate_ref[:, k * D_MODEL:(k + 1) * D_MODEL].astype(F32) for k in range(3))
    merged = (_sigmoid(ga) * _dot(a_act, wa_ref[...])
              + _sigmoid(gm) * _dot(hm_ref[...], wm_ref[...])
              + _sigmoid(gd) * _dot(hd_ref[...], wd_ref[...]))
    mg_scr[...] = merged.astype(BF16)
    hn = h_ref[...] + _dot(mg_scr[...], wo_ref[...])
    hn_ref[...] = hn
    u_ref[...] = _rms(hn, g_ref[...]).astype(BF16)


def _merge(z, hm, hd, h, layer, conv_a, w_br_a, w_br_m, w_br_d, w_out, g, tm=384):
    m = h.shape[0]
    a_w = 3 * CONV_DIM
    ab = Z_AX // a_w
    hb = tm // HALO

    def weight(rows):
        return _resident((None, rows, D_MODEL), lambda i: (layer, 0, 0))

    return pl.pallas_call(
        functools.partial(_merge_kernel, tm),
        grid=(m // tm,),
        in_specs=[
            pl.BlockSpec((tm, a_w), lambda i: (i, ab)),
            pl.BlockSpec((HALO, a_w), lambda i: (jnp.maximum(i * hb - 1, 0), ab)),
            pl.BlockSpec((tm, ML_DIM), lambda i: (i, 0)),
            pl.BlockSpec((tm, DA_DIM), lambda i: (i, 0)),
            pl.BlockSpec((tm, 3 * D_MODEL), lambda i: (i, 0)),
            pl.BlockSpec((tm, D_MODEL), lambda i: (i, 0)),
            _resident((3, CONV_DIM), lambda i: (0, 0)),
            weight(CONV_DIM), weight(ML_DIM), weight(DA_DIM), weight(D_MODEL),
            _resident((1, D_MODEL), lambda i: (0, 0)),
        ],
        out_specs=[
            pl.BlockSpec((tm, D_MODEL), lambda i: (i, 0)),
            pl.BlockSpec((tm, D_MODEL), lambda i: (i, 0)),
        ],
        out_shape=[jax.ShapeDtypeStruct((m, D_MODEL), F32),
                   jax.ShapeDtypeStruct((m, D_MODEL), BF16)],
        scratch_shapes=[pltpu.VMEM((tm, D_MODEL), BF16)],
        compiler_params=_params(("parallel",)),
        name="merge_out_proj",
    )(z, z, hm, hd, z, h, conv_a, w_br_a, w_br_m, w_br_d, w_out, g)


FFN_ROWS = 2


def _row_blocks(tm, n):
    units = tm // HALO
    edges = [HALO * (units * k // n) for k in range(n + 1)]
    return list(zip(edges[:-1], edges[1:]))


def _ffn_up_kernel(u_ref, wa_ref, wb_ref, cw_ref, cb_ref, out_ref, w_scr):
    @pl.when(pl.program_id(1) == 0)
    def _():
        w_scr[0] = wa_ref[...].astype(BF16)
        w_scr[1] = wb_ref[...].astype(BF16)

    a_prev = jnp.zeros((HALO, out_ref.shape[1]), F32)
    for r0, r1 in _row_blocks(L, FFN_ROWS):
        u = u_ref[r0:r1, :]
        a = _dot(u, w_scr[0])
        pos = r0 + lax.broadcasted_iota(jnp.int32, (r1 - r0, 1), 0)
        conv = _causal_conv3(a_prev, a, cw_ref[...], pos) + cb_ref[...]
        gelu = 0.5 * conv * (1.0 + lax.erf(conv * (2.0 ** -0.5)))
        out_ref[r0:r1, :] = (gelu * _dot(u, w_scr[1])).astype(BF16)
        a_prev = a[r1 - r0 - HALO:, :]


def _ffn_up(u, w_up, layer, conv_w, conv_b, tn=512):
    m = u.shape[0]
    nb = D_FF // tn
    tm = L
    return pl.pallas_call(
        _ffn_up_kernel,
        grid=(nb, m // tm),
        in_specs=[
            pl.BlockSpec((tm, D_MODEL), lambda j, i: (i, 0)),
            pl.BlockSpec((None, D_MODEL, tn), lambda j, i: (layer, 0, j)),
            pl.BlockSpec((None, D_MODEL, tn), lambda j, i: (layer, 0, nb + j)),
            pl.BlockSpec((3, tn), lambda j, i: (0, j)),
            pl.BlockSpec((1, tn), lambda j, i: (0, j)),
        ],
        out_specs=pl.BlockSpec((tm, tn), lambda j, i: (i, j)),
        out_shape=jax.ShapeDtypeStruct((m, D_FF), BF16),
        scratch_shapes=[pltpu.VMEM((2, D_MODEL, tn), BF16)],
        compiler_params=_params(("parallel", "arbitrary")),
        name="ffn_up",
    )(u, w_up, w_up, conv_w, conv_b)


def _ffn_down_kernel(final, act_ref, w_ref, h_ref, g_ref, *out_refs):
    hn = h_ref[...] + _dot(act_ref[...], w_ref[...])
    y = _rms(hn, g_ref[...])
    if final:
        out_refs[0][...] = y
    else:
        out_refs[0][...] = hn
        out_refs[1][...] = y.astype(BF16)


def _ffn_down(act, w_down, layer, h, g, final):
    m = h.shape[0]
    if final:
        tm = 256
        m = m // L * SEQ
        per_batch = SEQ // tm

        def stream_row(i):
            return pl.multiple_of(i * tm + N_META * (i // per_batch + 1), N_META)

        act_spec = pl.BlockSpec((pl.Element(tm), pl.Element(D_FF)), lambda i: (stream_row(i), 0))
        h_spec = pl.BlockSpec((pl.Element(tm), pl.Element(D_MODEL)), lambda i: (stream_row(i), 0))
        row = pl.BlockSpec((tm, D_MODEL), lambda i: (i, 0))
        out_specs = [row]
        out_shape = [jax.ShapeDtypeStruct((m, D_MODEL), F32)]
    else:
        tm = 384
        row = pl.BlockSpec((tm, D_MODEL), lambda i: (i, 0))
        act_spec = pl.BlockSpec((tm, D_FF), lambda i: (i, 0))
        h_spec = row
        out_specs = [row, row]
        out_shape = [jax.ShapeDtypeStruct((m, D_MODEL), F32),
                     jax.ShapeDtypeStruct((m, D_MODEL), BF16)]
    return pl.pallas_call(
        functools.partial(_ffn_down_kernel, final),
        grid=(m // tm,),
        in_specs=[
            act_spec,
            _resident((None, D_FF, D_MODEL), lambda i: (layer, 0, 0)),
            h_spec,
            _resident((1, D_MODEL), lambda i: (0, 0)),
        ],
        out_specs=out_specs,
        out_shape=out_shape,
        compiler_params=_params(("parallel",)),
        name="ffn_down_final" if final else "ffn_down",
    )(act, w_down, h, g)


def kernel(x, meta, norm_mix, w_in, conv_a, b_if, ml_norm, da_lambda, da_norm, w_br_a, w_br_m, w_br_d,
           w_out, norm_ffn, w_up, conv_ffn, conv_ffn_b, w_down, norm_f):
    bsz, seq, d = x.shape
    assert (seq, d) == (SEQ, D_MODEL)
    depth = w_in.shape[0]
    w_in_t = jnp.swapaxes(w_in, 1, 2).reshape(depth * W_IN, D_MODEL)
    wa, wm, wd, wo, wdn = (w.astype(BF16) for w in (w_br_a, w_br_m, w_br_d, w_out, w_down))
    h, u = _embed(x.reshape(bsz * SEQ, D_MODEL), meta, norm_mix[0][None], bsz)
    for i in range(depth):
        z = _inproj(u, w_in_t, i)
        qt, vt, zift = _attn_proj(u, w_in_t, i, bsz)
        bias = jnp.broadcast_to(b_if[i].reshape(2 * ML_HEADS, 1), (2 * ML_HEADS, LANES))
        hm = _mlstm(z, zift, bias, ml_norm[i][None], bsz)
        hd = _attn(z, qt, vt, da_lambda[i], da_norm[i][:, None], i, bsz)
        h, u = _merge(z, hm, hd, h, i, conv_a[i], wa, wm, wd, wo, norm_ffn[i][None])
        act = _ffn_up(u, w_up, i, conv_ffn[i], conv_ffn_b[i][None])
        final = i == depth - 1
        g_next = norm_f if final else norm_mix[i + 1]
        outs = _ffn_down(act, wdn, i, h, g_next[None], final)
        if final:
            y = outs[0]
        else:
            h, u = outs
    return y.reshape(bsz, SEQ, D_MODEL)
```
